```python
import math
import jax
import jax.numpy as jnp
from jax import lax
import numpy as np

D_MODEL = 1024
BATCH = 32
SEQ = 2048
DEPTH = 2

GRID_W = 64
CTX_LEN = 256
EPS = 1e-6
CONV_W = 4
CONV_PAD = (CONV_W // 2, CONV_W - 1 - CONV_W // 2)

LRU_WIDTH = D_MODEL // 2
LRU_BLOCKS = 8
LRU_BLOCK = LRU_WIDTH // LRU_BLOCKS
LRU_C = 8.0

DN_HEAD_DIM = 128
DN_HEADS = D_MODEL // DN_HEAD_DIM
DN_WIDTH = DN_HEADS * DN_HEAD_DIM
DN_CHUNK = 64

S5_WIDTH = D_MODEL // 2
S5_GROUP = 16
S5_GROUPS = S5_WIDTH // S5_GROUP
S5_STATE = 64

N_BRANCH = 3

N_EXPERTS = 32
TOP_K = 4
D_EXPERT = D_MODEL
SWIGLU_LIMIT = 7.0
SWIGLU_ALPHA = 1.702
MOE_BLOCK = 512

IN_SIZES = (LRU_WIDTH, LRU_WIDTH, DN_WIDTH, DN_WIDTH, DN_WIDTH, DN_WIDTH,
            2 * DN_HEADS, 2 * DN_HEADS, S5_WIDTH, N_BRANCH * D_MODEL)
D_IN = sum(IN_SIZES)

kernel_name = "hybrid_lru_deltanet_s5_moe_dit"


def rmsnorm(x, g):
    xf = x.astype(jnp.float32)
    y = xf * lax.rsqrt(jnp.mean(xf * xf, axis=-1, keepdims=True) + EPS)
    return (y * g.astype(jnp.float32)).astype(x.dtype)


def l2norm(x):
    return x * lax.rsqrt(jnp.sum(x * x, axis=-1, keepdims=True) + EPS)


def modulate(h, shift, scale):
    return h * (1.0 + scale) + shift


def dwconv(x, w, rows):
    bsz, t, ch = x.shape
    xr = x if rows is None else x.reshape(bsz * rows, GRID_W, ch)
    y = lax.conv_general_dilated(xr, w[:, None, :].astype(x.dtype), window_strides=(1,), padding=(CONV_PAD,),
                                 dimension_numbers=("NWC", "WIO", "NWC"), feature_group_count=ch)
    return y.reshape(bsz, t, ch)


def _affine_combine(e1, e2):
    a1, b1 = e1
    a2, b2 = e2
    return a1 * a2, a2 * b1 + b2


def linear_scan(a, b, h0):
    b = b.at[:, 0].add(a[:, 0] * h0)
    return lax.associative_scan(_affine_combine, (a, b), axis=1)[1]


def _complex_affine_combine(e1, e2):
    a1r, a1i, b1r, b1i = e1
    a2r, a2i, b2r, b2i = e2
    return (a1r * a2r - a1i * a2i, a1r * a2i + a1i * a2r,
            a2r * b1r - a2i * b1i + b2r, a2r * b1i + a2i * b1r + b2i)


def rglru_scan(xc, w_gate, b_gate, lam, h0):
    bsz, t, w = xc.shape
    xb = xc.reshape(bsz, t, LRU_BLOCKS, LRU_BLOCK)
    gates = jnp.einsum("btni,gnij->gbtnj", xb, w_gate).reshape(2, bsz, t, w) + b_gate[:, None, None, :]
    r = jax.nn.sigmoid(gates[0])
    i = jax.nn.sigmoid(gates[1])
    log_a = -LRU_C * r * jax.nn.softplus(-lam)
    a = jnp.exp(log_a)
    b = jnp.sqrt(-jnp.expm1(2.0 * log_a)) * (i * xc)
    return linear_scan(a, b, h0)


def rglru_mixer(a_x, a_y, conv_w, conv_b, w_gate, b_gate, lam, h0, rows):
    f32 = jnp.float32
    xc = (dwconv(a_x, conv_w, rows) + conv_b).astype(f32)
    w_gate, b_gate, lam = w_gate.astype(f32), b_gate.astype(f32), lam.astype(f32)
    h_f = rglru_scan(xc, w_gate[0], b_gate[0], lam[0], h0[0])
    h_b = rglru_scan(jnp.flip(xc, 1), w_gate[1], b_gate[1], lam[1], h0[1])
    y = (h_f + jnp.flip(h_b, 1)).astype(a_x.dtype) * jax.nn.gelu(a_y)
    return y, (h_f[:, -1], h_b[:, -1])


def delta_chunk(q, k, v, g, beta, s0):
    bsz, t, nh, _ = q.shape
    n = t // DN_CHUNK

    def to_chunks(u):
        u = u.reshape((bsz, n, DN_CHUNK, nh) + u.shape[3:])
        return jnp.moveaxis(u, 3, 1)

    q, k, v, g, beta = (to_chunks(u) for u in (q, k, v, g, beta))
    dv = v.shape[-1]
    g = jnp.cumsum(g, axis=-1)
    idx = jnp.arange(DN_CHUNK)
    incl = idx[:, None] >= idx[None, :]
    strict = idx[:, None] > idx[None, :]
    decay = jnp.exp(jnp.where(incl, g[..., :, None] - g[..., None, :], -jnp.inf))
    k_beta = k * beta[..., None]
    a_low = jnp.where(strict, jnp.einsum("bhnid,bhnjd->bhnij", k_beta, k) * decay, 0.0)
    eye = jnp.eye(DN_CHUNK, dtype=q.dtype)
    t_inv = lax.linalg.triangular_solve(eye + a_low, jnp.broadcast_to(eye, a_low.shape),
                                        left_side=True, lower=True, unit_diagonal=True)
    u = jnp.einsum("bhnij,bhnjd->bhnid", t_inv, v * beta[..., None])
    w = jnp.einsum("bhnij,bhnjd->bhnid", t_inv, k_beta * jnp.exp(g)[..., None])
    attn = jnp.where(incl, jnp.einsum("bhnid,bhnjd->bhnij", q, k) * decay, 0.0)
    q_dec = q * jnp.exp(g)[..., None]
    k_dec = k * jnp.exp(g[..., -1:] - g)[..., None]
    g_tot = jnp.exp(g[..., -1])

    def step(s, inp):
        u_i, w_i, qd_i, kd_i, at_i, gt_i = inp
        v_new = u_i - jnp.einsum("bhck,bhkv->bhcv", w_i, s)
        o_i = jnp.einsum("bhck,bhkv->bhcv", qd_i, s) + jnp.einsum("bhij,bhjv->bhiv", at_i, v_new)
        s = s * gt_i[..., None, None] + jnp.einsum("bhck,bhcv->bhkv", kd_i, v_new)
        return s, o_i

    xs = tuple(jnp.moveaxis(a, 2, 0) for a in (u, w, q_dec, k_dec, attn, g_tot))
    s_fin, o = lax.scan(step, s0, xs)
    o = jnp.transpose(o, (1, 0, 3, 2, 4)).reshape(bsz, t, nh, dv)
    return o, s_fin


def deltanet_mixer(q, k, v, z, beta_raw, alpha_raw, conv_w, a_log, dt_bias, norm_g, s0, rows):
    f32 = jnp.float32
    bsz, t, _ = q.shape
    qkv = jax.nn.silu(dwconv(jnp.concatenate([q, k, v], axis=-1), conv_w, rows)).astype(f32)

    def heads(u):
        return u.reshape(bsz, t, DN_HEADS, DN_HEAD_DIM)

    q, k, v = (heads(u) for u in jnp.split(qkv, 3, axis=-1))
    q = l2norm(q) * DN_HEAD_DIM ** -0.5
    k = l2norm(k)
    beta = jax.nn.sigmoid(beta_raw.astype(f32)).reshape(bsz, t, 2, DN_HEADS)
    g = -jnp.exp(a_log.astype(f32)) * jax.nn.softplus(
        alpha_raw.astype(f32).reshape(bsz, t, 2, DN_HEADS) + dt_bias.astype(f32))
    o_f, s_f = delta_chunk(q, k, v, g[:, :, 0], beta[:, :, 0], s0[0])
    fl = lambda u: jnp.flip(u, 1)
    o_b, s_b = delta_chunk(fl(q), fl(k), fl(v), fl(g[:, :, 1]), fl(beta[:, :, 1]), s0[1])
    o = o_f + fl(o_b)
    o = rmsnorm(o, norm_g) * jax.nn.silu(heads(z.astype(f32)))
    return o.reshape(bsz, t, DN_WIDTH).astype(z.dtype), (s_f, s_b)


def s5_scan(u, lam_re, lam_im, log_dt, b_re, b_im, c_re, c_im, h0):
    t = u.shape[1]
    dt = jnp.exp(log_dt)[:, None]
    mag = jnp.exp(lam_re * dt)
    ar, ai = mag * jnp.cos(lam_im * dt), mag * jnp.sin(lam_im * dt)
    den = lam_re * lam_re + lam_im * lam_im
    fr = ((ar - 1.0) * lam_re + ai * lam_im) / den
    fi = (ai * lam_re - (ar - 1.0) * lam_im) / den
    bb_re = fr[..., None] * b_re - fi[..., None] * b_im
    bb_im = fr[..., None] * b_im + fi[..., None] * b_re
    xr = jnp.einsum("btgh,gph->btgp", u, bb_re)
    xi = jnp.einsum("btgh,gph->btgp", u, bb_im)
    h0r, h0i = h0
    xr = xr.at[:, 0].add(ar * h0r - ai * h0i)
    xi = xi.at[:, 0].add(ar * h0i + ai * h0r)
    shape = (1, t) + ar.shape
    _, _, hr, hi = lax.associative_scan(
        _complex_affine_combine,
        (jnp.broadcast_to(ar, shape), jnp.broadcast_to(ai, shape), xr, xi), axis=1)
    y = jnp.einsum("btgp,ghp->btgh", hr, c_re) - jnp.einsum("btgp,ghp->btgh", hi, c_im)
    return y, (hr[:, -1], hi[:, -1])


def s5_mixer(u, lam_re, lam_im, log_dt, b_re, b_im, c_re, c_im, d_skip, w_glu, b_glu, h0):
    f32 = jnp.float32
    bsz, t, _ = u.shape
    uf = u.astype(f32)
    ug = uf.reshape(bsz, t, S5_GROUPS, S5_GROUP)
    prm = [a.astype(f32) for a in (lam_re, lam_im, log_dt, b_re, b_im, c_re, c_im)]
    y_f, s_f = s5_scan(ug, *(a[0] for a in prm), h0[0])
    y_b, s_b = s5_scan(jnp.flip(ug, 1), *(a[1] for a in prm), h0[1])
    y = (y_f + jnp.flip(y_b, 1)).reshape(bsz, t, S5_WIDTH) + d_skip.astype(f32) * uf
    y = jax.nn.gelu(y)
    y = y * jax.nn.sigmoid(y @ w_glu.astype(f32) + b_glu.astype(f32))
    return y.astype(u.dtype), (s_f, s_b)


def mix_stream(h, p, init, rows, emit):
    bsz, t, _ = h.shape
    if init is None:
        zeros = lambda *s: jnp.zeros(s, jnp.float32)
        init = ((zeros(bsz, LRU_WIDTH),) * 2,
                (zeros(bsz, DN_HEADS, DN_HEAD_DIM, DN_HEAD_DIM),) * 2,
                ((zeros(bsz, S5_GROUPS, S5_STATE),) * 2,) * 2)
    proj = h @ p["w_in"]
    a_x, a_y, q, k, v, z, beta_raw, alpha_raw, s5_u, gate_raw = jnp.split(
        proj, np.cumsum(IN_SIZES)[:-1].tolist(), axis=-1)
    y_a, st_a = rglru_mixer(a_x, a_y, p["lru_conv_w"], p["lru_conv_b"], p["lru_w_gate"], p["lru_b_gate"],
                            p["lru_lam"], init[0], rows)
    y_b, st_b = deltanet_mixer(q, k, v, z, beta_raw, alpha_raw, p["dn_conv_w"], p["dn_a_log"],
                               p["dn_dt_bias"], p["dn_norm_g"], init[1], rows)
    y_c, st_c = s5_mixer(s5_u, p["s5_lam_re"], p["s5_lam_im"], p["s5_log_dt"], p["s5_b_re"], p["s5_b_im"],
                         p["s5_c_re"], p["s5_c_im"], p["s5_d"], p["s5_w_glu"], p["s5_b_glu"], init[2])
    states = (st_a, st_b, st_c)
    if not emit:
        return None, states
    gates = jax.nn.sigmoid(gate_raw).reshape(bsz, t, N_BRANCH, D_MODEL)
    merged = (gates[..., 0, :] * (y_a @ p["w_br_a"])
              + gates[..., 1, :] * (y_b @ p["w_br_b"])
              + gates[..., 2, :] * (y_c @ p["w_br_c"]))
    return merged @ p["w_out"], states


def moe(h, w_router, b_router, w_e1, b_e1, w_e2, b_e2):
    n, d = h.shape
    nk = n * TOP_K
    logits = (h @ w_router + b_router).astype(jnp.float32)
    top_val, top_idx = lax.top_k(logits, TOP_K)
    weights = jax.nn.softmax(top_val, axis=-1).reshape(-1)
    flat_e = top_idx.reshape(-1)
    order = jnp.argsort(flat_e)
    e_sorted = flat_e[order]
    tok = order // TOP_K
    sizes = jnp.bincount(flat_e, length=N_EXPERTS).astype(jnp.int32)
    padded = ((sizes + MOE_BLOCK - 1) // MOE_BLOCK) * MOE_BLOCK
    ends_pad = jnp.cumsum(padded)
    starts_pad = ends_pad - padded
    starts = jnp.cumsum(sizes) - sizes
    dest = starts_pad[e_sorted] + (jnp.arange(nk, dtype=jnp.int32) - starts[e_sorted])
    n_blocks = -(-nk // MOE_BLOCK) + N_EXPERTS
    block_e = jnp.minimum(jnp.searchsorted(ends_pad, jnp.arange(n_blocks) * MOE_BLOCK, side="right"),
                          N_EXPERTS - 1).astype(jnp.int32)
    xs = jnp.zeros((n_blocks * MOE_BLOCK, d), h.dtype).at[dest].set(h[tok])

    def expert_block(args):
        xb, e = args
        gu = xb @ w_e1[e] + b_e1[e]
        glu, lin = jnp.split(gu, 2, axis=-1)
        glu = jnp.minimum(glu, SWIGLU_LIMIT)
        lin = jnp.clip(lin, -SWIGLU_LIMIT, SWIGLU_LIMIT)
        act = glu * jax.nn.sigmoid(SWIGLU_ALPHA * glu) * (lin + 1.0)
        return act @ w_e2[e] + b_e2[e]

    yb = lax.map(expert_block, (xs.reshape(n_blocks, MOE_BLOCK, d), block_e))
    y = yb.reshape(n_blocks * MOE_BLOCK, d)[dest] * weights[order][:, None].astype(h.dtype)
    return jax.ops.segment_sum(y, tok, num_segments=n)


def setup_inputs(seed: int = 0) -> dict:
    key = jax.random.key(seed)
    keys = list(jax.random.split(key, 48))

    def nrm(shape, std):
        return std * jax.random.normal(keys.pop(), shape, jnp.float32)

    def unif(shape, lo, hi):
        return jax.random.uniform(keys.pop(), shape, jnp.float32, lo, hi)

    L, D = DEPTH, D_MODEL
    G, P, H = S5_GROUPS, S5_STATE, S5_GROUP
    lru_base = unif((L, 2, LRU_WIDTH), 0.9, 0.999) ** (1.0 / LRU_C)
    dn_dt = jnp.exp(unif((L, 2, DN_HEADS), math.log(1e-3), math.log(1e-1)))
    return {
        "x": nrm((BATCH, SEQ, D), 1.0),
        "c": nrm((BATCH, D), 1.0),
        "ctx": nrm((BATCH, CTX_LEN, D), 1.0),
        "c_ctx": nrm((D,), 1.0),
        "w_ada": nrm((L, D, 6 * D), 0.5 * D ** -0.5),
        "b_ada": nrm((L, 6 * D), 0.02),
        "g_mix": 1.0 + nrm((L, D), 0.02),
        "g_ffn": 1.0 + nrm((L, D), 0.02),
        "w_in": nrm((L, D, D_IN), D ** -0.5),
        "lru_conv_w": nrm((L, CONV_W, LRU_WIDTH), CONV_W ** -0.5),
        "lru_conv_b": nrm((L, LRU_WIDTH), 0.02),
        "lru_w_gate": nrm((L, 2, 2, LRU_BLOCKS, LRU_BLOCK, LRU_BLOCK), LRU_BLOCK ** -0.5),
        "lru_b_gate": nrm((L, 2, 2, LRU_WIDTH), 0.02),
        "lru_lam": jnp.log(lru_base) - jnp.log1p(-lru_base),
        "dn_conv_w": nrm((L, CONV_W, 3 * DN_WIDTH), CONV_W ** -0.5),
        "dn_a_log": jnp.log(unif((L, 2, DN_HEADS), 1.0, 16.0)),
        "dn_dt_bias": dn_dt + jnp.log(-jnp.expm1(-dn_dt)),
        "dn_norm_g": 1.0 + nrm((L, DN_HEAD_DIM), 0.02),
        "s5_lam_re": -0.5 + nrm((L, 2, G, P), 0.01),
        "s5_lam_im": math.pi * jnp.arange(P, dtype=jnp.float32) + nrm((L, 2, G, P), 0.01),
        "s5_log_dt": unif((L, 2, G), math.log(1e-3), math.log(1e-1)),
        "s5_b_re": nrm((L, 2, G, P, H), (2 * H) ** -0.5),
        "s5_b_im": nrm((L, 2, G, P, H), (2 * H) ** -0.5),
        "s5_c_re": nrm((L, 2, G, H, P), P ** -0.5),
        "s5_c_im": nrm((L, 2, G, H, P), P ** -0.5),
        "s5_d": nrm((L, S5_WIDTH), 1.0),
        "s5_w_glu": nrm((L, S5_WIDTH, S5_WIDTH), S5_WIDTH ** -0.5),
        "s5_b_glu": nrm((L, S5_WIDTH), 0.02),
        "w_br_a": nrm((L, LRU_WIDTH, D), LRU_WIDTH ** -0.5),
        "w_br_b": nrm((L, DN_WIDTH, D), DN_WIDTH ** -0.5),
        "w_br_c": nrm((L, S5_WIDTH, D), S5_WIDTH ** -0.5),
        "w_out": nrm((L, D, D), D ** -0.5),
        "w_router": nrm((L, D, N_EXPERTS), D ** -0.5),
        "b_router": nrm((L, N_EXPERTS), 0.01),
        "w_e1": nrm((L, N_EXPERTS, D, 2 * D_EXPERT), D ** -0.5),
        "b_e1": nrm((L, N_EXPERTS, 2 * D_EXPERT), 0.01),
        "w_e2": nrm((L, N_EXPERTS, D_EXPERT, D), D_EXPERT ** -0.5),
        "b_e2": nrm((L, N_EXPERTS, D), 0.01),
        "g_final": 1.0 + nrm((D,), 0.02),
    }


def reference(x, c, ctx, c_ctx, w_ada, b_ada, g_mix, g_ffn, w_in, lru_conv_w, lru_conv_b, lru_w_gate,
              lru_b_gate, lru_lam, dn_conv_w, dn_a_log, dn_dt_bias, dn_norm_g, s5_lam_re, s5_lam_im,
              s5_log_dt, s5_b_re, s5_b_im, s5_c_re, s5_c_im, s5_d, s5_w_glu, s5_b_glu, w_br_a, w_br_b,
              w_br_c, w_out, w_router, b_router, w_e1, b_e1, w_e2, b_e2, g_final):
    rows = x.shape[1] // GRID_W
    s_c = jax.nn.silu(c)
    s_ctx = jax.nn.silu(c_ctx)
    x_lat, x_ctx = x, ctx
    for l in range(DEPTH):
        last = l == DEPTH - 1
        p = {
            "w_in": w_in[l], "lru_conv_w": lru_conv_w[l], "lru_conv_b": lru_conv_b[l],
            "lru_w_gate": lru_w_gate[l], "lru_b_gate": lru_b_gate[l], "lru_lam": lru_lam[l],
            "dn_conv_w": dn_conv_w[l], "dn_a_log": dn_a_log[l], "dn_dt_bias": dn_dt_bias[l],
            "dn_norm_g": dn_norm_g[l], "s5_lam_re": s5_lam_re[l], "s5_lam_im": s5_lam_im[l],
            "s5_log_dt": s5_log_dt[l], "s5_b_re": s5_b_re[l], "s5_b_im": s5_b_im[l],
            "s5_c_re": s5_c_re[l], "s5_c_im": s5_c_im[l], "s5_d": s5_d[l], "s5_w_glu": s5_w_glu[l],
            "s5_b_glu": s5_b_glu[l], "w_br_a": w_br_a[l], "w_br_b": w_br_b[l], "w_br_c": w_br_c[l],
            "w_out": w_out[l],
        }
        m_lat = jnp.split((s_c @ w_ada[l] + b_ada[l])[:, None, :], 6, axis=-1)
        m_ctx = jnp.split(s_ctx @ w_ada[l] + b_ada[l], 6, axis=-1)

        h_ctx = modulate(rmsnorm(x_ctx, g_mix[l]), m_ctx[0], m_ctx[1])
        o_ctx, ctx_states = mix_stream(h_ctx, p, None, None, not last)
        h_lat = modulate(rmsnorm(x_lat, g_mix[l]), m_lat[0], m_lat[1])
        o_lat, _ = mix_stream(h_lat, p, ctx_states, rows, True)
        x_lat = x_lat + m_lat[2] * o_lat

        h_lat = modulate(rmsnorm(x_lat, g_ffn[l]), m_lat[3], m_lat[4])
        if last:
            y_lat = moe(h_lat.reshape(-1, D_MODEL), w_router[l], b_router[l], w_e1[l], b_e1[l], w_e2[l], b_e2[l])
            x_lat = x_lat + m_lat[5] * y_lat.reshape(h_lat.shape)
        else:
            x_ctx = x_ctx + m_ctx[2] * o_ctx
            h_ctx = modulate(rmsnorm(x_ctx, g_ffn[l]), m_ctx[3], m_ctx[4])
            n_ctx = h_ctx.shape[0] * h_ctx.shape[1]
            y = moe(jnp.concatenate([h_ctx.reshape(-1, D_MODEL), h_lat.reshape(-1, D_MODEL)], axis=0),
                    w_router[l], b_router[l], w_e1[l], b_e1[l], w_e2[l], b_e2[l])
            x_ctx = x_ctx + m_ctx[5] * y[:n_ctx].reshape(h_ctx.shape)
            x_lat = x_lat + m_lat[5] * y[n_ctx:].reshape(h_lat.shape)
    return rmsnorm(x_lat, g_final)
```

```python
import functools
import math

import jax
import jax.numpy as jnp
from jax import lax
from jax.experimental import pallas as pl
from jax.experimental.pallas import tpu as pltpu

F32 = jnp.float32
BF16 = jnp.bfloat16

D_MODEL = 1024
EPS = 1e-6
CONV_W = 4
CONV_LEFT = CONV_W // 2
GRID_W = 64

LRU_WIDTH = 512
LRU_BLOCKS = 8
LRU_BLOCK = LRU_WIDTH // LRU_BLOCKS
LRU_C = 8.0

DN_HEAD_DIM = 128
DN_HEADS = 8
DN_WIDTH = DN_HEADS * DN_HEAD_DIM
DN_CHUNK = 64

S5_WIDTH = 512
S5_GROUP = 16
S5_GROUPS = 32
S5_STATE = 64
S5_LANES = S5_GROUPS * S5_STATE
S5_SPLIT = 4

N_EXPERTS = 32
TOP_K = 4
D_EXPERT = 1024
SWIGLU_LIMIT = 7.0
SWIGLU_ALPHA = 1.702
MOE_BLOCK = 512

COL_AX, COL_AY, COL_Q, COL_K, COL_V, COL_Z = 0, 512, 1024, 2048, 3072, 4096
COL_BA, COL_U, COL_GATE, D_IN_PAD = 5120, 5632, 6144, 9216
D_IN = 8736

SUBLANES = 8
VMEM_LIMIT = 52 * 1024 * 1024


def _cparams(sem):
    return pltpu.CompilerParams(dimension_semantics=sem, vmem_limit_bytes=VMEM_LIMIT)


def _sigmoid(x):
    return 1.0 / (1.0 + jnp.exp(-x))


def _softplus(x):
    return jnp.maximum(x, 0.0) + jnp.log1p(jnp.exp(-jnp.abs(x)))


def _silu(x):
    return x * _sigmoid(x)


def _gelu_tanh(x):
    return 0.5 * x * (1.0 + jnp.tanh(math.sqrt(2.0 / math.pi) * (x + 0.044715 * x * x * x)))


def _dot(a, b):
    return jnp.dot(a.astype(BF16), b.astype(BF16), preferred_element_type=F32)


def _ada_kernel(s_ref, w_ref, b_ref, o_ref):
    o_ref[...] = _dot(_silu(s_ref[...]), w_ref[...]) + b_ref[...]


def ada_modulation(s, w_ada, b_ada):
    n_layers, d, n = w_ada.shape
    r = s.shape[0]
    tn = 1536
    return pl.pallas_call(
        _ada_kernel,
        out_shape=jax.ShapeDtypeStruct((n_layers, r, n), F32),
        grid=(n_layers, n // tn),
        in_specs=[pl.BlockSpec((r, d), lambda l, j: (0, 0)),
                  pl.BlockSpec((None, d, tn), lambda l, j: (l, 0, j)),
                  pl.BlockSpec((None, 1, tn), lambda l, j: (l, 0, j))],
        out_specs=pl.BlockSpec((None, r, tn), lambda l, j: (l, 0, j)),
        compiler_params=_cparams(("arbitrary", "arbitrary")),
        name="ada_modulation",
    )(s, w_ada, b_ada.reshape(n_layers, 1, n))


def _norm_mod(x, g, shift, scale):
    y = x * lax.rsqrt(jnp.mean(x * x, axis=-1, keepdims=True) + EPS)
    return (y * g) * (1.0 + scale) + shift


def _inproj_kernel(x_ref, mod_ref, g_ref, w_ref, o_ref, h_ref, *, shift_row):
    @pl.when(pl.program_id(2) == 0)
    def _():
        mod = mod_ref[...]
        h = _norm_mod(x_ref[...], g_ref[...], mod[shift_row:shift_row + 1], mod[shift_row + 1:shift_row + 2])
        h_ref[...] = h.astype(BF16)

    o_ref[...] = jnp.dot(h_ref[...], w_ref[...], preferred_element_type=F32)


def inproj(x, mod, g, w_pad):
    b, t, d = x.shape
    n = w_pad.shape[1]
    tm = min(t, 1024)
    tn = 1536
    return pl.pallas_call(
        functools.partial(_inproj_kernel, shift_row=0),
        out_shape=jax.ShapeDtypeStruct((b, t, n), F32),
        grid=(b, t // tm, n // tn),
        in_specs=[pl.BlockSpec((None, tm, d), lambda i, r, j: (i, r, 0)),
                  pl.BlockSpec((None, 6, d), lambda i, r, j: (i, 0, 0)),
                  pl.BlockSpec((1, d), lambda i, r, j: (0, 0)),
                  pl.BlockSpec((d, tn), lambda i, r, j: (0, j))],
        out_specs=pl.BlockSpec((None, tm, tn), lambda i, r, j: (i, r, j)),
        scratch_shapes=[pltpu.VMEM((tm, d), BF16)],
        compiler_params=_cparams(("arbitrary", "arbitrary", "arbitrary")),
        name="inproj",
    )(x, mod, g, w_pad)


def _conv_time_major(x, w, seg):
    tc = x.shape[0]
    t_in_seg = lax.broadcasted_iota(jnp.int32, x.shape, 0) % seg
    acc = x * w[CONV_LEFT:CONV_LEFT + 1][None]
    for j in range(CONV_W):
        off = j - CONV_LEFT
        if off == 0:
            continue
        if off < 0:
            sh = jnp.concatenate([jnp.zeros((-off,) + x.shape[1:], x.dtype), x[:tc + off]], axis=0)
            ok = t_in_seg >= -off
        else:
            sh = jnp.concatenate([x[off:], jnp.zeros((off,) + x.shape[1:], x.dtype)], axis=0)
            ok = t_in_seg < seg - off
        acc = acc + jnp.where(ok, sh, 0.0) * w[j:j + 1][None]
    return acc


def _lru_kernel(x_ref, cw_ref, cb_ref, wg_ref, bg_ref, lam_ref, h0_ref, *rest, seg, reverse, add_prev):
    if add_prev:
        prev_ref, o_ref, hfin_ref, a_ref, b_ref, h_ref = rest
    else:
        o_ref, hfin_ref, a_ref, b_ref, h_ref = rest
    tc, bc, w = x_ref.shape
    step = pl.program_id(1)

    @pl.when(step == 0)
    def _():
        h_ref[...] = h0_ref[...]

    xc = _conv_time_major(x_ref[...], cw_ref[...], seg) + cb_ref[...][None]
    gates = _dot(xc.reshape(tc * bc, w), wg_ref[...]) + bg_ref[...]
    r = _sigmoid(gates[:, :w]).reshape(tc, bc, w)
    i = _sigmoid(gates[:, w:]).reshape(tc, bc, w)
    log_a = (-LRU_C * _softplus(-lam_ref[...]))[None] * r
    a_ref[...] = jnp.exp(log_a)
    b_ref[...] = jnp.sqrt(1.0 - jnp.exp(2.0 * log_a)) * (i * xc)

    def body(s, h):
        t = tc - 1 - s if reverse else s
        h = a_ref[t] * h + b_ref[t]
        if add_prev:
            o_ref[t] = h + prev_ref[t]
        else:
            o_ref[t] = h
        return h

    h = lax.fori_loop(0, tc, body, h_ref[...], unroll=8)
    h_ref[...] = h
    hfin_ref[...] = h


def lru_scan(x_tm, conv_w, conv_b, wg, bg, lam, h0, prev, *, seg, reverse):
    t, b, w = x_tm.shape
    tc = min(t, 256)
    bc = SUBLANES
    nt = t // tc
    tmap = (lambda i, s: (nt - 1 - s, i, 0)) if reverse else (lambda i, s: (s, i, 0))
    blk = pl.BlockSpec((tc, bc, w), tmap)
    const = lambda shape: pl.BlockSpec(shape, lambda i, s: (0,) * len(shape))
    in_specs = [blk, const((CONV_W, w)), const((1, w)), const((w, 2 * w)), const((1, 2 * w)), const((1, w)),
                pl.BlockSpec((bc, w), lambda i, s: (i, 0))]
    args = [x_tm, conv_w, conv_b, wg, bg, lam, h0]
    if prev is not None:
        in_specs.append(blk)
        args.append(prev)
    return pl.pallas_call(
        functools.partial(_lru_kernel, seg=seg, reverse=reverse, add_prev=prev is not None),
        out_shape=(jax.ShapeDtypeStruct((t, b, w), F32), jax.ShapeDtypeStruct((b, w), F32)),
        grid=(b // bc, nt),
        in_specs=in_specs,
        out_specs=(blk, pl.BlockSpec((bc, w), lambda i, s: (i, 0))),
        scratch_shapes=[pltpu.VMEM((tc, bc, w), F32), pltpu.VMEM((tc, bc, w), F32), pltpu.VMEM((bc, w), F32)],
        compiler_params=_cparams(("arbitrary", "arbitrary")),
        name="lru_scan_bwd" if reverse else "lru_scan_fwd",
    )(*args)


def _lru_gate_dense(w_gate):
    eye = jnp.eye(LRU_BLOCKS, dtype=w_gate.dtype)
    dense = jnp.einsum("gnij,nm->gnimj", w_gate, eye).reshape(2, LRU_WIDTH, LRU_WIDTH)
    return jnp.concatenate([dense[0], dense[1]], axis=1).astype(BF16)


def _s5_kernel(u_ref, bre_ref, bim_ref, cre_ref, cim_ref, ar_ref, ai_ref, h0r_ref, h0i_ref, *rest,
               reverse, add_prev):
    if add_prev:
        prev_ref, o_ref, fr_ref, fi_ref, xr_ref, xi_ref, hr_ref, hi_ref = rest
    else:
        o_ref, fr_ref, fi_ref, xr_ref, xi_ref, hr_ref, hi_ref = rest
    tc, bc, w = u_ref.shape
    blk_in = w // S5_SPLIT
    blk_st = S5_LANES // S5_SPLIT

    @pl.when(pl.program_id(1) == 0)
    def _():
        hr_ref[...] = h0r_ref[...]
        hi_ref[...] = h0i_ref[...]

    u2 = u_ref[...].reshape(tc * bc, w).astype(BF16)
    for j in range(S5_SPLIT):
        uj = u2[:, j * blk_in:(j + 1) * blk_in]
        xr_ref[:, :, j * blk_st:(j + 1) * blk_st] = jnp.dot(
            uj, bre_ref[j], preferred_element_type=F32).reshape(tc, bc, blk_st)
        xi_ref[:, :, j * blk_st:(j + 1) * blk_st] = jnp.dot(
            uj, bim_ref[j], preferred_element_type=F32).reshape(tc, bc, blk_st)

    ar = jnp.broadcast_to(ar_ref[...], (bc, S5_LANES))
    ai = jnp.broadcast_to(ai_ref[...], (bc, S5_LANES))

    def body(s, carry):
        hr, hi = carry
        t = tc - 1 - s if reverse else s
        nr = ar * hr - ai * hi + xr_ref[t]
        ni = ar * hi + ai * hr + xi_ref[t]
        xr_ref[t] = nr
        xi_ref[t] = ni
        return nr, ni

    hr, hi = lax.fori_loop(0, tc, body, (hr_ref[...], hi_ref[...]))
    hr_ref[...] = hr
    hi_ref[...] = hi
    fr_ref[...] = hr
    fi_ref[...] = hi

    for j in range(S5_SPLIT):
        hrj = xr_ref[:, :, j * blk_st:(j + 1) * blk_st].reshape(tc * bc, blk_st).astype(BF16)
        hij = xi_ref[:, :, j * blk_st:(j + 1) * blk_st].reshape(tc * bc, blk_st).astype(BF16)
        y = (jnp.dot(hrj, cre_ref[j], preferred_element_type=F32)
             - jnp.dot(hij, cim_ref[j], preferred_element_type=F32)).reshape(tc, bc, blk_in)
        if add_prev:
            y = y + prev_ref[:, :, j * blk_in:(j + 1) * blk_in]
        o_ref[:, :, j * blk_in:(j + 1) * blk_in] = y


def s5_scan(u_tm, prm, h0r, h0i, prev, *, reverse):
    t, b, w = u_tm.shape
    tc = min(t, 64)
    bc = SUBLANES
    nt = t // tc
    tmap = (lambda i, s: (nt - 1 - s, i, 0)) if reverse else (lambda i, s: (s, i, 0))
    blk = pl.BlockSpec((tc, bc, w), tmap)
    const = lambda shape: pl.BlockSpec(shape, lambda i, s: (0,) * len(shape))
    st = pl.BlockSpec((bc, S5_LANES), lambda i, s: (i, 0))
    bre, bim, cre, cim, ar, ai = prm
    in_specs = [blk, const(bre.shape), const(bim.shape), const(cre.shape), const(cim.shape),
                const(ar.shape), const(ai.shape), st, st]
    args = [u_tm, bre, bim, cre, cim, ar, ai, h0r, h0i]
    if prev is not None:
        in_specs.append(blk)
        args.append(prev)
    st_shape = jax.ShapeDtypeStruct((b, S5_LANES), F32)
    return pl.pallas_call(
        functools.partial(_s5_kernel, reverse=reverse, add_prev=prev is not None),
        out_shape=(jax.ShapeDtypeStruct((t, b, w), F32), st_shape, st_shape),
        grid=(b // bc, nt),
        in_specs=in_specs,
        out_specs=(blk, st, st),
        scratch_shapes=[pltpu.VMEM((tc, bc, S5_LANES), F32), pltpu.VMEM((tc, bc, S5_LANES), F32),
                        pltpu.VMEM((bc, S5_LANES), F32), pltpu.VMEM((bc, S5_LANES), F32)],
        compiler_params=_cparams(("arbitrary", "arbitrary")),
        name="s5_scan_bwd" if reverse else "s5_scan_fwd",
    )(*args)


def _s5_params(lam_re, lam_im, log_dt, b_re, b_im, c_re, c_im):
    dt = jnp.exp(log_dt)[:, None]
    mag = jnp.exp(lam_re * dt)
    ar, ai = mag * jnp.cos(lam_im * dt), mag * jnp.sin(lam_im * dt)
    den = lam_re * lam_re + lam_im * lam_im
    fr = ((ar - 1.0) * lam_re + ai * lam_im) / den
    fi = (ai * lam_re - (ar - 1.0) * lam_im) / den
    bb_re = fr[..., None] * b_re - fi[..., None] * b_im
    bb_im = fr[..., None] * b_im + fi[..., None] * b_re
    gl = S5_GROUPS // S5_SPLIT
    eye = jnp.eye(gl, dtype=F32)

    def pack_in(bb):
        bb = jnp.transpose(bb, (0, 2, 1)).reshape(S5_SPLIT, gl, S5_GROUP, S5_STATE)
        return jnp.einsum("jghp,gm->jghmp", bb, eye).reshape(S5_SPLIT, gl * S5_GROUP, gl * S5_STATE).astype(BF16)

    def pack_out(c):
        c = jnp.transpose(c, (0, 2, 1)).reshape(S5_SPLIT, gl, S5_STATE, S5_GROUP)
        return jnp.einsum("jgph,gm->jgpmh", c, eye).reshape(S5_SPLIT, gl * S5_STATE, gl * S5_GROUP).astype(BF16)

    return (pack_in(bb_re), pack_in(bb_im), pack_out(c_re), pack_out(c_im),
            ar.reshape(1, S5_LANES), ai.reshape(1, S5_LANES))


def _conv_rows(x, w, seg):
    tc = x.shape[0]
    t_in_seg = lax.broadcasted_iota(jnp.int32, x.shape, 0) % seg
    acc = x * w[CONV_LEFT:CONV_LEFT + 1]
    for j in range(CONV_W):
        off = j - CONV_LEFT
        if off == 0:
            continue
        sh = pltpu.roll(x, (-off) % tc, 0)
        ok = (t_in_seg >= -off) if off < 0 else (t_in_seg < seg - off)
        acc = acc + jnp.where(ok, sh, 0.0) * w[j:j + 1]
    return acc


def _l2norm(x):
    return x * lax.rsqrt(jnp.sum(x * x, axis=-1, keepdims=True) + EPS)


def _dot_nt(a, b):
    return lax.dot_general(a.astype(BF16), b.astype(BF16), (((1,), (1,)), ((), ())), preferred_element_type=F32)


def _dot_tn(a, b):
    return lax.dot_general(a.astype(BF16), b.astype(BF16), (((0,), (0,)), ((), ())), preferred_element_type=F32)


def _split3_dot(m, x):
    x1 = x.astype(BF16)
    r1 = x - x1.astype(F32)
    x2 = r1.astype(BF16)
    x3 = (r1 - x2.astype(F32)).astype(BF16)
    mb = m.astype(BF16)
    dot = lambda v: jnp.dot(mb, v, preferred_element_type=F32)
    return dot(x1) + dot(x2) + dot(x3)


def _dn_kernel(q_ref, k_ref, v_ref, ba_ref, cwq_ref, cwk_ref, cwv_ref, alog_ref, dtb_ref, s0_ref, *rest,
               seg, reverse, add_prev, dirn):
    if add_prev:
        prev_ref, o_ref, sfin_ref, qc_ref, kc_ref, vc_ref, beta_ref, g_ref, s_ref = rest
    else:
        o_ref, sfin_ref, qc_ref, kc_ref, vc_ref, beta_ref, g_ref, s_ref = rest
    tc = q_ref.shape[0]
    n_chunks = tc // DN_CHUNK
    step = pl.program_id(1)

    @pl.when(step == 0)
    def _():
        s_ref[...] = s0_ref[...]

    qc_ref[...] = _silu(_conv_rows(q_ref[...], cwq_ref[...], seg))
    kc_ref[...] = _silu(_conv_rows(k_ref[...], cwk_ref[...], seg))
    vc_ref[...] = _silu(_conv_rows(v_ref[...], cwv_ref[...], seg))
    ba = ba_ref[...]
    beta_ref[...] = _sigmoid(ba)
    g_ref[...] = -jnp.exp(alog_ref[...]) * _softplus(ba + dtb_ref[...])

    ii = lax.broadcasted_iota(jnp.int32, (DN_CHUNK, DN_CHUNK), 0)
    jj = lax.broadcasted_iota(jnp.int32, (DN_CHUNK, DN_CHUNK), 1)
    incl = (ii <= jj) if reverse else (ii >= jj)
    strict = (ii < jj) if reverse else (ii > jj)
    eye = (ii == jj).astype(F32)
    tri = incl.astype(F32)
    last = 0 if reverse else DN_CHUNK - 1
    neg = jnp.float32(-1e30)

    def chunk(ci, carry):
        cidx = n_chunks - 1 - ci if reverse else ci
        r0 = pl.multiple_of(cidx * DN_CHUNK, DN_CHUNK)
        rows = pl.ds(r0, DN_CHUNK)
        g_cum = _split3_dot(tri, g_ref[rows, :])
        g_t = g_cum.T
        e_g = jnp.exp(g_cum)
        e_rest = jnp.exp(g_cum[last:last + 1] - g_cum)
        e_tot = e_g[last:last + 1]
        beta_all = beta_ref[rows, :]
        for h in range(DN_HEADS):
            lanes = slice(h * DN_HEAD_DIM, (h + 1) * DN_HEAD_DIM)
            cb = dirn * DN_HEADS + h
            cg = 2 * DN_HEADS + cb
            q = _l2norm(qc_ref[rows, lanes]) * (DN_HEAD_DIM ** -0.5)
            k = _l2norm(kc_ref[rows, lanes])
            v = vc_ref[rows, lanes]
            beta = beta_all[:, cb:cb + 1]
            decay = jnp.exp(jnp.where(incl, g_cum[:, cg:cg + 1] - g_t[cg:cg + 1, :], neg))
            kb = k * beta
            a_low = jnp.where(strict, _dot_nt(kb, k) * decay, 0.0)
            x = eye - a_low
            p = _dot(a_low, a_low)
            x = x + _dot(x, p)
            for _ in range(4):
                p = _dot(p, p)
                x = x + _dot(x, p)
            e_col = e_g[:, cg:cg + 1]
            u = _dot(x, v * beta)
            w = _dot(x, kb * e_col)
            attn = jnp.where(incl, _dot_nt(q, k) * decay, 0.0)
            q_dec = q * e_col
            k_dec = k * e_rest[:, cg:cg + 1]
            s = s_ref[h]
            v_new = u - _dot(w, s)
            o = _dot(q_dec, s) + _dot(attn, v_new)
            s_ref[h] = s * e_tot[:, cg:cg + 1] + _dot_tn(k_dec, v_new)
            if add_prev:
                o = o + prev_ref[rows, lanes]
            o_ref[rows, lanes] = o
        return carry

    lax.fori_loop(0, n_chunks, chunk, 0)

    @pl.when(step == pl.num_programs(1) - 1)
    def _():
        sfin_ref[...] = s_ref[...]


def dn_scan(proj, conv_w, a_log, dt_bias, s0, prev, *, seg, reverse):
    b, t, _ = proj.shape
    tc = min(t, 256)
    nt = t // tc
    dirn = 1 if reverse else 0
    tix = (lambda s: nt - 1 - s) if reverse else (lambda s: s)
    wq = DN_WIDTH
    col = lambda cb: pl.BlockSpec((None, tc, wq), lambda i, s: (i, tix(s), cb))
    cw = lambda cb: pl.BlockSpec((CONV_W, wq), lambda i, s: (0, cb))
    vec = pl.BlockSpec((1, 128), lambda i, s: (0, 0))
    st = pl.BlockSpec((None, DN_HEADS, DN_HEAD_DIM, DN_HEAD_DIM), lambda i, s: (i, 0, 0, 0))
    oblk = pl.BlockSpec((None, tc, wq), lambda i, s: (i, tix(s), 0))
    in_specs = [col(COL_Q // wq), col(COL_K // wq), col(COL_V // wq),
                pl.BlockSpec((None, tc, 128), lambda i, s: (i, tix(s), COL_BA // 128)),
                cw(0), cw(1), cw(2), vec, vec, st]
    args = [proj, proj, proj, proj, conv_w, conv_w, conv_w, a_log, dt_bias, s0]
    if prev is not None:
        in_specs.append(oblk)
        args.append(prev)
    return pl.pallas_call(
        functools.partial(_dn_kernel, seg=seg, reverse=reverse, add_prev=prev is not None, dirn=dirn),
        out_shape=(jax.ShapeDtypeStruct((b, t, wq), F32),
                   jax.ShapeDtypeStruct((b, DN_HEADS, DN_HEAD_DIM, DN_HEAD_DIM), F32)),
        grid=(b, nt),
        in_specs=in_specs,
        out_specs=(oblk, st),
        scratch_shapes=[pltpu.VMEM((tc, wq), F32), pltpu.VMEM((tc, wq), F32), pltpu.VMEM((tc, wq), F32),
                        pltpu.VMEM((tc, 128), F32), pltpu.VMEM((tc, 128), F32),
                        pltpu.VMEM((DN_HEADS, DN_HEAD_DIM, DN_HEAD_DIM), F32)],
        compiler_params=_cparams(("arbitrary", "arbitrary")),
        name="dn_scan_bwd" if reverse else "dn_scan_fwd",
    )(*args)


def _dn_lane_vec(p):
    return jnp.zeros((1, 128), F32).at[0, 2 * DN_HEADS:4 * DN_HEADS].set(p.reshape(-1))


def _merge_kernel(hl_ref, ay_ref, od_ref, z_ref, ys_ref, u_ref, gt_ref, x_ref, mod_ref, ng_ref, sd_ref,
                  wglu_ref, bglu_ref, wa_ref, wb_ref, wc_ref, wo_ref, gf_ref, wr1_ref, wr2_ref, br_ref,
                  xo_ref, h_ref, lg_ref):
    d = x_ref.shape[-1]
    mod = mod_ref[...]
    y_a = hl_ref[...] * _gelu_tanh(ay_ref[...])
    acc = _sigmoid(gt_ref[:, 0:d]) * _dot(y_a, wa_ref[...])

    ng = ng_ref[...]
    heads = []
    for h in range(DN_HEADS):
        lanes = slice(h * DN_HEAD_DIM, (h + 1) * DN_HEAD_DIM)
        o = od_ref[:, lanes]
        o = o * lax.rsqrt(jnp.mean(o * o, axis=-1, keepdims=True) + EPS) * ng
        heads.append((o * _silu(z_ref[:, lanes])).astype(BF16))
    y_b = jnp.concatenate(heads, axis=-1)
    acc = acc + _sigmoid(gt_ref[:, d:2 * d]) * jnp.dot(y_b, wb_ref[...], preferred_element_type=F32)

    y_c = _gelu_tanh(ys_ref[...] + sd_ref[...] * u_ref[...])
    y_c = y_c * _sigmoid(_dot(y_c, wglu_ref[...]) + bglu_ref[...])
    acc = acc + _sigmoid(gt_ref[:, 2 * d:3 * d]) * _dot(y_c, wc_ref[...])

    x_new = x_ref[...] + mod[2:3] * _dot(acc, wo_ref[...])
    xo_ref[...] = x_new
    h = _norm_mod(x_new, gf_ref[...], mod[3:4], mod[4:5])
    h1 = h.astype(BF16)
    h2 = (h - h1.astype(F32)).astype(BF16)
    h_ref[...] = h1
    lg_ref[...] = (jnp.dot(h1, wr1_ref[...], preferred_element_type=F32)
                   + jnp.dot(h1, wr2_ref[...], preferred_element_type=F32)
                   + jnp.dot(h2, wr1_ref[...], preferred_element_type=F32)) + br_ref[...]


def merge_stream(h_lru, proj, o_dn, y_s5, x, mod, p):
    b, t, d = x.shape
    tm = min(t, 256)
    row = lambda w, cb: pl.BlockSpec((None, tm, w), lambda i, r: (i, r, cb))
    const = lambda a: pl.BlockSpec(a.shape, lambda i, r: (0,) * a.ndim)
    consts = [p["dn_norm_g"], p["s5_d"], p["s5_w_glu"], p["s5_b_glu"], p["w_br_a"], p["w_br_b"], p["w_br_c"],
              p["w_out"], p["g_ffn"], p["w_r1"], p["w_r2"], p["b_router"]]
    return pl.pallas_call(
        _merge_kernel,
        out_shape=(jax.ShapeDtypeStruct((b, t, d), F32), jax.ShapeDtypeStruct((b, t, d), BF16),
                   jax.ShapeDtypeStruct((b, t, 128), F32)),
        grid=(b, t // tm),
        in_specs=[row(LRU_WIDTH, 0), row(LRU_WIDTH, COL_AY // LRU_WIDTH), row(DN_WIDTH, 0),
                  row(DN_WIDTH, COL_Z // DN_WIDTH), row(S5_WIDTH, 0), row(S5_WIDTH, COL_U // S5_WIDTH),
                  row(3 * d, COL_GATE // (3 * d)), row(d, 0),
                  pl.BlockSpec((None, 6, d), lambda i, r: (i, 0, 0))] + [const(a) for a in consts],
        out_specs=(row(d, 0), row(d, 0), row(128, 0)),
        compiler_params=_cparams(("arbitrary", "arbitrary")),
        name="merge_stream",
    )(h_lru, proj, o_dn, proj, y_s5, proj, proj, x, mod, *consts)


def _route_kernel(lg_ref, o_ref, cnt_ref, carry_ref):
    tm = lg_ref.shape[0]

    @pl.when(pl.program_id(0) == 0)
    def _():
        carry_ref[...] = jnp.zeros_like(carry_ref)

    logits = lg_ref[...]
    lane = lax.broadcasted_iota(jnp.int32, logits.shape, 1)
    neg = jnp.float32(-jnp.inf)
    vals, idxs = [], []
    sel = jnp.zeros(logits.shape, F32)
    for _ in range(TOP_K):
        m = jnp.max(logits, axis=-1, keepdims=True)
        idx = jnp.min(jnp.where(logits == m, lane, 128), axis=-1, keepdims=True)
        hit = lane == idx
        logits = jnp.where(hit, neg, logits)
        sel = jnp.where(hit, 1.0, sel)
        vals.append(m)
        idxs.append(idx)
    exps = [jnp.exp(v - vals[0]) for v in vals]
    inv = 1.0 / (exps[0] + exps[1] + exps[2] + exps[3])

    ii = lax.broadcasted_iota(jnp.int32, (tm, tm), 0)
    jj = lax.broadcasted_iota(jnp.int32, (tm, tm), 1)
    before = jnp.dot((ii > jj).astype(BF16), sel.astype(BF16), preferred_element_type=F32) + carry_ref[...]
    out = jnp.zeros(logits.shape, F32)
    for k in range(TOP_K):
        rank = jnp.sum(jnp.where(lane == idxs[k], before, 0.0), axis=-1, keepdims=True)
        out = jnp.where(lane == k, idxs[k].astype(F32), out)
        out = jnp.where(lane == TOP_K + k, rank, out)
        out = jnp.where(lane == 2 * TOP_K + k, exps[k] * inv, out)
    o_ref[...] = out
    carry_ref[...] = carry_ref[...] + jnp.sum(sel, axis=0, keepdims=True)
    cnt_ref[...] = carry_ref[...]


def route(logits):
    n = logits.shape[0]
    tm = 512
    return pl.pallas_call(
        _route_kernel,
        out_shape=(jax.ShapeDtypeStruct((n, 128), F32), jax.ShapeDtypeStruct((1, 128), F32)),
        grid=(n // tm,),
        in_specs=[pl.BlockSpec((tm, 128), lambda i: (i, 0))],
        out_specs=(pl.BlockSpec((tm, 128), lambda i: (i, 0)), pl.BlockSpec((1, 128), lambda i: (0, 0))),
        scratch_shapes=[pltpu.VMEM((1, 128), F32)],
        compiler_params=_cparams(("arbitrary",)),
        name="route",
    )(logits)


def _expert_kernel(be_ref, nu_ref, x_ref, w1_ref, b1_ref, w2_ref, b2_ref, o_ref):
    @pl.when(pl.program_id(0) < nu_ref[0])
    def _():
        de = w2_ref.shape[0]
        gu = jnp.dot(x_ref[...], w1_ref[...], preferred_element_type=F32) + b1_ref[...]
        glu = jnp.minimum(gu[:, :de], SWIGLU_LIMIT)
        lin = jnp.clip(gu[:, de:], -SWIGLU_LIMIT, SWIGLU_LIMIT)
        act = glu * _sigmoid(SWIGLU_ALPHA * glu) * (lin + 1.0)
        o_ref[...] = _dot(act, w2_ref[...]) + b2_ref[...]


def expert_blocks(xs, block_e, n_used, w1, b1, w2, b2):
    rows, d = xs.shape
    n_blocks = rows // MOE_BLOCK
    de = w2.shape[1]
    blk = lambda i, be, nu: (jnp.minimum(i, nu[0] - 1), 0)
    ex = lambda i, be, nu: (be[jnp.minimum(i, nu[0] - 1)], 0, 0)
    return pl.pallas_call(
        _expert_kernel,
        out_shape=jax.ShapeDtypeStruct((rows, d), F32),
        grid_spec=pltpu.PrefetchScalarGridSpec(
            num_scalar_prefetch=2,
            grid=(n_blocks,),
            in_specs=[pl.BlockSpec((MOE_BLOCK, d), blk),
                      pl.BlockSpec((None, d, 2 * de), ex),
                      pl.BlockSpec((None, 1, 2 * de), ex),
                      pl.BlockSpec((None, de, d), ex),
                      pl.BlockSpec((None, 1, d), ex)],
            out_specs=pl.BlockSpec((MOE_BLOCK, d), blk)),
        compiler_params=_cparams(("arbitrary",)),
        name="expert_blocks",
    )(block_e, n_used, xs, w1, b1, w2, b2)


def _combine_kernel(y_ref, rt_ref, x_ref, mod_ref, gfin_ref, o_ref, *, final_norm):
    d = x_ref.shape[-1]
    rt = rt_ref[...]
    acc = jnp.zeros(x_ref.shape, F32)
    for k in range(TOP_K):
        acc = acc + rt[:, 2 * TOP_K + k:2 * TOP_K + k + 1] * y_ref[:, k * d:(k + 1) * d]
    x_new = x_ref[...] + mod_ref[5:6] * acc
    if final_norm:
        x_new = x_new * lax.rsqrt(jnp.mean(x_new * x_new, axis=-1, keepdims=True) + EPS) * gfin_ref[...]
    o_ref[...] = x_new


def combine(yk, rt, x, mod, g_final, *, final_norm):
    b, t, d = x.shape
    tm = min(t, 512)
    row = lambda w: pl.BlockSpec((None, tm, w), lambda i, r: (i, r, 0))
    return pl.pallas_call(
        functools.partial(_combine_kernel, final_norm=final_norm),
        out_shape=jax.ShapeDtypeStruct((b, t, d), F32),
        grid=(b, t // tm),
        in_specs=[row(TOP_K * d), row(128), row(d), pl.BlockSpec((None, 6, d), lambda i, r: (i, 0, 0)),
                  pl.BlockSpec((1, d), lambda i, r: (0, 0))],
        out_specs=row(d),
        compiler_params=_cparams(("arbitrary", "arbitrary")),
        name="combine",
    )(yk, rt, x, mod, g_final)


def _pad_w_in(w_in):
    d = w_in.shape[0]
    n_ba = 4 * DN_HEADS
    pad = jnp.zeros((d, COL_U - COL_BA - n_ba), w_in.dtype)
    return jnp.concatenate([w_in[:, :COL_BA + n_ba], pad, w_in[:, COL_BA + n_ba:]], axis=1).astype(BF16)


def kernel(x, c, ctx, c_ctx, w_ada, b_ada, g_mix, g_ffn, w_in, lru_conv_w, lru_conv_b, lru_w_gate, lru_b_gate, lru_lam, dn_conv_w, dn_a_log, dn_dt_bias, dn_norm_g, s5_lam_re, s5_lam_im, s5_log_dt, s5_b_re, s5_b_im, s5_c_re, s5_c_im, s5_d, s5_w_glu, s5_b_glu, w_br_a, w_br_b, w_br_c, w_out, w_router, b_router, w_e1, b_e1, w_e2, b_e2, g_final):
    bsz, t, d = x.shape
    n_layers = w_in.shape[0]
    s = jnp.concatenate([c, jnp.broadcast_to(c_ctx[None], (SUBLANES, d))], axis=0)
    mods = ada_modulation(s, w_ada.astype(BF16), b_ada)
    x_lat, x_ctx = x, ctx
    for l in range(n_layers):
        last = l == n_layers - 1
        p = _layer_params(l, g_mix, g_ffn, w_in, lru_conv_w, lru_conv_b, lru_w_gate, lru_b_gate, lru_lam, dn_conv_w,
                          dn_a_log, dn_dt_bias, dn_norm_g, s5_lam_re, s5_lam_im, s5_log_dt, s5_b_re, s5_b_im,
                          s5_c_re, s5_c_im, s5_d, s5_w_glu, s5_b_glu, w_br_a, w_br_b, w_br_c, w_out, w_router,
                          b_router)
        m_lat = mods[l, :bsz].reshape(bsz, 6, d)
        m_ctx = jnp.broadcast_to(mods[l, bsz].reshape(1, 6, d), (bsz, 6, d))
        ctx_out, ctx_states = mix_stream(x_ctx, m_ctx, p, None, x_ctx.shape[1], not last)
        lat_out, _ = mix_stream(x_lat, m_lat, p, ctx_states, GRID_W, True)
        w1, b1 = w_e1[l].astype(BF16), b_e1[l][:, None, :]
        w2, b2 = w_e2[l].astype(BF16), b_e2[l][:, None, :]
        if last:
            x_mid, h, lg = lat_out
            yk, rt = moe(h.reshape(-1, d), lg.reshape(-1, 128), w1, b1, w2, b2)
            x_lat = combine(yk.reshape(bsz, t, TOP_K * d), rt.reshape(bsz, t, 128), x_mid, m_lat,
                            g_final[None], final_norm=True)
        else:
            xc_mid, hc, lgc = ctx_out
            xl_mid, hl, lgl = lat_out
            n_ctx = bsz * x_ctx.shape[1]
            yk, rt = moe(jnp.concatenate([hc.reshape(-1, d), hl.reshape(-1, d)], axis=0),
                         jnp.concatenate([lgc.reshape(-1, 128), lgl.reshape(-1, 128)], axis=0), w1, b1, w2, b2)
            x_ctx = combine(yk[:n_ctx].reshape(bsz, -1, TOP_K * d), rt[:n_ctx].reshape(bsz, -1, 128), xc_mid, m_ctx,
                            g_final[None], final_norm=False)
            x_lat = combine(yk[n_ctx:].reshape(bsz, t, TOP_K * d), rt[n_ctx:].reshape(bsz, t, 128), xl_mid, m_lat,
                            g_final[None], final_norm=False)
    return x_lat


def _layer_params(l, g_mix, g_ffn, w_in, lru_conv_w, lru_conv_b, lru_w_gate, lru_b_gate, lru_lam, dn_conv_w,
                  dn_a_log, dn_dt_bias, dn_norm_g, s5_lam_re, s5_lam_im, s5_log_dt, s5_b_re, s5_b_im, s5_c_re,
                  s5_c_im, s5_d, s5_w_glu, s5_b_glu, w_br_a, w_br_b, w_br_c, w_out, w_router, b_router):
    s5 = [s5_lam_re[l], s5_lam_im[l], s5_log_dt[l], s5_b_re[l], s5_b_im[l], s5_c_re[l], s5_c_im[l]]
    wr = jnp.zeros((w_router.shape[1], 128), F32).at[:, :N_EXPERTS].set(w_router[l])
    wr1 = wr.astype(BF16)
    return {
        "g_mix": g_mix[l][None], "w_in": _pad_w_in(w_in[l]),
        "lru_conv_w": lru_conv_w[l], "lru_conv_b": lru_conv_b[l][None],
        "lru_wg": [_lru_gate_dense(lru_w_gate[l, dr]) for dr in range(2)],
        "lru_bg": lru_b_gate[l].reshape(2, 1, 2 * LRU_WIDTH), "lru_lam": lru_lam[l][:, None, :],
        "dn_conv_w": dn_conv_w[l], "dn_a_log": _dn_lane_vec(dn_a_log[l]), "dn_dt_bias": _dn_lane_vec(dn_dt_bias[l]),
        "s5": [_s5_params(*(a[dr] for a in s5)) for dr in range(2)],
        "dn_norm_g": dn_norm_g[l][None], "s5_d": s5_d[l][None], "s5_w_glu": s5_w_glu[l].astype(BF16),
        "s5_b_glu": s5_b_glu[l][None], "w_br_a": w_br_a[l].astype(BF16), "w_br_b": w_br_b[l].astype(BF16),
        "w_br_c": w_br_c[l].astype(BF16), "w_out": w_out[l].astype(BF16), "g_ffn": g_ffn[l][None],
        "w_r1": wr1, "w_r2": (wr - wr1.astype(F32)).astype(BF16),
        "b_router": jnp.full((1, 128), -1e30, F32).at[0, :N_EXPERTS].set(b_router[l]),
    }


def mix_stream(x, mod, p, init, seg, emit):
    bsz = x.shape[0]
    if init is None:
        zl = jnp.zeros((bsz, LRU_WIDTH), F32)
        zd = jnp.zeros((bsz, DN_HEADS, DN_HEAD_DIM, DN_HEAD_DIM), F32)
        zs = jnp.zeros((bsz, S5_LANES), F32)
        init = ((zl, zl), (zd, zd), ((zs, zs), (zs, zs)))
    proj = inproj(x, mod, p["g_mix"], p["w_in"])
    to_tm = lambda a: jnp.transpose(a, (1, 0, 2))

    ax_tm = to_tm(proj[:, :, COL_AX:COL_AX + LRU_WIDTH])
    lru = lambda dr, prev: lru_scan(ax_tm, p["lru_conv_w"], p["lru_conv_b"], p["lru_wg"][dr], p["lru_bg"][dr],
                                    p["lru_lam"][dr], init[0][dr], prev, seg=seg, reverse=dr == 1)
    h_f, lru_f = lru(0, None)
    h_sum, lru_b = lru(1, h_f)

    dn = lambda dr, prev: dn_scan(proj, p["dn_conv_w"], p["dn_a_log"], p["dn_dt_bias"], init[1][dr], prev,
                                  seg=seg, reverse=dr == 1)
    o_f, dn_f = dn(0, None)
    o_sum, dn_b = dn(1, o_f)

    u_tm = to_tm(proj[:, :, COL_U:COL_U + S5_WIDTH])
    s5 = lambda dr, prev: s5_scan(u_tm, p["s5"][dr], init[2][dr][0], init[2][dr][1], prev, reverse=dr == 1)
    y_f, s5_fr, s5_fi = s5(0, None)
    y_sum, s5_br, s5_bi = s5(1, y_f)

    states = ((lru_f, lru_b), (dn_f, dn_b), ((s5_fr, s5_fi), (s5_br, s5_bi)))
    if not emit:
        return None, states
    return merge_stream(to_tm(h_sum), proj, o_sum, to_tm(y_sum), x, mod, p), states


def moe(h, logits, w1, b1, w2, b2):
    n, d = h.shape
    rt, counts = route(logits)
    idx = rt[:, :TOP_K].astype(jnp.int32)
    rank = rt[:, TOP_K:2 * TOP_K].astype(jnp.int32)
    sizes = counts[0, :N_EXPERTS].astype(jnp.int32)
    padded = ((sizes + MOE_BLOCK - 1) // MOE_BLOCK) * MOE_BLOCK
    ends_pad = jnp.cumsum(padded)
    starts_pad = ends_pad - padded
    dest = (starts_pad[idx] + rank).reshape(-1)
    n_blocks = -(-(n * TOP_K) // MOE_BLOCK) + N_EXPERTS
    block_e = jnp.minimum(jnp.searchsorted(ends_pad, jnp.arange(n_blocks) * MOE_BLOCK, side="right"),
                          N_EXPERTS - 1).astype(jnp.int32)
    n_used = (ends_pad[-1:] // MOE_BLOCK).astype(jnp.int32)
    row_tok = jnp.zeros((n_blocks * MOE_BLOCK,), jnp.int32).at[dest].set(
        jnp.repeat(jnp.arange(n, dtype=jnp.int32), TOP_K))
    yb = expert_blocks(h[row_tok], block_e, n_used, w1, b1, w2, b2)
    return yb[dest].reshape(n, TOP_K * d), rt
```

```python
import functools
import math

import jax
import jax.numpy as jnp
from jax import lax
from jax.experimental import pallas as pl
from jax.experimental.pallas import tpu as pltpu

F32 = jnp.float32
BF16 = jnp.bfloat16

D_MODEL = 1024
EPS = 1e-6
CONV_W = 4
CONV_LEFT = CONV_W // 2
GRID_W = 64

LRU_WIDTH = 512
LRU_BLOCKS = 8
LRU_BLOCK = LRU_WIDTH // LRU_BLOCKS
LRU_C = 8.0

DN_HEAD_DIM = 128
DN_HEADS = 8
DN_WIDTH = DN_HEADS * DN_HEAD_DIM
DN_CHUNK = 64
DN_HEAD_GROUP = 8

S5_WIDTH = 512
S5_GROUP = 16
S5_GROUPS = 32
S5_STATE = 64
S5_LANES = S5_GROUPS * S5_STATE
S5_SPLIT = 4

N_EXPERTS = 32
TOP_K = 4
D_EXPERT = 1024
SWIGLU_LIMIT = 7.0
SWIGLU_ALPHA = 1.702
MOE_BLOCK = 512

COL_AX, COL_AY, COL_Q, COL_K, COL_V, COL_Z = 0, 512, 1024, 2048, 3072, 4096
COL_BA, COL_U, COL_GATE, D_IN_PAD = 5120, 5632, 6144, 9216
D_IN = 8736

SUBLANES = 8
VMEM_LIMIT = 52 * 1024 * 1024


def _cparams(sem):
    return pltpu.CompilerParams(dimension_semantics=sem, vmem_limit_bytes=VMEM_LIMIT)


def _sigmoid(x):
    return 1.0 / (1.0 + jnp.exp(-x))


def _softplus(x):
    return jnp.maximum(x, 0.0) + jnp.log1p(jnp.exp(-jnp.abs(x)))


def _silu(x):
    return x * _sigmoid(x)


def _gelu_tanh(x):
    return 0.5 * x * (1.0 + jnp.tanh(math.sqrt(2.0 / math.pi) * (x + 0.044715 * x * x * x)))


def _dot(a, b):
    return jnp.dot(a.astype(BF16), b.astype(BF16), preferred_element_type=F32)


def _ada_kernel(s_ref, w_ref, b_ref, o_ref):
    o_ref[...] = _dot(_silu(s_ref[...]), w_ref[...]) + b_ref[...]


def ada_modulation(s, w_ada, b_ada):
    n_layers, d, n = w_ada.shape
    r = s.shape[0]
    tn = 1536
    return pl.pallas_call(
        _ada_kernel,
        out_shape=jax.ShapeDtypeStruct((n_layers, r, n), F32),
        grid=(n_layers, n // tn),
        in_specs=[pl.BlockSpec((r, d), lambda l, j: (0, 0)),
                  pl.BlockSpec((None, d, tn), lambda l, j: (l, 0, j)),
                  pl.BlockSpec((None, 1, tn), lambda l, j: (l, 0, j))],
        out_specs=pl.BlockSpec((None, r, tn), lambda l, j: (l, 0, j)),
        compiler_params=_cparams(("arbitrary", "arbitrary")),
        name="ada_modulation",
    )(s, w_ada, b_ada.reshape(n_layers, 1, n))


def _norm_mod(x, g, shift, scale):
    y = x * lax.rsqrt(jnp.mean(x * x, axis=-1, keepdims=True) + EPS)
    return (y * g) * (1.0 + scale) + shift


def _inproj_kernel(x_ref, mod_ref, g_ref, w_ref, o_ref, h_ref, *, shift_row):
    @pl.when(pl.program_id(2) == 0)
    def _():
        mod = mod_ref[...]
        h = _norm_mod(x_ref[...], g_ref[...], mod[shift_row:shift_row + 1], mod[shift_row + 1:shift_row + 2])
        h_ref[...] = h.astype(BF16)

    o_ref[...] = jnp.dot(h_ref[...], w_ref[...], preferred_element_type=F32)


def inproj(x, mod, g, w_pad):
    b, t, d = x.shape
    n = w_pad.shape[1]
    tm = min(t, 1024)
    tn = 1536
    return pl.pallas_call(
        functools.partial(_inproj_kernel, shift_row=0),
        out_shape=jax.ShapeDtypeStruct((b, t, n), F32),
        grid=(b, t // tm, n // tn),
        in_specs=[pl.BlockSpec((None, tm, d), lambda i, r, j: (i, r, 0)),
                  pl.BlockSpec((None, 6, d), lambda i, r, j: (i, 0, 0)),
                  pl.BlockSpec((1, d), lambda i, r, j: (0, 0)),
                  pl.BlockSpec((d, tn), lambda i, r, j: (0, j))],
        out_specs=pl.BlockSpec((None, tm, tn), lambda i, r, j: (i, r, j)),
        scratch_shapes=[pltpu.VMEM((tm, d), BF16)],
        compiler_params=_cparams(("arbitrary", "arbitrary", "arbitrary")),
        name="inproj",
    )(x, mod, g, w_pad)


def _conv_time_major(x, w, seg):
    tc = x.shape[0]
    t_in_seg = lax.broadcasted_iota(jnp.int32, x.shape, 0) % seg
    acc = x * w[CONV_LEFT:CONV_LEFT + 1][None]
    for j in range(CONV_W):
        off = j - CONV_LEFT
        if off == 0:
            continue
        if off < 0:
            sh = jnp.concatenate([jnp.zeros((-off,) + x.shape[1:], x.dtype), x[:tc + off]], axis=0)
            ok = t_in_seg >= -off
        else:
            sh = jnp.concatenate([x[off:], jnp.zeros((off,) + x.shape[1:], x.dtype)], axis=0)
            ok = t_in_seg < seg - off
        acc = acc + jnp.where(ok, sh, 0.0) * w[j:j + 1][None]
    return acc


def _lru_kernel(x_ref, cw_ref, cb_ref, wg_ref, bg_ref, lam_ref, h0_ref, *rest, seg, reverse, add_prev):
    if add_prev:
        prev_ref, o_ref, hfin_ref, a_ref, b_ref, h_ref = rest
    else:
        o_ref, hfin_ref, a_ref, b_ref, h_ref = rest
    tc, bc, w = x_ref.shape
    step = pl.program_id(1)

    @pl.when(step == 0)
    def _():
        h_ref[...] = h0_ref[...]

    xc = _conv_time_major(x_ref[...], cw_ref[...], seg) + cb_ref[...][None]
    gates = _dot(xc.reshape(tc * bc, w), wg_ref[...]) + bg_ref[...]
    r = _sigmoid(gates[:, :w]).reshape(tc, bc, w)
    i = _sigmoid(gates[:, w:]).reshape(tc, bc, w)
    log_a = (-LRU_C * _softplus(-lam_ref[...]))[None] * r
    a_ref[...] = jnp.exp(log_a)
    b_ref[...] = jnp.sqrt(1.0 - jnp.exp(2.0 * log_a)) * (i * xc)

    def body(s, h):
        t = tc - 1 - s if reverse else s
        h = a_ref[t] * h + b_ref[t]
        if add_prev:
            o_ref[t] = h + prev_ref[t]
        else:
            o_ref[t] = h
        return h

    h = lax.fori_loop(0, tc, body, h_ref[...], unroll=8)
    h_ref[...] = h
    hfin_ref[...] = h


def lru_scan(x_tm, conv_w, conv_b, wg, bg, lam, h0, prev, *, seg, reverse):
    t, b, w = x_tm.shape
    tc = min(t, 256)
    bc = SUBLANES
    nt = t // tc
    tmap = (lambda i, s: (nt - 1 - s, i, 0)) if reverse else (lambda i, s: (s, i, 0))
    blk = pl.BlockSpec((tc, bc, w), tmap)
    const = lambda shape: pl.BlockSpec(shape, lambda i, s: (0,) * len(shape))
    in_specs = [blk, const((CONV_W, w)), const((1, w)), const((w, 2 * w)), const((1, 2 * w)), const((1, w)),
                pl.BlockSpec((bc, w), lambda i, s: (i, 0))]
    args = [x_tm, conv_w, conv_b, wg, bg, lam, h0]
    if prev is not None:
        in_specs.append(blk)
        args.append(prev)
    return pl.pallas_call(
        functools.partial(_lru_kernel, seg=seg, reverse=reverse, add_prev=prev is not None),
        out_shape=(jax.ShapeDtypeStruct((t, b, w), F32), jax.ShapeDtypeStruct((b, w), F32)),
        grid=(b // bc, nt),
        in_specs=in_specs,
        out_specs=(blk, pl.BlockSpec((bc, w), lambda i, s: (i, 0))),
        scratch_shapes=[pltpu.VMEM((tc, bc, w), F32), pltpu.VMEM((tc, bc, w), F32), pltpu.VMEM((bc, w), F32)],
        compiler_params=_cparams(("arbitrary", "arbitrary")),
        name="lru_scan_bwd" if reverse else "lru_scan_fwd",
    )(*args)


def _lru_gate_dense(w_gate):
    eye = jnp.eye(LRU_BLOCKS, dtype=w_gate.dtype)
    dense = jnp.einsum("gnij,nm->gnimj", w_gate, eye).reshape(2, LRU_WIDTH, LRU_WIDTH)
    return jnp.concatenate([dense[0], dense[1]], axis=1).astype(BF16)


def _s5_kernel(u_ref, bre_ref, bim_ref, cre_ref, cim_ref, ar_ref, ai_ref, h0r_ref, h0i_ref, *rest,
               reverse, add_prev):
    if add_prev:
        prev_ref, o_ref, fr_ref, fi_ref, xr_ref, xi_ref, hr_ref, hi_ref = rest
    else:
        o_ref, fr_ref, fi_ref, xr_ref, xi_ref, hr_ref, hi_ref = rest
    tc, bc, w = u_ref.shape
    blk_in = w // S5_SPLIT
    blk_st = S5_LANES // S5_SPLIT

    @pl.when(pl.program_id(1) == 0)
    def _():
        hr_ref[...] = h0r_ref[...]
        hi_ref[...] = h0i_ref[...]

    u2 = u_ref[...].reshape(tc * bc, w).astype(BF16)
    for j in range(S5_SPLIT):
        uj = u2[:, j * blk_in:(j + 1) * blk_in]
        xr_ref[:, :, j * blk_st:(j + 1) * blk_st] = jnp.dot(
            uj, bre_ref[j], preferred_element_type=F32).reshape(tc, bc, blk_st)
        xi_ref[:, :, j * blk_st:(j + 1) * blk_st] = jnp.dot(
            uj, bim_ref[j], preferred_element_type=F32).reshape(tc, bc, blk_st)

    ar = jnp.broadcast_to(ar_ref[...], (bc, S5_LANES))
    ai = jnp.broadcast_to(ai_ref[...], (bc, S5_LANES))

    def body(s, carry):
        hr, hi = carry
        t = tc - 1 - s if reverse else s
        nr = ar * hr - ai * hi + xr_ref[t]
        ni = ar * hi + ai * hr + xi_ref[t]
        xr_ref[t] = nr
        xi_ref[t] = ni
        return nr, ni

    hr, hi = lax.fori_loop(0, tc, body, (hr_ref[...], hi_ref[...]))
    hr_ref[...] = hr
    hi_ref[...] = hi
    fr_ref[...] = hr
    fi_ref[...] = hi

    for j in range(S5_SPLIT):
        hrj = xr_ref[:, :, j * blk_st:(j + 1) * blk_st].reshape(tc * bc, blk_st).astype(BF16)
        hij = xi_ref[:, :, j * blk_st:(j + 1) * blk_st].reshape(tc * bc, blk_st).astype(BF16)
        y = (jnp.dot(hrj, cre_ref[j], preferred_element_type=F32)
             - jnp.dot(hij, cim_ref[j], preferred_element_type=F32)).reshape(tc, bc, blk_in)
        if add_prev:
            y = y + prev_ref[:, :, j * blk_in:(j + 1) * blk_in]
        o_ref[:, :, j * blk_in:(j + 1) * blk_in] = y


def s5_scan(u_tm, prm, h0r, h0i, prev, *, reverse):
    t, b, w = u_tm.shape
    tc = min(t, 64)
    bc = SUBLANES
    nt = t // tc
    tmap = (lambda i, s: (nt - 1 - s, i, 0)) if reverse else (lambda i, s: (s, i, 0))
    blk = pl.BlockSpec((tc, bc, w), tmap)
    const = lambda shape: pl.BlockSpec(shape, lambda i, s: (0,) * len(shape))
    st = pl.BlockSpec((bc, S5_LANES), lambda i, s: (i, 0))
    bre, bim, cre, cim, ar, ai = prm
    in_specs = [blk, const(bre.shape), const(bim.shape), const(cre.shape), const(cim.shape),
                const(ar.shape), const(ai.shape), st, st]
    args = [u_tm, bre, bim, cre, cim, ar, ai, h0r, h0i]
    if prev is not None:
        in_specs.append(blk)
        args.append(prev)
    st_shape = jax.ShapeDtypeStruct((b, S5_LANES), F32)
    return pl.pallas_call(
        functools.partial(_s5_kernel, reverse=reverse, add_prev=prev is not None),
        out_shape=(jax.ShapeDtypeStruct((t, b, w), F32), st_shape, st_shape),
        grid=(b // bc, nt),
        in_specs=in_specs,
        out_specs=(blk, st, st),
        scratch_shapes=[pltpu.VMEM((tc, bc, S5_LANES), F32), pltpu.VMEM((tc, bc, S5_LANES), F32),
                        pltpu.VMEM((bc, S5_LANES), F32), pltpu.VMEM((bc, S5_LANES), F32)],
        compiler_params=_cparams(("arbitrary", "arbitrary")),
        name="s5_scan_bwd" if reverse else "s5_scan_fwd",
    )(*args)


def _s5_params(lam_re, lam_im, log_dt, b_re, b_im, c_re, c_im):
    dt = jnp.exp(log_dt)[:, None]
    mag = jnp.exp(lam_re * dt)
    ar, ai = mag * jnp.cos(lam_im * dt), mag * jnp.sin(lam_im * dt)
    den = lam_re * lam_re + lam_im * lam_im
    fr = ((ar - 1.0) * lam_re + ai * lam_im) / den
    fi = (ai * lam_re - (ar - 1.0) * lam_im) / den
    bb_re = fr[..., None] * b_re - fi[..., None] * b_im
    bb_im = fr[..., None] * b_im + fi[..., None] * b_re
    gl = S5_GROUPS // S5_SPLIT
    eye = jnp.eye(gl, dtype=F32)

    def pack_in(bb):
        bb = jnp.transpose(bb, (0, 2, 1)).reshape(S5_SPLIT, gl, S5_GROUP, S5_STATE)
        return jnp.einsum("jghp,gm->jghmp", bb, eye).reshape(S5_SPLIT, gl * S5_GROUP, gl * S5_STATE).astype(BF16)

    def pack_out(c):
        c = jnp.transpose(c, (0, 2, 1)).reshape(S5_SPLIT, gl, S5_STATE, S5_GROUP)
        return jnp.einsum("jgph,gm->jgpmh", c, eye).reshape(S5_SPLIT, gl * S5_STATE, gl * S5_GROUP).astype(BF16)

    return (pack_in(bb_re), pack_in(bb_im), pack_out(c_re), pack_out(c_im),
            ar.reshape(1, S5_LANES), ai.reshape(1, S5_LANES))


def _conv_rows(x, w, seg):
    tc = x.shape[0]
    t_in_seg = lax.broadcasted_iota(jnp.int32, x.shape, 0) % seg
    acc = x * w[CONV_LEFT:CONV_LEFT + 1]
    for j in range(CONV_W):
        off = j - CONV_LEFT
        if off == 0:
            continue
        sh = pltpu.roll(x, (-off) % tc, 0)
        ok = (t_in_seg >= -off) if off < 0 else (t_in_seg < seg - off)
        acc = acc + jnp.where(ok, sh, 0.0) * w[j:j + 1]
    return acc


def _l2norm(x):
    return x * lax.rsqrt(jnp.sum(x * x, axis=-1, keepdims=True) + EPS)


def _dot_nt(a, b):
    return lax.dot_general(a.astype(BF16), b.astype(BF16), (((1,), (1,)), ((), ())), preferred_element_type=F32)


def _dot_tn(a, b):
    return lax.dot_general(a.astype(BF16), b.astype(BF16), (((0,), (0,)), ((), ())), preferred_element_type=F32)


def _split3_dot(m, x):
    x1 = x.astype(BF16)
    r1 = x - x1.astype(F32)
    x2 = r1.astype(BF16)
    x3 = (r1 - x2.astype(F32)).astype(BF16)
    mb = m.astype(BF16)
    dot = lambda v: jnp.dot(mb, v, preferred_element_type=F32)
    return dot(x1) + dot(x2) + dot(x3)


def _dn_kernel(q_ref, k_ref, v_ref, ba_ref, cwq_ref, cwk_ref, cwv_ref, alog_ref, dtb_ref, s0_ref, *rest,
               seg, reverse, add_prev, dirn):
    if add_prev:
        prev_ref, o_ref, sfin_ref, qc_ref, kc_ref, vc_ref, beta_ref, g_ref, s_ref = rest
    else:
        o_ref, sfin_ref, qc_ref, kc_ref, vc_ref, beta_ref, g_ref, s_ref = rest
    tc = q_ref.shape[0]
    n_chunks = tc // DN_CHUNK
    step = pl.program_id(1)

    @pl.when(step == 0)
    def _():
        s_ref[...] = s0_ref[...]

    qc_ref[...] = _silu(_conv_rows(q_ref[...], cwq_ref[...], seg))
    kc_ref[...] = _silu(_conv_rows(k_ref[...], cwk_ref[...], seg))
    vc_ref[...] = _silu(_conv_rows(v_ref[...], cwv_ref[...], seg))
    ba = ba_ref[...]
    beta_ref[...] = _sigmoid(ba)
    g_ref[...] = -jnp.exp(alog_ref[...]) * _softplus(ba + dtb_ref[...])

    ii = lax.broadcasted_iota(jnp.int32, (DN_CHUNK, DN_CHUNK), 0)
    jj = lax.broadcasted_iota(jnp.int32, (DN_CHUNK, DN_CHUNK), 1)
    incl = (ii <= jj) if reverse else (ii >= jj)
    strict = (ii < jj) if reverse else (ii > jj)
    eye = (ii == jj).astype(F32)
    tri = incl.astype(F32)
    last = 0 if reverse else DN_CHUNK - 1
    neg = jnp.float32(-1e30)

    def chunk(ci, carry):
        cidx = n_chunks - 1 - ci if reverse else ci
        r0 = pl.multiple_of(cidx * DN_CHUNK, DN_CHUNK)
        rows = pl.ds(r0, DN_CHUNK)
        g_cum = _split3_dot(tri, g_ref[rows, :])
        g_t = g_cum.T
        e_g = jnp.exp(g_cum)
        e_rest = jnp.exp(g_cum[last:last + 1] - g_cum)
        e_tot = e_g[last:last + 1]
        beta_all = beta_ref[rows, :]
        for h0 in range(0, DN_HEADS, DN_HEAD_GROUP):
            hs = range(h0, h0 + DN_HEAD_GROUP)
            each = lambda f, *cols: [f(*args) for args in zip(*cols)]
            lanes = [slice(h * DN_HEAD_DIM, (h + 1) * DN_HEAD_DIM) for h in hs]
            cb = [dirn * DN_HEADS + h for h in hs]
            cg = [2 * DN_HEADS + c for c in cb]
            q = [_l2norm(qc_ref[rows, ln]) * (DN_HEAD_DIM ** -0.5) for ln in lanes]
            k = [_l2norm(kc_ref[rows, ln]) for ln in lanes]
            v = [vc_ref[rows, ln] for ln in lanes]
            beta = [beta_all[:, c:c + 1] for c in cb]
            decay = [jnp.exp(jnp.where(incl, g_cum[:, c:c + 1] - g_t[c:c + 1, :], neg)) for c in cg]
            kb = each(lambda a, b: a * b, k, beta)
            a_low = each(lambda a, b, dc: jnp.where(strict, _dot_nt(a, b) * dc, 0.0), kb, k, decay)
            x = [eye - a for a in a_low]
            p = [_dot(a, a) for a in a_low]
            x = each(lambda a, b: a + _dot(a, b), x, p)
            for _ in range(4):
                p = [_dot(a, a) for a in p]
                x = each(lambda a, b: a + _dot(a, b), x, p)
            e_col = [e_g[:, c:c + 1] for c in cg]
            u = each(lambda a, b, c: _dot(a, b * c), x, v, beta)
            w = each(lambda a, b, c: _dot(a, b * c), x, kb, e_col)
            attn = each(lambda a, b, dc: jnp.where(incl, _dot_nt(a, b) * dc, 0.0), q, k, decay)
            q_dec = each(lambda a, b: a * b, q, e_col)
            k_dec = [kk * e_rest[:, c:c + 1] for kk, c in zip(k, cg)]
            s = [s_ref[h] for h in hs]
            v_new = each(lambda a, b, c: a - _dot(b, c), u, w, s)
            o = each(lambda a, b, c, e: _dot(a, b) + _dot(c, e), q_dec, s, attn, v_new)
            s_new = [ss * e_tot[:, c:c + 1] + _dot_tn(kd, vn) for ss, c, kd, vn in zip(s, cg, k_dec, v_new)]
            if add_prev:
                o = [oo + prev_ref[rows, ln] for oo, ln in zip(o, lanes)]
            for h, ss in zip(hs, s_new):
                s_ref[h] = ss
            for ln, oo in zip(lanes, o):
                o_ref[rows, ln] = oo
        return carry

    lax.fori_loop(0, n_chunks, chunk, 0)

    @pl.when(step == pl.num_programs(1) - 1)
    def _():
        sfin_ref[...] = s_ref[...]


def dn_scan(proj, conv_w, a_log, dt_bias, s0, prev, *, seg, reverse):
    b, t, _ = proj.shape
    tc = min(t, 256)
    nt = t // tc
    dirn = 1 if reverse else 0
    tix = (lambda s: nt - 1 - s) if reverse else (lambda s: s)
    wq = DN_WIDTH
    col = lambda cb: pl.BlockSpec((None, tc, wq), lambda i, s: (i, tix(s), cb))
    cw = lambda cb: pl.BlockSpec((CONV_W, wq), lambda i, s: (0, cb))
    vec = pl.BlockSpec((1, 128), lambda i, s: (0, 0))
    st = pl.BlockSpec((None, DN_HEADS, DN_HEAD_DIM, DN_HEAD_DIM), lambda i, s: (i, 0, 0, 0))
    oblk = pl.BlockSpec((None, tc, wq), lambda i, s: (i, tix(s), 0))
    in_specs = [col(COL_Q // wq), col(COL_K // wq), col(COL_V // wq),
                pl.BlockSpec((None, tc, 128), lambda i, s: (i, tix(s), COL_BA // 128)),
                cw(0), cw(1), cw(2), vec, vec, st]
    args = [proj, proj, proj, proj, conv_w, conv_w, conv_w, a_log, dt_bias, s0]
    if prev is not None:
        in_specs.append(oblk)
        args.append(prev)
    return pl.pallas_call(
        functools.partial(_dn_kernel, seg=seg, reverse=reverse, add_prev=prev is not None, dirn=dirn),
        out_shape=(jax.ShapeDtypeStruct((b, t, wq), F32),
                   jax.ShapeDtypeStruct((b, DN_HEADS, DN_HEAD_DIM, DN_HEAD_DIM), F32)),
        grid=(b, nt),
        in_specs=in_specs,
        out_specs=(oblk, st),
        scratch_shapes=[pltpu.VMEM((tc, wq), F32), pltpu.VMEM((tc, wq), F32), pltpu.VMEM((tc, wq), F32),
                        pltpu.VMEM((tc, 128), F32), pltpu.VMEM((tc, 128), F32),
                        pltpu.VMEM((DN_HEADS, DN_HEAD_DIM, DN_HEAD_DIM), F32)],
        compiler_params=_cparams(("arbitrary", "arbitrary")),
        name="dn_scan_bwd" if reverse else "dn_scan_fwd",
    )(*args)


def _dn_lane_vec(p):
    return jnp.zeros((1, 128), F32).at[0, 2 * DN_HEADS:4 * DN_HEADS].set(p.reshape(-1))


def _merge_kernel(hl_ref, ay_ref, od_ref, z_ref, ys_ref, u_ref, gt_ref, x_ref, mod_ref, ng_ref, sd_ref,
                  wglu_ref, bglu_ref, wa_ref, wb_ref, wc_ref, wo_ref, gf_ref, wr1_ref, wr2_ref, br_ref,
                  xo_ref, h_ref, lg_ref):
    d = x_ref.shape[-1]
    mod = mod_ref[...]
    y_a = hl_ref[...] * _gelu_tanh(ay_ref[...])
    acc = _sigmoid(gt_ref[:, 0:d]) * _dot(y_a, wa_ref[...])

    ng = ng_ref[...]
    heads = []
    for h in range(DN_HEADS):
        lanes = slice(h * DN_HEAD_DIM, (h + 1) * DN_HEAD_DIM)
        o = od_ref[:, lanes]
        o = o * lax.rsqrt(jnp.mean(o * o, axis=-1, keepdims=True) + EPS) * ng
        heads.append((o * _silu(z_ref[:, lanes])).astype(BF16))
    y_b = jnp.concatenate(heads, axis=-1)
    acc = acc + _sigmoid(gt_ref[:, d:2 * d]) * jnp.dot(y_b, wb_ref[...], preferred_element_type=F32)

    y_c = _gelu_tanh(ys_ref[...] + sd_ref[...] * u_ref[...])
    y_c = y_c * _sigmoid(_dot(y_c, wglu_ref[...]) + bglu_ref[...])
    acc = acc + _sigmoid(gt_ref[:, 2 * d:3 * d]) * _dot(y_c, wc_ref[...])

    x_new = x_ref[...] + mod[2:3] * _dot(acc, wo_ref[...])
    xo_ref[...] = x_new
    h = _norm_mod(x_new, gf_ref[...], mod[3:4], mod[4:5])
    h1 = h.astype(BF16)
    h2 = (h - h1.astype(F32)).astype(BF16)
    h_ref[...] = h1
    lg_ref[...] = (jnp.dot(h1, wr1_ref[...], preferred_element_type=F32)
                   + jnp.dot(h1, wr2_ref[...], preferred_element_type=F32)
                   + jnp.dot(h2, wr1_ref[...], preferred_element_type=F32)) + br_ref[...]


def merge_stream(h_lru, proj, o_dn, y_s5, x, mod, p):
    b, t, d = x.shape
    tm = min(t, 256)
    row = lambda w, cb: pl.BlockSpec((None, tm, w), lambda i, r: (i, r, cb))
    const = lambda a: pl.BlockSpec(a.shape, lambda i, r: (0,) * a.ndim)
    consts = [p["dn_norm_g"], p["s5_d"], p["s5_w_glu"], p["s5_b_glu"], p["w_br_a"], p["w_br_b"], p["w_br_c"],
              p["w_out"], p["g_ffn"], p["w_r1"], p["w_r2"], p["b_router"]]
    return pl.pallas_call(
        _merge_kernel,
        out_shape=(jax.ShapeDtypeStruct((b, t, d), F32), jax.ShapeDtypeStruct((b, t, d), BF16),
                   jax.ShapeDtypeStruct((b, t, 128), F32)),
        grid=(b, t // tm),
        in_specs=[row(LRU_WIDTH, 0), row(LRU_WIDTH, COL_AY // LRU_WIDTH), row(DN_WIDTH, 0),
                  row(DN_WIDTH, COL_Z // DN_WIDTH), row(S5_WIDTH, 0), row(S5_WIDTH, COL_U // S5_WIDTH),
                  row(3 * d, COL_GATE // (3 * d)), row(d, 0),
                  pl.BlockSpec((None, 6, d), lambda i, r: (i, 0, 0))] + [const(a) for a in consts],
        out_specs=(row(d, 0), row(d, 0), row(128, 0)),
        compiler_params=_cparams(("arbitrary", "arbitrary")),
        name="merge_stream",
    )(h_lru, proj, o_dn, proj, y_s5, proj, proj, x, mod, *consts)


def _route_kernel(lg_ref, o_ref, cnt_ref, carry_ref):
    tm = lg_ref.shape[0]

    @pl.when(pl.program_id(0) == 0)
    def _():
        carry_ref[...] = jnp.zeros_like(carry_ref)

    logits = lg_ref[...]
    lane = lax.broadcasted_iota(jnp.int32, logits.shape, 1)
    neg = jnp.float32(-jnp.inf)
    vals, idxs = [], []
    sel = jnp.zeros(logits.shape, F32)
    for _ in range(TOP_K):
        m = jnp.max(logits, axis=-1, keepdims=True)
        idx = jnp.min(jnp.where(logits == m, lane, 128), axis=-1, keepdims=True)
        hit = lane == idx
        logits = jnp.where(hit, neg, logits)
        sel = jnp.where(hit, 1.0, sel)
        vals.append(m)
        idxs.append(idx)
    exps = [jnp.exp(v - vals[0]) for v in vals]
    inv = 1.0 / (exps[0] + exps[1] + exps[2] + exps[3])

    ii = lax.broadcasted_iota(jnp.int32, (tm, tm), 0)
    jj = lax.broadcasted_iota(jnp.int32, (tm, tm), 1)
    before = jnp.dot((ii > jj).astype(BF16), sel.astype(BF16), preferred_element_type=F32) + carry_ref[...]
    out = jnp.zeros(logits.shape, F32)
    for k in range(TOP_K):
        rank = jnp.sum(jnp.where(lane == idxs[k], before, 0.0), axis=-1, keepdims=True)
        out = jnp.where(lane == k, idxs[k].astype(F32), out)
        out = jnp.where(lane == TOP_K + k, rank, out)
        out = jnp.where(lane == 2 * TOP_K + k, exps[k] * inv, out)
    o_ref[...] = out
    carry_ref[...] = carry_ref[...] + jnp.sum(sel, axis=0, keepdims=True)
    cnt_ref[...] = carry_ref[...]


def route(logits):
    n = logits.shape[0]
    tm = 512
    return pl.pallas_call(
        _route_kernel,
        out_shape=(jax.ShapeDtypeStruct((n, 128), F32), jax.ShapeDtypeStruct((1, 128), F32)),
        grid=(n // tm,),
        in_specs=[pl.BlockSpec((tm, 128), lambda i: (i, 0))],
        out_specs=(pl.BlockSpec((tm, 128), lambda i: (i, 0)), pl.BlockSpec((1, 128), lambda i: (0, 0))),
        scratch_shapes=[pltpu.VMEM((1, 128), F32)],
        compiler_params=_cparams(("arbitrary",)),
        name="route",
    )(logits)


def _expert_kernel(be_ref, nu_ref, x_ref, w1_ref, b1_ref, w2_ref, b2_ref, o_ref, w1b_ref, w2b_ref):
    i = pl.program_id(0)

    @pl.when(i < nu_ref[0])
    def _():
        @pl.when(jnp.logical_or(i == 0, be_ref[i] != be_ref[jnp.maximum(i - 1, 0)]))
        def _():
            w1b_ref[...] = w1_ref[...].astype(BF16)
            w2b_ref[...] = w2_ref[...].astype(BF16)

        de = w2_ref.shape[0]
        gu = jnp.dot(x_ref[...], w1b_ref[...], preferred_element_type=F32) + b1_ref[...]
        glu = jnp.minimum(gu[:, :de], SWIGLU_LIMIT)
        lin = jnp.clip(gu[:, de:], -SWIGLU_LIMIT, SWIGLU_LIMIT)
        act = glu * _sigmoid(SWIGLU_ALPHA * glu) * (lin + 1.0)
        o_ref[...] = jnp.dot(act.astype(BF16), w2b_ref[...], preferred_element_type=F32) + b2_ref[...]


def expert_blocks(xs, block_e, n_used, w1, b1, w2, b2):
    rows, d = xs.shape
    n_blocks = rows // MOE_BLOCK
    de = w2.shape[1]
    blk = lambda i, be, nu: (jnp.minimum(i, nu[0] - 1), 0)
    ex = lambda i, be, nu: (be[jnp.minimum(i, nu[0] - 1)], 0, 0)
    return pl.pallas_call(
        _expert_kernel,
        out_shape=jax.ShapeDtypeStruct((rows, d), F32),
        grid_spec=pltpu.PrefetchScalarGridSpec(
            num_scalar_prefetch=2,
            grid=(n_blocks,),
            in_specs=[pl.BlockSpec((MOE_BLOCK, d), blk),
                      pl.BlockSpec((None, d, 2 * de), ex),
                      pl.BlockSpec((None, 1, 2 * de), ex),
                      pl.BlockSpec((None, de, d), ex),
                      pl.BlockSpec((None, 1, d), ex)],
            out_specs=pl.BlockSpec((MOE_BLOCK, d), blk),
            scratch_shapes=[pltpu.VMEM((d, 2 * de), BF16), pltpu.VMEM((de, d), BF16)]),
        compiler_params=_cparams(("arbitrary",)),
        name="expert_blocks",
    )(block_e, n_used, xs, w1, b1, w2, b2)


def _combine_kernel(*refs, final_norm):
    y_refs = refs[:TOP_K]
    rt_ref, x_ref, mod_ref, gfin_ref, o_ref = refs[TOP_K:]
    rt = rt_ref[...]
    acc = jnp.zeros(x_ref.shape, F32)
    for k in range(TOP_K):
        acc = acc + rt[:, 2 * TOP_K + k:2 * TOP_K + k + 1] * y_refs[k][...]
    x_new = x_ref[...] + mod_ref[5:6] * acc
    if final_norm:
        x_new = x_new * lax.rsqrt(jnp.mean(x_new * x_new, axis=-1, keepdims=True) + EPS) * gfin_ref[...]
    o_ref[...] = x_new


def combine(yk, rt, row0, x, mod, g_final, *, final_norm):
    b, t, d = x.shape
    tm = min(t, 512)
    assert row0 % tm == 0
    nt = t // tm
    tok = lambda i, r: row0 // tm + i * nt + r
    row = pl.BlockSpec((None, tm, d), lambda i, r: (i, r, 0))
    y_spec = lambda k: pl.BlockSpec((None, tm, d), lambda i, r: (k, tok(i, r), 0))
    return pl.pallas_call(
        functools.partial(_combine_kernel, final_norm=final_norm),
        out_shape=jax.ShapeDtypeStruct((b, t, d), F32),
        grid=(b, nt),
        in_specs=[y_spec(k) for k in range(TOP_K)] + [
            pl.BlockSpec((tm, 128), lambda i, r: (tok(i, r), 0)), row,
            pl.BlockSpec((None, 6, d), lambda i, r: (i, 0, 0)), pl.BlockSpec((1, d), lambda i, r: (0, 0))],
        out_specs=row,
        compiler_params=_cparams(("arbitrary", "arbitrary")),
        name="combine",
    )(*([yk] * TOP_K), rt, x, mod, g_final)


def _pad_w_in(w_in):
    d = w_in.shape[0]
    n_ba = 4 * DN_HEADS
    pad = jnp.zeros((d, COL_U - COL_BA - n_ba), w_in.dtype)
    return jnp.concatenate([w_in[:, :COL_BA + n_ba], pad, w_in[:, COL_BA + n_ba:]], axis=1).astype(BF16)


def kernel(x, c, ctx, c_ctx, w_ada, b_ada, g_mix, g_ffn, w_in, lru_conv_w, lru_conv_b, lru_w_gate, lru_b_gate, lru_lam, dn_conv_w, dn_a_log, dn_dt_bias, dn_norm_g, s5_lam_re, s5_lam_im, s5_log_dt, s5_b_re, s5_b_im, s5_c_re, s5_c_im, s5_d, s5_w_glu, s5_b_glu, w_br_a, w_br_b, w_br_c, w_out, w_router, b_router, w_e1, b_e1, w_e2, b_e2, g_final):
    bsz, t, d = x.shape
    n_layers = w_in.shape[0]
    s = jnp.concatenate([c, jnp.broadcast_to(c_ctx[None], (SUBLANES, d))], axis=0)
    mods = ada_modulation(s, w_ada.astype(BF16), b_ada)
    x_lat, x_ctx = x, ctx
    for l in range(n_layers):
        last = l == n_layers - 1
        p = _layer_params(l, g_mix, g_ffn, w_in, lru_conv_w, lru_conv_b, lru_w_gate, lru_b_gate, lru_lam, dn_conv_w,
                          dn_a_log, dn_dt_bias, dn_norm_g, s5_lam_re, s5_lam_im, s5_log_dt, s5_b_re, s5_b_im,
                          s5_c_re, s5_c_im, s5_d, s5_w_glu, s5_b_glu, w_br_a, w_br_b, w_br_c, w_out, w_router,
                          b_router)
        m_lat = mods[l, :bsz].reshape(bsz, 6, d)
        m_ctx = jnp.broadcast_to(mods[l, bsz].reshape(1, 6, d), (bsz, 6, d))
        ctx_out, ctx_states = mix_stream(x_ctx, m_ctx, p, None, x_ctx.shape[1], not last)
        lat_out, _ = mix_stream(x_lat, m_lat, p, ctx_states, GRID_W, True)
        w1, b1 = w_e1[l], b_e1[l][:, None, :]
        w2, b2 = w_e2[l], b_e2[l][:, None, :]
        if last:
            x_mid, h, lg = lat_out
            yk, rt = moe(h.reshape(-1, d), lg.reshape(-1, 128), w1, b1, w2, b2)
            x_lat = combine(yk, rt, 0, x_mid, m_lat, g_final[None], final_norm=True)
        else:
            xc_mid, hc, lgc = ctx_out
            xl_mid, hl, lgl = lat_out
            n_ctx = bsz * x_ctx.shape[1]
            yk, rt = moe(jnp.concatenate([hc.reshape(-1, d), hl.reshape(-1, d)], axis=0),
                         jnp.concatenate([lgc.reshape(-1, 128), lgl.reshape(-1, 128)], axis=0), w1, b1, w2, b2)
            x_ctx = combine(yk, rt, 0, xc_mid, m_ctx, g_final[None], final_norm=False)
            x_lat = combine(yk, rt, n_ctx, xl_mid, m_lat, g_final[None], final_norm=False)
    return x_lat


def _layer_params(l, g_mix, g_ffn, w_in, lru_conv_w, lru_conv_b, lru_w_gate, lru_b_gate, lru_lam, dn_conv_w,
                  dn_a_log, dn_dt_bias, dn_norm_g, s5_lam_re, s5_lam_im, s5_log_dt, s5_b_re, s5_b_im, s5_c_re,
                  s5_c_im, s5_d, s5_w_glu, s5_b_glu, w_br_a, w_br_b, w_br_c, w_out, w_router, b_router):
    s5 = [s5_lam_re[l], s5_lam_im[l], s5_log_dt[l], s5_b_re[l], s5_b_im[l], s5_c_re[l], s5_c_im[l]]
    wr = jnp.zeros((w_router.shape[1], 128), F32).at[:, :N_EXPERTS].set(w_router[l])
    wr1 = wr.astype(BF16)
    return {
        "g_mix": g_mix[l][None], "w_in": _pad_w_in(w_in[l]),
        "lru_conv_w": lru_conv_w[l], "lru_conv_b": lru_conv_b[l][None],
        "lru_wg": [_lru_gate_dense(lru_w_gate[l, dr]) for dr in range(2)],
        "lru_bg": lru_b_gate[l].reshape(2, 1, 2 * LRU_WIDTH), "lru_lam": lru_lam[l][:, None, :],
        "dn_conv_w": dn_conv_w[l], "dn_a_log": _dn_lane_vec(dn_a_log[l]), "dn_dt_bias": _dn_lane_vec(dn_dt_bias[l]),
        "s5": [_s5_params(*(a[dr] for a in s5)) for dr in range(2)],
        "dn_norm_g": dn_norm_g[l][None], "s5_d": s5_d[l][None], "s5_w_glu": s5_w_glu[l].astype(BF16),
        "s5_b_glu": s5_b_glu[l][None], "w_br_a": w_br_a[l].astype(BF16), "w_br_b": w_br_b[l].astype(BF16),
        "w_br_c": w_br_c[l].astype(BF16), "w_out": w_out[l].astype(BF16), "g_ffn": g_ffn[l][None],
        "w_r1": wr1, "w_r2": (wr - wr1.astype(F32)).astype(BF16),
        "b_router": jnp.full((1, 128), -1e30, F32).at[0, :N_EXPERTS].set(b_router[l]),
    }


def mix_stream(x, mod, p, init, seg, emit):
    bsz = x.shape[0]
    if init is None:
        zl = jnp.zeros((bsz, LRU_WIDTH), F32)
        zd = jnp.zeros((bsz, DN_HEADS, DN_HEAD_DIM, DN_HEAD_DIM), F32)
        zs = jnp.zeros((bsz, S5_LANES), F32)
        init = ((zl, zl), (zd, zd), ((zs, zs), (zs, zs)))
    proj = inproj(x, mod, p["g_mix"], p["w_in"])
    to_tm = lambda a: jnp.transpose(a, (1, 0, 2))

    ax_tm = to_tm(proj[:, :, COL_AX:COL_AX + LRU_WIDTH])
    lru = lambda dr, prev: lru_scan(ax_tm, p["lru_conv_w"], p["lru_conv_b"], p["lru_wg"][dr], p["lru_bg"][dr],
                                    p["lru_lam"][dr], init[0][dr], prev, seg=seg, reverse=dr == 1)
    h_f, lru_f = lru(0, None)
    h_sum, lru_b = lru(1, h_f)

    dn = lambda dr, prev: dn_scan(proj, p["dn_conv_w"], p["dn_a_log"], p["dn_dt_bias"], init[1][dr], prev,
                                  seg=seg, reverse=dr == 1)
    o_f, dn_f = dn(0, None)
    o_sum, dn_b = dn(1, o_f)

    u_tm = to_tm(proj[:, :, COL_U:COL_U + S5_WIDTH])
    s5 = lambda dr, prev: s5_scan(u_tm, p["s5"][dr], init[2][dr][0], init[2][dr][1], prev, reverse=dr == 1)
    y_f, s5_fr, s5_fi = s5(0, None)
    y_sum, s5_br, s5_bi = s5(1, y_f)

    states = ((lru_f, lru_b), (dn_f, dn_b), ((s5_fr, s5_fi), (s5_br, s5_bi)))
    if not emit:
        return None, states
    return merge_stream(to_tm(h_sum), proj, o_sum, to_tm(y_sum), x, mod, p), states


def moe(h, logits, w1, b1, w2, b2):
    n, d = h.shape
    rt, counts = route(logits)
    idx = rt[:, :TOP_K].astype(jnp.int32)
    rank = rt[:, TOP_K:2 * TOP_K].astype(jnp.int32)
    sizes = counts[0, :N_EXPERTS].astype(jnp.int32)
    padded = ((sizes + MOE_BLOCK - 1) // MOE_BLOCK) * MOE_BLOCK
    ends_pad = jnp.cumsum(padded)
    starts_pad = ends_pad - padded
    dest = (starts_pad[idx] + rank).T.reshape(-1)
    n_blocks = -(-(n * TOP_K) // MOE_BLOCK) + N_EXPERTS
    block_e = jnp.minimum(jnp.searchsorted(ends_pad, jnp.arange(n_blocks) * MOE_BLOCK, side="right"),
                          N_EXPERTS - 1).astype(jnp.int32)
    n_used = (ends_pad[-1:] // MOE_BLOCK).astype(jnp.int32)
    row_tok = jnp.zeros((n_blocks * MOE_BLOCK,), jnp.int32).at[dest].set(
        jnp.tile(jnp.arange(n, dtype=jnp.int32), TOP_K))
    yb = expert_blocks(h[row_tok], block_e, n_used, w1, b1, w2, b2)
    return yb[dest].reshape(TOP_K, n, d), rt
```

```python
import functools
import math

import jax
import jax.numpy as jnp
from jax import lax
from jax.experimental import pallas as pl
from jax.experimental.pallas import tpu as pltpu

F32 = jnp.float32
BF16 = jnp.bfloat16

D_MODEL = 1024
EPS = 1e-6
CONV_W = 4
CONV_LEFT = CONV_W // 2
GRID_W = 64

LRU_WIDTH = 512
LRU_BLOCKS = 8
LRU_BLOCK = LRU_WIDTH // LRU_BLOCKS
LRU_C = 8.0

DN_HEAD_DIM = 128
DN_HEADS = 8
DN_WIDTH = DN_HEADS * DN_HEAD_DIM
DN_CHUNK = 64
DN_BATCH = 2

S5_WIDTH = 512
S5_GROUP = 16
S5_GROUPS = 32
S5_STATE = 64
S5_LANES = S5_GROUPS * S5_STATE
S5_SPLIT = 4

N_EXPERTS = 32
TOP_K = 4
D_EXPERT = 1024
SWIGLU_LIMIT = 7.0
SWIGLU_ALPHA = 1.702
MOE_BLOCK = 512

COL_AX, COL_AY, COL_Q, COL_K, COL_V, COL_Z = 0, 512, 1024, 2048, 3072, 4096
COL_BA, COL_U, COL_GATE, D_IN_PAD = 5120, 5632, 6144, 9216
D_IN = 8736

SUBLANES = 8
VMEM_LIMIT = 52 * 1024 * 1024


def _cparams(sem):
    return pltpu.CompilerParams(dimension_semantics=sem, vmem_limit_bytes=VMEM_LIMIT)


def _sigmoid(x):
    return 1.0 / (1.0 + jnp.exp(-x))


def _softplus(x):
    return jnp.maximum(x, 0.0) + jnp.log1p(jnp.exp(-jnp.abs(x)))


def _silu(x):
    return x * _sigmoid(x)


def _gelu_tanh(x):
    return 0.5 * x * (1.0 + jnp.tanh(math.sqrt(2.0 / math.pi) * (x + 0.044715 * x * x * x)))


def _dot(a, b):
    return jnp.dot(a.astype(BF16), b.astype(BF16), preferred_element_type=F32)


def _ada_kernel(s_ref, w_ref, b_ref, o_ref):
    o_ref[...] = _dot(_silu(s_ref[...]), w_ref[...]) + b_ref[...]


def ada_modulation(s, w_ada, b_ada):
    n_layers, d, n = w_ada.shape
    r = s.shape[0]
    tn = 1536
    return pl.pallas_call(
        _ada_kernel,
        out_shape=jax.ShapeDtypeStruct((n_layers, r, n), F32),
        grid=(n_layers, n // tn),
        in_specs=[pl.BlockSpec((r, d), lambda l, j: (0, 0)),
                  pl.BlockSpec((None, d, tn), lambda l, j: (l, 0, j)),
                  pl.BlockSpec((None, 1, tn), lambda l, j: (l, 0, j))],
        out_specs=pl.BlockSpec((None, r, tn), lambda l, j: (l, 0, j)),
        compiler_params=_cparams(("arbitrary", "arbitrary")),
        name="ada_modulation",
    )(s, w_ada, b_ada.reshape(n_layers, 1, n))


def _norm_mod(x, g, shift, scale):
    y = x * lax.rsqrt(jnp.mean(x * x, axis=-1, keepdims=True) + EPS)
    return (y * g) * (1.0 + scale) + shift


def _inproj_kernel(x_ref, mod_ref, g_ref, w_ref, o_ref, h_ref, *, shift_row):
    @pl.when(pl.program_id(2) == 0)
    def _():
        mod = mod_ref[...]
        h = _norm_mod(x_ref[...], g_ref[...], mod[shift_row:shift_row + 1], mod[shift_row + 1:shift_row + 2])
        h_ref[...] = h.astype(BF16)

    o_ref[...] = jnp.dot(h_ref[...], w_ref[...], preferred_element_type=F32)


def inproj(x, mod, g, w_pad):
    b, t, d = x.shape
    n = w_pad.shape[1]
    tm = min(t, 1024)
    tn = 1536
    return pl.pallas_call(
        functools.partial(_inproj_kernel, shift_row=0),
        out_shape=jax.ShapeDtypeStruct((b, t, n), F32),
        grid=(b, t // tm, n // tn),
        in_specs=[pl.BlockSpec((None, tm, d), lambda i, r, j: (i, r, 0)),
                  pl.BlockSpec((None, 6, d), lambda i, r, j: (i, 0, 0)),
                  pl.BlockSpec((1, d), lambda i, r, j: (0, 0)),
                  pl.BlockSpec((d, tn), lambda i, r, j: (0, j))],
        out_specs=pl.BlockSpec((None, tm, tn), lambda i, r, j: (i, r, j)),
        scratch_shapes=[pltpu.VMEM((tm, d), BF16)],
        compiler_params=_cparams(("arbitrary", "arbitrary", "arbitrary")),
        name="inproj",
    )(x, mod, g, w_pad)


def _conv_time_major(x, w, seg):
    tc = x.shape[0]
    t_in_seg = lax.broadcasted_iota(jnp.int32, x.shape, 0) % seg
    acc = x * w[CONV_LEFT:CONV_LEFT + 1][None]
    for j in range(CONV_W):
        off = j - CONV_LEFT
        if off == 0:
            continue
        if off < 0:
            sh = jnp.concatenate([jnp.zeros((-off,) + x.shape[1:], x.dtype), x[:tc + off]], axis=0)
            ok = t_in_seg >= -off
        else:
            sh = jnp.concatenate([x[off:], jnp.zeros((off,) + x.shape[1:], x.dtype)], axis=0)
            ok = t_in_seg < seg - off
        acc = acc + jnp.where(ok, sh, 0.0) * w[j:j + 1][None]
    return acc


def _lru_kernel(x_ref, cw_ref, cb_ref, wg_ref, bg_ref, lam_ref, h0_ref, *rest, seg, reverse, add_prev):
    if add_prev:
        prev_ref, o_ref, hfin_ref, a_ref, b_ref, h_ref = rest
    else:
        o_ref, hfin_ref, a_ref, b_ref, h_ref = rest
    tc, bc, w = x_ref.shape
    step = pl.program_id(1)

    @pl.when(step == 0)
    def _():
        h_ref[...] = h0_ref[...]

    xc = _conv_time_major(x_ref[...], cw_ref[...], seg) + cb_ref[...][None]
    gates = _dot(xc.reshape(tc * bc, w), wg_ref[...]) + bg_ref[...]
    r = _sigmoid(gates[:, :w]).reshape(tc, bc, w)
    i = _sigmoid(gates[:, w:]).reshape(tc, bc, w)
    log_a = (-LRU_C * _softplus(-lam_ref[...]))[None] * r
    a_ref[...] = jnp.exp(log_a)
    b_ref[...] = jnp.sqrt(1.0 - jnp.exp(2.0 * log_a)) * (i * xc)

    def body(s, h):
        t = tc - 1 - s if reverse else s
        h = a_ref[t] * h + b_ref[t]
        if add_prev:
            o_ref[t] = h + prev_ref[t]
        else:
            o_ref[t] = h
        return h

    h = lax.fori_loop(0, tc, body, h_ref[...], unroll=8)
    h_ref[...] = h
    hfin_ref[...] = h


def lru_scan(x_tm, conv_w, conv_b, wg, bg, lam, h0, prev, *, seg, reverse):
    t, b, w = x_tm.shape
    tc = min(t, 256)
    bc = SUBLANES
    nt = t // tc
    tmap = (lambda i, s: (nt - 1 - s, i, 0)) if reverse else (lambda i, s: (s, i, 0))
    blk = pl.BlockSpec((tc, bc, w), tmap)
    const = lambda shape: pl.BlockSpec(shape, lambda i, s: (0,) * len(shape))
    in_specs = [blk, const((CONV_W, w)), const((1, w)), const((w, 2 * w)), const((1, 2 * w)), const((1, w)),
                pl.BlockSpec((bc, w), lambda i, s: (i, 0))]
    args = [x_tm, conv_w, conv_b, wg, bg, lam, h0]
    if prev is not None:
        in_specs.append(blk)
        args.append(prev)
    return pl.pallas_call(
        functools.partial(_lru_kernel, seg=seg, reverse=reverse, add_prev=prev is not None),
        out_shape=(jax.ShapeDtypeStruct((t, b, w), F32), jax.ShapeDtypeStruct((b, w), F32)),
        grid=(b // bc, nt),
        in_specs=in_specs,
        out_specs=(blk, pl.BlockSpec((bc, w), lambda i, s: (i, 0))),
        scratch_shapes=[pltpu.VMEM((tc, bc, w), F32), pltpu.VMEM((tc, bc, w), F32), pltpu.VMEM((bc, w), F32)],
        compiler_params=_cparams(("arbitrary", "arbitrary")),
        name="lru_scan_bwd" if reverse else "lru_scan_fwd",
    )(*args)


def _lru_gate_dense(w_gate):
    eye = jnp.eye(LRU_BLOCKS, dtype=w_gate.dtype)
    dense = jnp.einsum("gnij,nm->gnimj", w_gate, eye).reshape(2, LRU_WIDTH, LRU_WIDTH)
    return jnp.concatenate([dense[0], dense[1]], axis=1).astype(BF16)


def _s5_kernel(u_ref, bre_ref, bim_ref, cre_ref, cim_ref, ar_ref, ai_ref, h0r_ref, h0i_ref, *rest,
               reverse, add_prev):
    if add_prev:
        prev_ref, o_ref, fr_ref, fi_ref, xr_ref, xi_ref, hr_ref, hi_ref = rest
    else:
        o_ref, fr_ref, fi_ref, xr_ref, xi_ref, hr_ref, hi_ref = rest
    tc, bc, w = u_ref.shape
    blk_in = w // S5_SPLIT
    blk_st = S5_LANES // S5_SPLIT

    @pl.when(pl.program_id(1) == 0)
    def _():
        hr_ref[...] = h0r_ref[...]
        hi_ref[...] = h0i_ref[...]

    u2 = u_ref[...].reshape(tc * bc, w).astype(BF16)
    for j in range(S5_SPLIT):
        uj = u2[:, j * blk_in:(j + 1) * blk_in]
        xr_ref[:, :, j * blk_st:(j + 1) * blk_st] = jnp.dot(
            uj, bre_ref[j], preferred_element_type=F32).reshape(tc, bc, blk_st)
        xi_ref[:, :, j * blk_st:(j + 1) * blk_st] = jnp.dot(
            uj, bim_ref[j], preferred_element_type=F32).reshape(tc, bc, blk_st)

    ar = jnp.broadcast_to(ar_ref[...], (bc, S5_LANES))
    ai = jnp.broadcast_to(ai_ref[...], (bc, S5_LANES))

    def body(s, carry):
        hr, hi = carry
        t = tc - 1 - s if reverse else s
        nr = ar * hr - ai * hi + xr_ref[t]
        ni = ar * hi + ai * hr + xi_ref[t]
        xr_ref[t] = nr
        xi_ref[t] = ni
        return nr, ni

    hr, hi = lax.fori_loop(0, tc, body, (hr_ref[...], hi_ref[...]))
    hr_ref[...] = hr
    hi_ref[...] = hi
    fr_ref[...] = hr
    fi_ref[...] = hi

    for j in range(S5_SPLIT):
        hrj = xr_ref[:, :, j * blk_st:(j + 1) * blk_st].reshape(tc * bc, blk_st).astype(BF16)
        hij = xi_ref[:, :, j * blk_st:(j + 1) * blk_st].reshape(tc * bc, blk_st).astype(BF16)
        y = (jnp.dot(hrj, cre_ref[j], preferred_element_type=F32)
             - jnp.dot(hij, cim_ref[j], preferred_element_type=F32)).reshape(tc, bc, blk_in)
        if add_prev:
            y = y + prev_ref[:, :, j * blk_in:(j + 1) * blk_in]
        o_ref[:, :, j * blk_in:(j + 1) * blk_in] = y


def s5_scan(u_tm, prm, h0r, h0i, prev, *, reverse):
    t, b, w = u_tm.shape
    tc = min(t, 64)
    bc = SUBLANES
    nt = t // tc
    tmap = (lambda i, s: (nt - 1 - s, i, 0)) if reverse else (lambda i, s: (s, i, 0))
    blk = pl.BlockSpec((tc, bc, w), tmap)
    const = lambda shape: pl.BlockSpec(shape, lambda i, s: (0,) * len(shape))
    st = pl.BlockSpec((bc, S5_LANES), lambda i, s: (i, 0))
    bre, bim, cre, cim, ar, ai = prm
    in_specs = [blk, const(bre.shape), const(bim.shape), const(cre.shape), const(cim.shape),
                const(ar.shape), const(ai.shape), st, st]
    args = [u_tm, bre, bim, cre, cim, ar, ai, h0r, h0i]
    if prev is not None:
        in_specs.append(blk)
        args.append(prev)
    st_shape = jax.ShapeDtypeStruct((b, S5_LANES), F32)
    return pl.pallas_call(
        functools.partial(_s5_kernel, reverse=reverse, add_prev=prev is not None),
        out_shape=(jax.ShapeDtypeStruct((t, b, w), F32), st_shape, st_shape),
        grid=(b // bc, nt),
        in_specs=in_specs,
        out_specs=(blk, st, st),
        scratch_shapes=[pltpu.VMEM((tc, bc, S5_LANES), F32), pltpu.VMEM((tc, bc, S5_LANES), F32),
                        pltpu.VMEM((bc, S5_LANES), F32), pltpu.VMEM((bc, S5_LANES), F32)],
        compiler_params=_cparams(("arbitrary", "arbitrary")),
        name="s5_scan_bwd" if reverse else "s5_scan_fwd",
    )(*args)


def _s5_params(lam_re, lam_im, log_dt, b_re, b_im, c_re, c_im):
    dt = jnp.exp(log_dt)[:, None]
    mag = jnp.exp(lam_re * dt)
    ar, ai = mag * jnp.cos(lam_im * dt), mag * jnp.sin(lam_im * dt)
    den = lam_re * lam_re + lam_im * lam_im
    fr = ((ar - 1.0) * lam_re + ai * lam_im) / den
    fi = (ai * lam_re - (ar - 1.0) * lam_im) / den
    bb_re = fr[..., None] * b_re - fi[..., None] * b_im
    bb_im = fr[..., None] * b_im + fi[..., None] * b_re
    gl = S5_GROUPS // S5_SPLIT
    eye = jnp.eye(gl, dtype=F32)

    def pack_in(bb):
        bb = jnp.transpose(bb, (0, 2, 1)).reshape(S5_SPLIT, gl, S5_GROUP, S5_STATE)
        return jnp.einsum("jghp,gm->jghmp", bb, eye).reshape(S5_SPLIT, gl * S5_GROUP, gl * S5_STATE).astype(BF16)

    def pack_out(c):
        c = jnp.transpose(c, (0, 2, 1)).reshape(S5_SPLIT, gl, S5_STATE, S5_GROUP)
        return jnp.einsum("jgph,gm->jgpmh", c, eye).reshape(S5_SPLIT, gl * S5_STATE, gl * S5_GROUP).astype(BF16)

    return (pack_in(bb_re), pack_in(bb_im), pack_out(c_re), pack_out(c_im),
            ar.reshape(1, S5_LANES), ai.reshape(1, S5_LANES))


def _conv_rows(x, w, seg):
    tc = x.shape[0]
    t_in_seg = lax.broadcasted_iota(jnp.int32, x.shape, 0) % seg
    acc = x * w[CONV_LEFT:CONV_LEFT + 1]
    for j in range(CONV_W):
        off = j - CONV_LEFT
        if off == 0:
            continue
        sh = pltpu.roll(x, (-off) % tc, 0)
        ok = (t_in_seg >= -off) if off < 0 else (t_in_seg < seg - off)
        acc = acc + jnp.where(ok, sh, 0.0) * w[j:j + 1]
    return acc


def _l2norm(x):
    return x * lax.rsqrt(jnp.sum(x * x, axis=-1, keepdims=True) + EPS)


def _dot_nt(a, b):
    return lax.dot_general(a.astype(BF16), b.astype(BF16), (((1,), (1,)), ((), ())), preferred_element_type=F32)


def _dot_tn(a, b):
    return lax.dot_general(a.astype(BF16), b.astype(BF16), (((0,), (0,)), ((), ())), preferred_element_type=F32)


def _split3_dot(m, x):
    x1 = x.astype(BF16)
    r1 = x - x1.astype(F32)
    x2 = r1.astype(BF16)
    x3 = (r1 - x2.astype(F32)).astype(BF16)
    mb = m.astype(BF16)
    dot = lambda v: jnp.dot(mb, v, preferred_element_type=F32)
    return dot(x1) + dot(x2) + dot(x3)


def _dn_kernel(q_ref, k_ref, v_ref, ba_ref, cwq_ref, cwk_ref, cwv_ref, alog_ref, dtb_ref, s0_ref, *rest,
               seg, reverse, add_prev, dirn):
    if add_prev:
        prev_ref, o_ref, sfin_ref, qc_ref, kc_ref, vc_ref, beta_ref, g_ref, s_ref = rest
    else:
        o_ref, sfin_ref, qc_ref, kc_ref, vc_ref, beta_ref, g_ref, s_ref = rest
    nb, tc, wq = q_ref.shape
    n_chunks = tc // DN_CHUNK
    step = pl.program_id(1)

    @pl.when(step == 0)
    def _():
        s_ref[...] = s0_ref[...]

    rows_all = lambda ref: ref[...].reshape(nb * tc, ref.shape[-1])
    qc_ref[...] = _silu(_conv_rows(rows_all(q_ref), cwq_ref[...], seg))
    kc_ref[...] = _silu(_conv_rows(rows_all(k_ref), cwk_ref[...], seg))
    vc_ref[...] = _silu(_conv_rows(rows_all(v_ref), cwv_ref[...], seg))
    ba = rows_all(ba_ref)
    beta_ref[...] = _sigmoid(ba)
    g_ref[...] = -jnp.exp(alog_ref[...]) * _softplus(ba + dtb_ref[...])

    ii = lax.broadcasted_iota(jnp.int32, (DN_CHUNK, DN_CHUNK), 0)
    jj = lax.broadcasted_iota(jnp.int32, (DN_CHUNK, DN_CHUNK), 1)
    incl = (ii <= jj) if reverse else (ii >= jj)
    strict = (ii < jj) if reverse else (ii > jj)
    eye = (ii == jj).astype(F32)
    tri = incl.astype(F32)
    last = 0 if reverse else DN_CHUNK - 1
    neg = jnp.float32(-1e30)

    def chunk(ci, carry):
        cidx = n_chunks - 1 - ci if reverse else ci
        r0 = pl.multiple_of(cidx * DN_CHUNK, DN_CHUNK)
        rows = pl.ds(r0, DN_CHUNK)
        each = lambda f, *cols: [f(*args) for args in zip(*cols)]
        srows = [pl.ds(pl.multiple_of(bb * tc + r0, DN_CHUNK), DN_CHUNK) for bb in range(nb)]
        g_cum = [_split3_dot(tri, g_ref[sr, :]) for sr in srows]
        g_t = [g.T for g in g_cum]
        e_g = [jnp.exp(g) for g in g_cum]
        e_rest = [jnp.exp(g[last:last + 1] - g) for g in g_cum]
        beta_all = [beta_ref[sr, :] for sr in srows]
        pairs = [(bb, h) for bb in range(nb) for h in range(DN_HEADS)]
        lanes = [slice(h * DN_HEAD_DIM, (h + 1) * DN_HEAD_DIM) for _, h in pairs]
        cb = [dirn * DN_HEADS + h for _, h in pairs]
        cg = [2 * DN_HEADS + c for c in cb]
        q = [_l2norm(qc_ref[srows[bb], ln]) * (DN_HEAD_DIM ** -0.5) for (bb, _), ln in zip(pairs, lanes)]
        k = [_l2norm(kc_ref[srows[bb], ln]) for (bb, _), ln in zip(pairs, lanes)]
        v = [vc_ref[srows[bb], ln] for (bb, _), ln in zip(pairs, lanes)]
        beta = [beta_all[bb][:, c:c + 1] for (bb, _), c in zip(pairs, cb)]
        decay = [jnp.exp(jnp.where(incl, g_cum[bb][:, c:c + 1] - g_t[bb][c:c + 1, :], neg))
                 for (bb, _), c in zip(pairs, cg)]
        nc = DN_CHUNK
        kb = each(lambda a, b: a * b, k, beta)
        kq = each(lambda a, b, c: _dot_nt(jnp.concatenate([a, b], axis=0), c), kb, q, k)
        a_low = each(lambda r, dc: jnp.where(strict, r[:nc] * dc, 0.0), kq, decay)
        attn = each(lambda r, dc: jnp.where(incl, r[nc:] * dc, 0.0), kq, decay)
        x = [eye - a for a in a_low]
        p = [_dot(a, a) for a in a_low]
        for _ in range(4):
            xp = each(lambda a, b: _dot(jnp.concatenate([a, b], axis=0), b), x, p)
            x = each(lambda a, r: a + r[:nc], x, xp)
            p = [r[nc:] for r in xp]
        x = each(lambda a, b: a + _dot(a, b), x, p)
        e_col = [e_g[bb][:, c:c + 1] for (bb, _), c in zip(pairs, cg)]
        uw = each(lambda a, vv, bt, kk, ec: _dot(a, jnp.concatenate([vv * bt, kk * ec], axis=1)),
                  x, v, beta, kb, e_col)
        q_dec = each(lambda a, b: a * b, q, e_col)
        k_dec = [kk * e_rest[bb][:, c:c + 1] for kk, (bb, _), c in zip(k, pairs, cg)]
        e_tot = [e_g[bb][last:last + 1, c:c + 1] for (bb, _), c in zip(pairs, cg)]
        s = [s_ref[bb, h] for bb, h in pairs]
        ws = each(lambda r, qd, ss: _dot(jnp.concatenate([r[:, DN_HEAD_DIM:], qd], axis=0), ss), uw, q_dec, s)
        v_new = each(lambda r, m: r[:, :DN_HEAD_DIM] - m[:nc], uw, ws)
        o = each(lambda m, at, vn: m[nc:] + _dot(at, vn), ws, attn, v_new)
        s_new = each(lambda ss, et, kd, vn: ss * et + _dot_tn(kd, vn), s, e_tot, k_dec, v_new)
        if add_prev:
            o = [oo + prev_ref[bb, rows, ln] for oo, (bb, _), ln in zip(o, pairs, lanes)]
        for (bb, h), ss in zip(pairs, s_new):
            s_ref[bb, h] = ss
        for (bb, _), ln, oo in zip(pairs, lanes, o):
            o_ref[bb, rows, ln] = oo
        return carry

    lax.fori_loop(0, n_chunks, chunk, 0)

    @pl.when(step == pl.num_programs(1) - 1)
    def _():
        sfin_ref[...] = s_ref[...]


def dn_scan(proj, conv_w, a_log, dt_bias, s0, prev, *, seg, reverse):
    b, t, _ = proj.shape
    tc = min(t, 256)
    nb = DN_BATCH
    assert tc % seg == 0 and t % tc == 0 and b % nb == 0
    nt = t // tc
    dirn = 1 if reverse else 0
    tix = (lambda s: nt - 1 - s) if reverse else (lambda s: s)
    wq = DN_WIDTH
    col = lambda cb: pl.BlockSpec((nb, tc, wq), lambda i, s: (i, tix(s), cb))
    cw = lambda cb: pl.BlockSpec((CONV_W, wq), lambda i, s: (0, cb))
    vec = pl.BlockSpec((1, 128), lambda i, s: (0, 0))
    st = pl.BlockSpec((nb, DN_HEADS, DN_HEAD_DIM, DN_HEAD_DIM), lambda i, s: (i, 0, 0, 0))
    oblk = pl.BlockSpec((nb, tc, wq), lambda i, s: (i, tix(s), 0))
    in_specs = [col(COL_Q // wq), col(COL_K // wq), col(COL_V // wq),
                pl.BlockSpec((nb, tc, 128), lambda i, s: (i, tix(s), COL_BA // 128)),
                cw(0), cw(1), cw(2), vec, vec, st]
    args = [proj, proj, proj, proj, conv_w, conv_w, conv_w, a_log, dt_bias, s0]
    if prev is not None:
        in_specs.append(oblk)
        args.append(prev)
    return pl.pallas_call(
        functools.partial(_dn_kernel, seg=seg, reverse=reverse, add_prev=prev is not None, dirn=dirn),
        out_shape=(jax.ShapeDtypeStruct((b, t, wq), F32),
                   jax.ShapeDtypeStruct((b, DN_HEADS, DN_HEAD_DIM, DN_HEAD_DIM), F32)),
        grid=(b // nb, nt),
        in_specs=in_specs,
        out_specs=(oblk, st),
        scratch_shapes=[pltpu.VMEM((nb * tc, wq), F32), pltpu.VMEM((nb * tc, wq), F32),
                        pltpu.VMEM((nb * tc, wq), F32), pltpu.VMEM((nb * tc, 128), F32),
                        pltpu.VMEM((nb * tc, 128), F32),
                        pltpu.VMEM((nb, DN_HEADS, DN_HEAD_DIM, DN_HEAD_DIM), F32)],
        compiler_params=_cparams(("arbitrary", "arbitrary")),
        name="dn_scan_bwd" if reverse else "dn_scan_fwd",
    )(*args)


def _dn_lane_vec(p):
    return jnp.zeros((1, 128), F32).at[0, 2 * DN_HEADS:4 * DN_HEADS].set(p.reshape(-1))


def _merge_kernel(hl_ref, ay_ref, od_ref, z_ref, ys_ref, u_ref, gt_ref, x_ref, mod_ref, ng_ref, sd_ref,
                  wglu_ref, bglu_ref, wa_ref, wb_ref, wc_ref, wo_ref, gf_ref, wr1_ref, wr2_ref, br_ref,
                  xo_ref, h_ref, lg_ref):
    d = x_ref.shape[-1]
    mod = mod_ref[...]
    y_a = hl_ref[...] * _gelu_tanh(ay_ref[...])
    acc = _sigmoid(gt_ref[:, 0:d]) * _dot(y_a, wa_ref[...])

    ng = ng_ref[...]
    heads = []
    for h in range(DN_HEADS):
        lanes = slice(h * DN_HEAD_DIM, (h + 1) * DN_HEAD_DIM)
        o = od_ref[:, lanes]
        o = o * lax.rsqrt(jnp.mean(o * o, axis=-1, keepdims=True) + EPS) * ng
        heads.append((o * _silu(z_ref[:, lanes])).astype(BF16))
    y_b = jnp.concatenate(heads, axis=-1)
    acc = acc + _sigmoid(gt_ref[:, d:2 * d]) * jnp.dot(y_b, wb_ref[...], preferred_element_type=F32)

    y_c = _gelu_tanh(ys_ref[...] + sd_ref[...] * u_ref[...])
    y_c = y_c * _sigmoid(_dot(y_c, wglu_ref[...]) + bglu_ref[...])
    acc = acc + _sigmoid(gt_ref[:, 2 * d:3 * d]) * _dot(y_c, wc_ref[...])

    x_new = x_ref[...] + mod[2:3] * _dot(acc, wo_ref[...])
    xo_ref[...] = x_new
    h = _norm_mod(x_new, gf_ref[...], mod[3:4], mod[4:5])
    h1 = h.astype(BF16)
    h2 = (h - h1.astype(F32)).astype(BF16)
    h_ref[...] = h1
    lg_ref[...] = (jnp.dot(h1, wr1_ref[...], preferred_element_type=F32)
                   + jnp.dot(h1, wr2_ref[...], preferred_element_type=F32)
                   + jnp.dot(h2, wr1_ref[...], preferred_element_type=F32)) + br_ref[...]


def merge_stream(h_lru, proj, o_dn, y_s5, x, mod, p):
    b, t, d = x.shape
    tm = min(t, 256)
    row = lambda w, cb: pl.BlockSpec((None, tm, w), lambda i, r: (i, r, cb))
    const = lambda a: pl.BlockSpec(a.shape, lambda i, r: (0,) * a.ndim)
    consts = [p["dn_norm_g"], p["s5_d"], p["s5_w_glu"], p["s5_b_glu"], p["w_br_a"], p["w_br_b"], p["w_br_c"],
              p["w_out"], p["g_ffn"], p["w_r1"], p["w_r2"], p["b_router"]]
    return pl.pallas_call(
        _merge_kernel,
        out_shape=(jax.ShapeDtypeStruct((b, t, d), F32), jax.ShapeDtypeStruct((b, t, d), BF16),
                   jax.ShapeDtypeStruct((b, t, 128), F32)),
        grid=(b, t // tm),
        in_specs=[row(LRU_WIDTH, 0), row(LRU_WIDTH, COL_AY // LRU_WIDTH), row(DN_WIDTH, 0),
                  row(DN_WIDTH, COL_Z // DN_WIDTH), row(S5_WIDTH, 0), row(S5_WIDTH, COL_U // S5_WIDTH),
                  row(3 * d, COL_GATE // (3 * d)), row(d, 0),
                  pl.BlockSpec((None, 6, d), lambda i, r: (i, 0, 0))] + [const(a) for a in consts],
        out_specs=(row(d, 0), row(d, 0), row(128, 0)),
        compiler_params=_cparams(("arbitrary", "arbitrary")),
        name="merge_stream",
    )(h_lru, proj, o_dn, proj, y_s5, proj, proj, x, mod, *consts)


def _route_kernel(lg_ref, o_ref, cnt_ref, carry_ref):
    tm = lg_ref.shape[0]

    @pl.when(pl.program_id(0) == 0)
    def _():
        carry_ref[...] = jnp.zeros_like(carry_ref)

    logits = lg_ref[...]
    lane = lax.broadcasted_iota(jnp.int32, logits.shape, 1)
    neg = jnp.float32(-jnp.inf)
    vals, idxs = [], []
    sel = jnp.zeros(logits.shape, F32)
    for _ in range(TOP_K):
        m = jnp.max(logits, axis=-1, keepdims=True)
        idx = jnp.min(jnp.where(logits == m, lane, 128), axis=-1, keepdims=True)
        hit = lane == idx
        logits = jnp.where(hit, neg, logits)
        sel = jnp.where(hit, 1.0, sel)
        vals.append(m)
        idxs.append(idx)
    exps = [jnp.exp(v - vals[0]) for v in vals]
    inv = 1.0 / (exps[0] + exps[1] + exps[2] + exps[3])

    ii = lax.broadcasted_iota(jnp.int32, (tm, tm), 0)
    jj = lax.broadcasted_iota(jnp.int32, (tm, tm), 1)
    before = jnp.dot((ii > jj).astype(BF16), sel.astype(BF16), preferred_element_type=F32) + carry_ref[...]
    out = jnp.zeros(logits.shape, F32)
    for k in range(TOP_K):
        rank = jnp.sum(jnp.where(lane == idxs[k], before, 0.0), axis=-1, keepdims=True)
        out = jnp.where(lane == k, idxs[k].astype(F32), out)
        out = jnp.where(lane == TOP_K + k, rank, out)
        out = jnp.where(lane == 2 * TOP_K + k, exps[k] * inv, out)
    o_ref[...] = out
    carry_ref[...] = carry_ref[...] + jnp.sum(sel, axis=0, keepdims=True)
    cnt_ref[...] = carry_ref[...]


def route(logits):
    n = logits.shape[0]
    tm = 512
    return pl.pallas_call(
        _route_kernel,
        out_shape=(jax.ShapeDtypeStruct((n, 128), F32), jax.ShapeDtypeStruct((1, 128), F32)),
        grid=(n // tm,),
        in_specs=[pl.BlockSpec((tm, 128), lambda i: (i, 0))],
        out_specs=(pl.BlockSpec((tm, 128), lambda i: (i, 0)), pl.BlockSpec((1, 128), lambda i: (0, 0))),
        scratch_shapes=[pltpu.VMEM((1, 128), F32)],
        compiler_params=_cparams(("arbitrary",)),
        name="route",
    )(logits)


def _expert_kernel(be_ref, nu_ref, x_ref, w1_ref, b1_ref, w2_ref, b2_ref, o_ref, w1b_ref, w2b_ref):
    i = pl.program_id(0)

    @pl.when(i < nu_ref[0])
    def _():
        @pl.when(jnp.logical_or(i == 0, be_ref[i] != be_ref[jnp.maximum(i - 1, 0)]))
        def _():
            w1b_ref[...] = w1_ref[...].astype(BF16)
            w2b_ref[...] = w2_ref[...].astype(BF16)

        de = w2_ref.shape[0]
        gu = jnp.dot(x_ref[...], w1b_ref[...], preferred_element_type=F32) + b1_ref[...]
        glu = jnp.minimum(gu[:, :de], SWIGLU_LIMIT)
        lin = jnp.clip(gu[:, de:], -SWIGLU_LIMIT, SWIGLU_LIMIT)
        act = glu * _sigmoid(SWIGLU_ALPHA * glu) * (lin + 1.0)
        o_ref[...] = jnp.dot(act.astype(BF16), w2b_ref[...], preferred_element_type=F32) + b2_ref[...]


def expert_blocks(xs, block_e, n_used, w1, b1, w2, b2):
    rows, d = xs.shape
    n_blocks = rows // MOE_BLOCK
    de = w2.shape[1]
    blk = lambda i, be, nu: (jnp.minimum(i, nu[0] - 1), 0)
    ex = lambda i, be, nu: (be[jnp.minimum(i, nu[0] - 1)], 0, 0)
    return pl.pallas_call(
        _expert_kernel,
        out_shape=jax.ShapeDtypeStruct((rows, d), F32),
        grid_spec=pltpu.PrefetchScalarGridSpec(
            num_scalar_prefetch=2,
            grid=(n_blocks,),
            in_specs=[pl.BlockSpec((MOE_BLOCK, d), blk),
                      pl.BlockSpec((None, d, 2 * de), ex),
                      pl.BlockSpec((None, 1, 2 * de), ex),
                      pl.BlockSpec((None, de, d), ex),
                      pl.BlockSpec((None, 1, d), ex)],
            out_specs=pl.BlockSpec((MOE_BLOCK, d), blk),
            scratch_shapes=[pltpu.VMEM((d, 2 * de), BF16), pltpu.VMEM((de, d), BF16)]),
        compiler_params=_cparams(("arbitrary",)),
        name="expert_blocks",
    )(block_e, n_used, xs, w1, b1, w2, b2)


def _combine_kernel(*refs, final_norm):
    y_refs = refs[:TOP_K]
    rt_ref, x_ref, mod_ref, gfin_ref, o_ref = refs[TOP_K:]
    rt = rt_ref[...]
    acc = jnp.zeros(x_ref.shape, F32)
    for k in range(TOP_K):
        acc = acc + rt[:, 2 * TOP_K + k:2 * TOP_K + k + 1] * y_refs[k][...]
    x_new = x_ref[...] + mod_ref[5:6] * acc
    if final_norm:
        x_new = x_new * lax.rsqrt(jnp.mean(x_new * x_new, axis=-1, keepdims=True) + EPS) * gfin_ref[...]
    o_ref[...] = x_new


def combine(yk, rt, row0, x, mod, g_final, *, final_norm):
    b, t, d = x.shape
    tm = min(t, 512)
    assert row0 % tm == 0
    nt = t // tm
    tok = lambda i, r: row0 // tm + i * nt + r
    row = pl.BlockSpec((None, tm, d), lambda i, r: (i, r, 0))
    y_spec = lambda k: pl.BlockSpec((None, tm, d), lambda i, r: (k, tok(i, r), 0))
    return pl.pallas_call(
        functools.partial(_combine_kernel, final_norm=final_norm),
        out_shape=jax.ShapeDtypeStruct((b, t, d), F32),
        grid=(b, nt),
        in_specs=[y_spec(k) for k in range(TOP_K)] + [
            pl.BlockSpec((tm, 128), lambda i, r: (tok(i, r), 0)), row,
            pl.BlockSpec((None, 6, d), lambda i, r: (i, 0, 0)), pl.BlockSpec((1, d), lambda i, r: (0, 0))],
        out_specs=row,
        compiler_params=_cparams(("arbitrary", "arbitrary")),
        name="combine",
    )(*([yk] * TOP_K), rt, x, mod, g_final)


def _pad_w_in(w_in):
    d = w_in.shape[0]
    n_ba = 4 * DN_HEADS
    pad = jnp.zeros((d, COL_U - COL_BA - n_ba), w_in.dtype)
    return jnp.concatenate([w_in[:, :COL_BA + n_ba], pad, w_in[:, COL_BA + n_ba:]], axis=1).astype(BF16)


def kernel(x, c, ctx, c_ctx, w_ada, b_ada, g_mix, g_ffn, w_in, lru_conv_w, lru_conv_b, lru_w_gate, lru_b_gate, lru_lam, dn_conv_w, dn_a_log, dn_dt_bias, dn_norm_g, s5_lam_re, s5_lam_im, s5_log_dt, s5_b_re, s5_b_im, s5_c_re, s5_c_im, s5_d, s5_w_glu, s5_b_glu, w_br_a, w_br_b, w_br_c, w_out, w_router, b_router, w_e1, b_e1, w_e2, b_e2, g_final):
    bsz, t, d = x.shape
    n_layers = w_in.shape[0]
    s = jnp.concatenate([c, jnp.broadcast_to(c_ctx[None], (SUBLANES, d))], axis=0)
    mods = ada_modulation(s, w_ada.astype(BF16), b_ada)
    x_lat, x_ctx = x, ctx
    for l in range(n_layers):
        last = l == n_layers - 1
        p = _layer_params(l, g_mix, g_ffn, w_in, lru_conv_w, lru_conv_b, lru_w_gate, lru_b_gate, lru_lam, dn_conv_w,
                          dn_a_log, dn_dt_bias, dn_norm_g, s5_lam_re, s5_lam_im, s5_log_dt, s5_b_re, s5_b_im,
                          s5_c_re, s5_c_im, s5_d, s5_w_glu, s5_b_glu, w_br_a, w_br_b, w_br_c, w_out, w_router,
                          b_router)
        m_lat = mods[l, :bsz].reshape(bsz, 6, d)
        m_ctx = jnp.broadcast_to(mods[l, bsz].reshape(1, 6, d), (bsz, 6, d))
        ctx_out, ctx_states = mix_stream(x_ctx, m_ctx, p, None, x_ctx.shape[1], not last)
        lat_out, _ = mix_stream(x_lat, m_lat, p, ctx_states, GRID_W, True)
        w1, b1 = w_e1[l], b_e1[l][:, None, :]
        w2, b2 = w_e2[l], b_e2[l][:, None, :]
        if last:
            x_mid, h, lg = lat_out
            yk, rt = moe(h.reshape(-1, d), lg.reshape(-1, 128), w1, b1, w2, b2)
            x_lat = combine(yk, rt, 0, x_mid, m_lat, g_final[None], final_norm=True)
        else:
            xc_mid, hc, lgc = ctx_out
            xl_mid, hl, lgl = lat_out
            n_ctx = bsz * x_ctx.shape[1]
            yk, rt = moe(jnp.concatenate([hc.reshape(-1, d), hl.reshape(-1, d)], axis=0),
                         jnp.concatenate([lgc.reshape(-1, 128), lgl.reshape(-1, 128)], axis=0), w1, b1, w2, b2)
            x_ctx = combine(yk, rt, 0, xc_mid, m_ctx, g_final[None], final_norm=False)
            x_lat = combine(yk, rt, n_ctx, xl_mid, m_lat, g_final[None], final_norm=False)
    return x_lat


def _layer_params(l, g_mix, g_ffn, w_in, lru_conv_w, lru_conv_b, lru_w_gate, lru_b_gate, lru_lam, dn_conv_w,
                  dn_a_log, dn_dt_bias, dn_norm_g, s5_lam_re, s5_lam_im, s5_log_dt, s5_b_re, s5_b_im, s5_c_re,
                  s5_c_im, s5_d, s5_w_glu, s5_b_glu, w_br_a, w_br_b, w_br_c, w_out, w_router, b_router):
    s5 = [s5_lam_re[l], s5_lam_im[l], s5_log_dt[l], s5_b_re[l], s5_b_im[l], s5_c_re[l], s5_c_im[l]]
    wr = jnp.zeros((w_router.shape[1], 128), F32).at[:, :N_EXPERTS].set(w_router[l])
    wr1 = wr.astype(BF16)
    return {
        "g_mix": g_mix[l][None], "w_in": _pad_w_in(w_in[l]),
        "lru_conv_w": lru_conv_w[l], "lru_conv_b": lru_conv_b[l][None],
        "lru_wg": [_lru_gate_dense(lru_w_gate[l, dr]) for dr in range(2)],
        "lru_bg": lru_b_gate[l].reshape(2, 1, 2 * LRU_WIDTH), "lru_lam": lru_lam[l][:, None, :],
        "dn_conv_w": dn_conv_w[l], "dn_a_log": _dn_lane_vec(dn_a_log[l]), "dn_dt_bias": _dn_lane_vec(dn_dt_bias[l]),
        "s5": [_s5_params(*(a[dr] for a in s5)) for dr in range(2)],
        "dn_norm_g": dn_norm_g[l][None], "s5_d": s5_d[l][None], "s5_w_glu": s5_w_glu[l].astype(BF16),
        "s5_b_glu": s5_b_glu[l][None], "w_br_a": w_br_a[l].astype(BF16), "w_br_b": w_br_b[l].astype(BF16),
        "w_br_c": w_br_c[l].astype(BF16), "w_out": w_out[l].astype(BF16), "g_ffn": g_ffn[l][None],
        "w_r1": wr1, "w_r2": (wr - wr1.astype(F32)).astype(BF16),
        "b_router": jnp.full((1, 128), -1e30, F32).at[0, :N_EXPERTS].set(b_router[l]),
    }


def mix_stream(x, mod, p, init, seg, emit):
    bsz = x.shape[0]
    if init is None:
        zl = jnp.zeros((bsz, LRU_WIDTH), F32)
        zd = jnp.zeros((bsz, DN_HEADS, DN_HEAD_DIM, DN_HEAD_DIM), F32)
        zs = jnp.zeros((bsz, S5_LANES), F32)
        init = ((zl, zl), (zd, zd), ((zs, zs), (zs, zs)))
    proj = inproj(x, mod, p["g_mix"], p["w_in"])
    to_tm = lambda a: jnp.transpose(a, (1, 0, 2))

    ax_tm = to_tm(proj[:, :, COL_AX:COL_AX + LRU_WIDTH])
    lru = lambda dr, prev: lru_scan(ax_tm, p["lru_conv_w"], p["lru_conv_b"], p["lru_wg"][dr], p["lru_bg"][dr],
                                    p["lru_lam"][dr], init[0][dr], prev, seg=seg, reverse=dr == 1)
    h_f, lru_f = lru(0, None)
    h_sum, lru_b = lru(1, h_f)

    dn = lambda dr, prev: dn_scan(proj, p["dn_conv_w"], p["dn_a_log"], p["dn_dt_bias"], init[1][dr], prev,
                                  seg=seg, reverse=dr == 1)
    o_f, dn_f = dn(0, None)
    o_sum, dn_b = dn(1, o_f)

    u_tm = to_tm(proj[:, :, COL_U:COL_U + S5_WIDTH])
    s5 = lambda dr, prev: s5_scan(u_tm, p["s5"][dr], init[2][dr][0], init[2][dr][1], prev, reverse=dr == 1)
    y_f, s5_fr, s5_fi = s5(0, None)
    y_sum, s5_br, s5_bi = s5(1, y_f)

    states = ((lru_f, lru_b), (dn_f, dn_b), ((s5_fr, s5_fi), (s5_br, s5_bi)))
    if not emit:
        return None, states
    return merge_stream(to_tm(h_sum), proj, o_sum, to_tm(y_sum), x, mod, p), states


def moe(h, logits, w1, b1, w2, b2):
    n, d = h.shape
    rt, counts = route(logits)
    idx = rt[:, :TOP_K].astype(jnp.int32)
    rank = rt[:, TOP_K:2 * TOP_K].astype(jnp.int32)
    sizes = counts[0, :N_EXPERTS].astype(jnp.int32)
    padded = ((sizes + MOE_BLOCK - 1) // MOE_BLOCK) * MOE_BLOCK
    ends_pad = jnp.cumsum(padded)
    starts_pad = ends_pad - padded
    dest = (starts_pad[idx] + rank).T.reshape(-1)
    n_blocks = -(-(n * TOP_K) // MOE_BLOCK) + N_EXPERTS
    block_start = jnp.arange(n_blocks, dtype=jnp.int32) * MOE_BLOCK
    block_e = jnp.minimum(jnp.sum(ends_pad[None, :] <= block_start[:, None], axis=1), N_EXPERTS - 1).astype(jnp.int32)
    n_used = (ends_pad[-1:] // MOE_BLOCK).astype(jnp.int32)
    row_tok = jnp.zeros((n_blocks * MOE_BLOCK,), jnp.int32).at[dest].set(
        jnp.tile(jnp.arange(n, dtype=jnp.int32), TOP_K))
    yb = expert_blocks(h[row_tok], block_e, n_used, w1, b1, w2, b2)
    return yb[dest].reshape(TOP_K, n, d), rt
```

```python
import functools
import math

import jax
import jax.numpy as jnp
from jax import lax
from jax.experimental import pallas as pl
from jax.experimental.pallas import tpu as pltpu

F32 = jnp.float32
BF16 = jnp.bfloat16

D_MODEL = 1024
EPS = 1e-6
CONV_W = 4
CONV_LEFT = CONV_W // 2
GRID_W = 64

LRU_WIDTH = 512
LRU_BLOCKS = 8
LRU_BLOCK = LRU_WIDTH // LRU_BLOCKS
LRU_C = 8.0

DN_HEAD_DIM = 128
DN_HEADS = 8
DN_WIDTH = DN_HEADS * DN_HEAD_DIM
DN_CHUNK = 64
DN_BATCH = 2

S5_WIDTH = 512
S5_GROUP = 16
S5_GROUPS = 32
S5_STATE = 64
S5_LANES = S5_GROUPS * S5_STATE
S5_SPLIT = 4

N_EXPERTS = 32
TOP_K = 4
D_EXPERT = 1024
SWIGLU_LIMIT = 7.0
SWIGLU_ALPHA = 1.702
MOE_BLOCK = 512
MOE_TILE = 512
MOE_PIECE = 16
MOE_LOCAL = MOE_TILE * TOP_K + MOE_PIECE * N_EXPERTS
MOE_CHUNK = 256

COL_AX, COL_AY, COL_Q, COL_K, COL_V, COL_Z = 0, 512, 1024, 2048, 3072, 4096
COL_BA, COL_U, COL_GATE, D_IN_PAD = 5120, 5632, 6144, 9216
D_IN = 8736

SUBLANES = 8
LANES = 128
VMEM_LIMIT = 52 * 1024 * 1024


def _cparams(sem):
    return pltpu.CompilerParams(dimension_semantics=sem, vmem_limit_bytes=VMEM_LIMIT)


def _sigmoid(x):
    return 1.0 / (1.0 + jnp.exp(-x))


def _softplus(x):
    return jnp.maximum(x, 0.0) + jnp.log1p(jnp.exp(-jnp.abs(x)))


def _silu(x):
    return x * _sigmoid(x)


def _gelu_tanh(x):
    return 0.5 * x * (1.0 + jnp.tanh(math.sqrt(2.0 / math.pi) * (x + 0.044715 * x * x * x)))


def _dot(a, b):
    return jnp.dot(a.astype(BF16), b.astype(BF16), preferred_element_type=F32)


def _ada_kernel(s_ref, w_ref, b_ref, o_ref):
    o_ref[...] = _dot(_silu(s_ref[...]), w_ref[...]) + b_ref[...]


def ada_modulation(s, w_ada, b_ada):
    n_layers, d, n = w_ada.shape
    r = s.shape[0]
    tn = 1536
    return pl.pallas_call(
        _ada_kernel,
        out_shape=jax.ShapeDtypeStruct((n_layers, r, n), F32),
        grid=(n_layers, n // tn),
        in_specs=[pl.BlockSpec((r, d), lambda l, j: (0, 0)),
                  pl.BlockSpec((None, d, tn), lambda l, j: (l, 0, j)),
                  pl.BlockSpec((None, 1, tn), lambda l, j: (l, 0, j))],
        out_specs=pl.BlockSpec((None, r, tn), lambda l, j: (l, 0, j)),
        compiler_params=_cparams(("arbitrary", "arbitrary")),
        name="ada_modulation",
    )(s, w_ada, b_ada.reshape(n_layers, 1, n))


def _norm_mod(x, g, shift, scale):
    y = x * lax.rsqrt(jnp.mean(x * x, axis=-1, keepdims=True) + EPS)
    return (y * g) * (1.0 + scale) + shift


def _inproj_kernel(x_ref, mod_ref, g_ref, w_ref, o_ref, h_ref, *, shift_row):
    @pl.when(pl.program_id(2) == 0)
    def _():
        mod = mod_ref[...]
        h = _norm_mod(x_ref[...], g_ref[...], mod[shift_row:shift_row + 1], mod[shift_row + 1:shift_row + 2])
        h_ref[...] = h.astype(BF16)

    o_ref[...] = jnp.dot(h_ref[...], w_ref[...], preferred_element_type=F32)


def inproj(x, mod, g, w_pad):
    b, t, d = x.shape
    n = w_pad.shape[1]
    tm = min(t, 1024)
    tn = 1536
    return pl.pallas_call(
        functools.partial(_inproj_kernel, shift_row=0),
        out_shape=jax.ShapeDtypeStruct((b, t, n), F32),
        grid=(b, t // tm, n // tn),
        in_specs=[pl.BlockSpec((None, tm, d), lambda i, r, j: (i, r, 0)),
                  pl.BlockSpec((None, 6, d), lambda i, r, j: (i, 0, 0)),
                  pl.BlockSpec((1, d), lambda i, r, j: (0, 0)),
                  pl.BlockSpec((d, tn), lambda i, r, j: (0, j))],
        out_specs=pl.BlockSpec((None, tm, tn), lambda i, r, j: (i, r, j)),
        scratch_shapes=[pltpu.VMEM((tm, d), BF16)],
        compiler_params=_cparams(("arbitrary", "arbitrary", "arbitrary")),
        name="inproj",
    )(x, mod, g, w_pad)


def _conv_time_major(x, w, seg):
    tc = x.shape[0]
    t_in_seg = lax.broadcasted_iota(jnp.int32, x.shape, 0) % seg
    acc = x * w[CONV_LEFT:CONV_LEFT + 1][None]
    for j in range(CONV_W):
        off = j - CONV_LEFT
        if off == 0:
            continue
        if off < 0:
            sh = jnp.concatenate([jnp.zeros((-off,) + x.shape[1:], x.dtype), x[:tc + off]], axis=0)
            ok = t_in_seg >= -off
        else:
            sh = jnp.concatenate([x[off:], jnp.zeros((off,) + x.shape[1:], x.dtype)], axis=0)
            ok = t_in_seg < seg - off
        acc = acc + jnp.where(ok, sh, 0.0) * w[j:j + 1][None]
    return acc


def _to_time_major(x_ref, xt_ref):
    bc, tc, w = x_ref.shape
    for j in range(w // LANES):
        for b in range(bc):
            xt_ref[j, pl.ds(b, tc, stride=bc), :] = x_ref[b, :, j * LANES:(j + 1) * LANES]


def _from_time_major(xt_ref, o_ref, prev_ref):
    bc, tc, w = o_ref.shape
    for j in range(w // LANES):
        for b in range(bc):
            v = xt_ref[j, pl.ds(b, tc, stride=bc), :]
            if prev_ref is not None:
                v = v + prev_ref[b, :, j * LANES:(j + 1) * LANES]
            o_ref[b, :, j * LANES:(j + 1) * LANES] = v


def _lru_kernel(x_ref, cw_ref, cb_ref, wg_ref, bg_ref, lam_ref, h0_ref, *rest, seg, reverse, add_prev):
    if add_prev:
        prev_ref, o_ref, hfin_ref, xt_ref, a_ref, b_ref, h_ref = rest
    else:
        prev_ref = None
        o_ref, hfin_ref, xt_ref, a_ref, b_ref, h_ref = rest
    bc, tc, w = x_ref.shape
    step = pl.program_id(1)

    @pl.when(step == 0)
    def _():
        h_ref[...] = h0_ref[...]

    _to_time_major(x_ref, xt_ref)
    x = jnp.concatenate([xt_ref[j] for j in range(w // LANES)], axis=-1).reshape(tc, bc, w)
    xc = _conv_time_major(x, cw_ref[...], seg) + cb_ref[...][None]
    gates = _dot(xc.reshape(tc * bc, w), wg_ref[...]) + bg_ref[...]
    r = _sigmoid(gates[:, :w]).reshape(tc, bc, w)
    i = _sigmoid(gates[:, w:]).reshape(tc, bc, w)
    log_a = (-LRU_C * _softplus(-lam_ref[...]))[None] * r
    a_ref[...] = jnp.exp(log_a)
    b_ref[...] = jnp.sqrt(1.0 - jnp.exp(2.0 * log_a)) * (i * xc)

    def body(s, h):
        t = tc - 1 - s if reverse else s
        h = a_ref[t] * h + b_ref[t]
        rows = pl.ds(pl.multiple_of(t * bc, bc), bc)
        for j in range(w // LANES):
            xt_ref[j, rows, :] = h[:, j * LANES:(j + 1) * LANES]
        return h

    h = lax.fori_loop(0, tc, body, h_ref[...], unroll=8)
    h_ref[...] = h
    hfin_ref[...] = h
    _from_time_major(xt_ref, o_ref, prev_ref)


def lru_scan(proj, conv_w, conv_b, wg, bg, lam, h0, prev, *, seg, reverse):
    b, t, _ = proj.shape
    w = LRU_WIDTH
    tc = min(t, 256)
    bc = SUBLANES
    assert tc % seg == 0 and t % tc == 0 and b % bc == 0
    nt = t // tc
    tix = (lambda s: nt - 1 - s) if reverse else (lambda s: s)
    blk = lambda cb: pl.BlockSpec((bc, tc, w), lambda i, s: (i, tix(s), cb))
    const = lambda shape: pl.BlockSpec(shape, lambda i, s: (0,) * len(shape))
    in_specs = [blk(COL_AX // w), const((CONV_W, w)), const((1, w)), const((w, 2 * w)), const((1, 2 * w)),
                const((1, w)), pl.BlockSpec((bc, w), lambda i, s: (i, 0))]
    args = [proj, conv_w, conv_b, wg, bg, lam, h0]
    if prev is not None:
        in_specs.append(blk(0))
        args.append(prev)
    return pl.pallas_call(
        functools.partial(_lru_kernel, seg=seg, reverse=reverse, add_prev=prev is not None),
        out_shape=(jax.ShapeDtypeStruct((b, t, w), F32), jax.ShapeDtypeStruct((b, w), F32)),
        grid=(b // bc, nt),
        in_specs=in_specs,
        out_specs=(blk(0), pl.BlockSpec((bc, w), lambda i, s: (i, 0))),
        scratch_shapes=[pltpu.VMEM((w // LANES, tc * bc, LANES), F32), pltpu.VMEM((tc, bc, w), F32),
                        pltpu.VMEM((tc, bc, w), F32), pltpu.VMEM((bc, w), F32)],
        compiler_params=_cparams(("arbitrary", "arbitrary")),
        name="lru_scan_bwd" if reverse else "lru_scan_fwd",
    )(*args)


def _lru_gate_dense(w_gate):
    eye = jnp.eye(LRU_BLOCKS, dtype=w_gate.dtype)
    dense = jnp.einsum("gnij,nm->gnimj", w_gate, eye).reshape(2, LRU_WIDTH, LRU_WIDTH)
    return jnp.concatenate([dense[0], dense[1]], axis=1).astype(BF16)


def _s5_kernel(u_ref, bre_ref, bim_ref, cre_ref, cim_ref, ar_ref, ai_ref, h0r_ref, h0i_ref, *rest,
               reverse, add_prev):
    if add_prev:
        prev_ref, o_ref, fr_ref, fi_ref, ut_ref, xr_ref, xi_ref, hr_ref, hi_ref = rest
    else:
        prev_ref = None
        o_ref, fr_ref, fi_ref, ut_ref, xr_ref, xi_ref, hr_ref, hi_ref = rest
    bc, tc, w = u_ref.shape
    blk_in = w // S5_SPLIT
    blk_st = S5_LANES // S5_SPLIT
    assert blk_in == LANES

    @pl.when(pl.program_id(1) == 0)
    def _():
        hr_ref[...] = h0r_ref[...]
        hi_ref[...] = h0i_ref[...]

    _to_time_major(u_ref, ut_ref)
    for j in range(S5_SPLIT):
        uj = ut_ref[j].astype(BF16)
        xr_ref[:, :, j * blk_st:(j + 1) * blk_st] = jnp.dot(
            uj, bre_ref[j], preferred_element_type=F32).reshape(tc, bc, blk_st)
        xi_ref[:, :, j * blk_st:(j + 1) * blk_st] = jnp.dot(
            uj, bim_ref[j], preferred_element_type=F32).reshape(tc, bc, blk_st)

    ar = jnp.broadcast_to(ar_ref[...], (bc, S5_LANES))
    ai = jnp.broadcast_to(ai_ref[...], (bc, S5_LANES))

    def body(s, carry):
        hr, hi = carry
        t = tc - 1 - s if reverse else s
        nr = ar * hr - ai * hi + xr_ref[t]
        ni = ar * hi + ai * hr + xi_ref[t]
        xr_ref[t] = nr
        xi_ref[t] = ni
        return nr, ni

    hr, hi = lax.fori_loop(0, tc, body, (hr_ref[...], hi_ref[...]))
    hr_ref[...] = hr
    hi_ref[...] = hi
    fr_ref[...] = hr
    fi_ref[...] = hi

    for j in range(S5_SPLIT):
        hrj = xr_ref[:, :, j * blk_st:(j + 1) * blk_st].reshape(tc * bc, blk_st).astype(BF16)
        hij = xi_ref[:, :, j * blk_st:(j + 1) * blk_st].reshape(tc * bc, blk_st).astype(BF16)
        ut_ref[j] = (jnp.dot(hrj, cre_ref[j], preferred_element_type=F32)
                     - jnp.dot(hij, cim_ref[j], preferred_element_type=F32))
    _from_time_major(ut_ref, o_ref, prev_ref)


def s5_scan(proj, prm, h0r, h0i, prev, *, reverse):
    b, t, _ = proj.shape
    w = S5_WIDTH
    tc = min(t, 64)
    bc = SUBLANES
    assert t % tc == 0 and b % bc == 0
    nt = t // tc
    tix = (lambda s: nt - 1 - s) if reverse else (lambda s: s)
    blk = lambda cb: pl.BlockSpec((bc, tc, w), lambda i, s: (i, tix(s), cb))
    const = lambda shape: pl.BlockSpec(shape, lambda i, s: (0,) * len(shape))
    st = pl.BlockSpec((bc, S5_LANES), lambda i, s: (i, 0))
    bre, bim, cre, cim, ar, ai = prm
    in_specs = [blk(COL_U // w), const(bre.shape), const(bim.shape), const(cre.shape), const(cim.shape),
                const(ar.shape), const(ai.shape), st, st]
    args = [proj, bre, bim, cre, cim, ar, ai, h0r, h0i]
    if prev is not None:
        in_specs.append(blk(0))
        args.append(prev)
    st_shape = jax.ShapeDtypeStruct((b, S5_LANES), F32)
    return pl.pallas_call(
        functools.partial(_s5_kernel, reverse=reverse, add_prev=prev is not None),
        out_shape=(jax.ShapeDtypeStruct((b, t, w), F32), st_shape, st_shape),
        grid=(b // bc, nt),
        in_specs=in_specs,
        out_specs=(blk(0), st, st),
        scratch_shapes=[pltpu.VMEM((w // LANES, tc * bc, LANES), F32),
                        pltpu.VMEM((tc, bc, S5_LANES), F32), pltpu.VMEM((tc, bc, S5_LANES), F32),
                        pltpu.VMEM((bc, S5_LANES), F32), pltpu.VMEM((bc, S5_LANES), F32)],
        compiler_params=_cparams(("arbitrary", "arbitrary")),
        name="s5_scan_bwd" if reverse else "s5_scan_fwd",
    )(*args)


def _s5_params(lam_re, lam_im, log_dt, b_re, b_im, c_re, c_im):
    dt = jnp.exp(log_dt)[:, None]
    mag = jnp.exp(lam_re * dt)
    ar, ai = mag * jnp.cos(lam_im * dt), mag * jnp.sin(lam_im * dt)
    den = lam_re * lam_re + lam_im * lam_im
    fr = ((ar - 1.0) * lam_re + ai * lam_im) / den
    fi = (ai * lam_re - (ar - 1.0) * lam_im) / den
    bb_re = fr[..., None] * b_re - fi[..., None] * b_im
    bb_im = fr[..., None] * b_im + fi[..., None] * b_re
    gl = S5_GROUPS // S5_SPLIT
    eye = jnp.eye(gl, dtype=F32)

    def pack_in(bb):
        bb = jnp.transpose(bb, (0, 2, 1)).reshape(S5_SPLIT, gl, S5_GROUP, S5_STATE)
        return jnp.einsum("jghp,gm->jghmp", bb, eye).reshape(S5_SPLIT, gl * S5_GROUP, gl * S5_STATE).astype(BF16)

    def pack_out(c):
        c = jnp.transpose(c, (0, 2, 1)).reshape(S5_SPLIT, gl, S5_STATE, S5_GROUP)
        return jnp.einsum("jgph,gm->jgpmh", c, eye).reshape(S5_SPLIT, gl * S5_STATE, gl * S5_GROUP).astype(BF16)

    return (pack_in(bb_re), pack_in(bb_im), pack_out(c_re), pack_out(c_im),
            ar.reshape(1, S5_LANES), ai.reshape(1, S5_LANES))


def _conv_rows(x, w, seg):
    tc = x.shape[0]
    t_in_seg = lax.broadcasted_iota(jnp.int32, x.shape, 0) % seg
    acc = x * w[CONV_LEFT:CONV_LEFT + 1]
    for j in range(CONV_W):
        off = j - CONV_LEFT
        if off == 0:
            continue
        sh = pltpu.roll(x, (-off) % tc, 0)
        ok = (t_in_seg >= -off) if off < 0 else (t_in_seg < seg - off)
        acc = acc + jnp.where(ok, sh, 0.0) * w[j:j + 1]
    return acc


def _l2norm(x):
    return x * lax.rsqrt(jnp.sum(x * x, axis=-1, keepdims=True) + EPS)


def _dot_nt(a, b):
    return lax.dot_general(a.astype(BF16), b.astype(BF16), (((1,), (1,)), ((), ())), preferred_element_type=F32)


def _dot_tn(a, b):
    return lax.dot_general(a.astype(BF16), b.astype(BF16), (((0,), (0,)), ((), ())), preferred_element_type=F32)


def _split3_dot(m, x):
    x1 = x.astype(BF16)
    r1 = x - x1.astype(F32)
    x2 = r1.astype(BF16)
    x3 = (r1 - x2.astype(F32)).astype(BF16)
    mb = m.astype(BF16)
    dot = lambda v: jnp.dot(mb, v, preferred_element_type=F32)
    return dot(x1) + dot(x2) + dot(x3)


def _dn_kernel(q_ref, k_ref, v_ref, ba_ref, cwq_ref, cwk_ref, cwv_ref, alog_ref, dtb_ref, s0_ref, *rest,
               seg, reverse, add_prev, dirn):
    if add_prev:
        prev_ref, o_ref, sfin_ref, qc_ref, kc_ref, vc_ref, beta_ref, g_ref, s_ref = rest
    else:
        o_ref, sfin_ref, qc_ref, kc_ref, vc_ref, beta_ref, g_ref, s_ref = rest
    nb, tc, wq = q_ref.shape
    n_chunks = tc // DN_CHUNK
    step = pl.program_id(1)

    @pl.when(step == 0)
    def _():
        s_ref[...] = s0_ref[...]

    rows_all = lambda ref: ref[...].reshape(nb * tc, ref.shape[-1])
    qc_ref[...] = _silu(_conv_rows(rows_all(q_ref), cwq_ref[...], seg))
    kc_ref[...] = _silu(_conv_rows(rows_all(k_ref), cwk_ref[...], seg))
    vc_ref[...] = _silu(_conv_rows(rows_all(v_ref), cwv_ref[...], seg))
    ba = rows_all(ba_ref)
    beta_ref[...] = _sigmoid(ba)
    g_ref[...] = -jnp.exp(alog_ref[...]) * _softplus(ba + dtb_ref[...])

    ii = lax.broadcasted_iota(jnp.int32, (DN_CHUNK, DN_CHUNK), 0)
    jj = lax.broadcasted_iota(jnp.int32, (DN_CHUNK, DN_CHUNK), 1)
    incl = (ii <= jj) if reverse else (ii >= jj)
    strict = (ii < jj) if reverse else (ii > jj)
    eye = (ii == jj).astype(F32)
    tri = incl.astype(F32)
    last = 0 if reverse else DN_CHUNK - 1
    neg = jnp.float32(-1e30)

    def chunk(ci, carry):
        cidx = n_chunks - 1 - ci if reverse else ci
        r0 = pl.multiple_of(cidx * DN_CHUNK, DN_CHUNK)
        rows = pl.ds(r0, DN_CHUNK)
        each = lambda f, *cols: [f(*args) for args in zip(*cols)]
        srows = [pl.ds(pl.multiple_of(bb * tc + r0, DN_CHUNK), DN_CHUNK) for bb in range(nb)]
        g_cum = [_split3_dot(tri, g_ref[sr, :]) for sr in srows]
        g_t = [g.T for g in g_cum]
        e_g = [jnp.exp(g) for g in g_cum]
        e_rest = [jnp.exp(g[last:last + 1] - g) for g in g_cum]
        beta_all = [beta_ref[sr, :] for sr in srows]
        pairs = [(bb, h) for bb in range(nb) for h in range(DN_HEADS)]
        lanes = [slice(h * DN_HEAD_DIM, (h + 1) * DN_HEAD_DIM) for _, h in pairs]
        cb = [dirn * DN_HEADS + h for _, h in pairs]
        cg = [2 * DN_HEADS + c for c in cb]
        q = [_l2norm(qc_ref[srows[bb], ln]) * (DN_HEAD_DIM ** -0.5) for (bb, _), ln in zip(pairs, lanes)]
        k = [_l2norm(kc_ref[srows[bb], ln]) for (bb, _), ln in zip(pairs, lanes)]
        v = [vc_ref[srows[bb], ln] for (bb, _), ln in zip(pairs, lanes)]
        beta = [beta_all[bb][:, c:c + 1] for (bb, _), c in zip(pairs, cb)]
        decay = [jnp.exp(jnp.where(incl, g_cum[bb][:, c:c + 1] - g_t[bb][c:c + 1, :], neg))
                 for (bb, _), c in zip(pairs, cg)]
        nc = DN_CHUNK
        kb = each(lambda a, b: a * b, k, beta)
        kq = each(lambda a, b, c: _dot_nt(jnp.concatenate([a, b], axis=0), c), kb, q, k)
        a_low = each(lambda r, dc: jnp.where(strict, r[:nc] * dc, 0.0), kq, decay)
        attn = each(lambda r, dc: jnp.where(incl, r[nc:] * dc, 0.0), kq, decay)
        x = [eye - a for a in a_low]
        p = [_dot(a, a) for a in a_low]
        for _ in range(4):
            xp = each(lambda a, b: _dot(jnp.concatenate([a, b], axis=0), b), x, p)
            x = each(lambda a, r: a + r[:nc], x, xp)
            p = [r[nc:] for r in xp]
        x = each(lambda a, b: a + _dot(a, b), x, p)
        e_col = [e_g[bb][:, c:c + 1] for (bb, _), c in zip(pairs, cg)]
        uw = each(lambda a, vv, bt, kk, ec: _dot(a, jnp.concatenate([vv * bt, kk * ec], axis=1)),
                  x, v, beta, kb, e_col)
        q_dec = each(lambda a, b: a * b, q, e_col)
        k_dec = [kk * e_rest[bb][:, c:c + 1] for kk, (bb, _), c in zip(k, pairs, cg)]
        e_tot = [e_g[bb][last:last + 1, c:c + 1] for (bb, _), c in zip(pairs, cg)]
        s = [s_ref[bb, h] for bb, h in pairs]
        ws = each(lambda r, qd, ss: _dot(jnp.concatenate([r[:, DN_HEAD_DIM:], qd], axis=0), ss), uw, q_dec, s)
        v_new = each(lambda r, m: r[:, :DN_HEAD_DIM] - m[:nc], uw, ws)
        o = each(lambda m, at, vn: m[nc:] + _dot(at, vn), ws, attn, v_new)
        s_new = each(lambda ss, et, kd, vn: ss * et + _dot_tn(kd, vn), s, e_tot, k_dec, v_new)
        if add_prev:
            o = [oo + prev_ref[bb, rows, ln] for oo, (bb, _), ln in zip(o, pairs, lanes)]
        for (bb, h), ss in zip(pairs, s_new):
            s_ref[bb, h] = ss
        for (bb, _), ln, oo in zip(pairs, lanes, o):
            o_ref[bb, rows, ln] = oo
        return carry

    lax.fori_loop(0, n_chunks, chunk, 0)

    @pl.when(step == pl.num_programs(1) - 1)
    def _():
        sfin_ref[...] = s_ref[...]


def dn_scan(proj, conv_w, a_log, dt_bias, s0, prev, *, seg, reverse):
    b, t, _ = proj.shape
    tc = min(t, 256)
    nb = DN_BATCH
    assert tc % seg == 0 and t % tc == 0 and b % nb == 0
    nt = t // tc
    dirn = 1 if reverse else 0
    tix = (lambda s: nt - 1 - s) if reverse else (lambda s: s)
    wq = DN_WIDTH
    col = lambda cb: pl.BlockSpec((nb, tc, wq), lambda i, s: (i, tix(s), cb))
    cw = lambda cb: pl.BlockSpec((CONV_W, wq), lambda i, s: (0, cb))
    vec = pl.BlockSpec((1, 128), lambda i, s: (0, 0))
    st = pl.BlockSpec((nb, DN_HEADS, DN_HEAD_DIM, DN_HEAD_DIM), lambda i, s: (i, 0, 0, 0))
    oblk = pl.BlockSpec((nb, tc, wq), lambda i, s: (i, tix(s), 0))
    in_specs = [col(COL_Q // wq), col(COL_K // wq), col(COL_V // wq),
                pl.BlockSpec((nb, tc, 128), lambda i, s: (i, tix(s), COL_BA // 128)),
                cw(0), cw(1), cw(2), vec, vec, st]
    args = [proj, proj, proj, proj, conv_w, conv_w, conv_w, a_log, dt_bias, s0]
    if prev is not None:
        in_specs.append(oblk)
        args.append(prev)
    return pl.pallas_call(
        functools.partial(_dn_kernel, seg=seg, reverse=reverse, add_prev=prev is not None, dirn=dirn),
        out_shape=(jax.ShapeDtypeStruct((b, t, wq), F32),
                   jax.ShapeDtypeStruct((b, DN_HEADS, DN_HEAD_DIM, DN_HEAD_DIM), F32)),
        grid=(b // nb, nt),
        in_specs=in_specs,
        out_specs=(oblk, st),
        scratch_shapes=[pltpu.VMEM((nb * tc, wq), F32), pltpu.VMEM((nb * tc, wq), F32),
                        pltpu.VMEM((nb * tc, wq), F32), pltpu.VMEM((nb * tc, 128), F32),
                        pltpu.VMEM((nb * tc, 128), F32),
                        pltpu.VMEM((nb, DN_HEADS, DN_HEAD_DIM, DN_HEAD_DIM), F32)],
        compiler_params=_cparams(("arbitrary", "arbitrary")),
        name="dn_scan_bwd" if reverse else "dn_scan_fwd",
    )(*args)


def _dn_lane_vec(p):
    return jnp.zeros((1, 128), F32).at[0, 2 * DN_HEADS:4 * DN_HEADS].set(p.reshape(-1))


def _merge_kernel(hl_ref, ay_ref, od_ref, z_ref, ys_ref, u_ref, gt_ref, x_ref, mod_ref, ng_ref, sd_ref,
                  wglu_ref, bglu_ref, wa_ref, wb_ref, wc_ref, wo_ref, gf_ref, wr1_ref, wr2_ref, br_ref,
                  xo_ref, h_ref, lg_ref):
    d = x_ref.shape[-1]
    mod = mod_ref[...]
    y_a = hl_ref[...] * _gelu_tanh(ay_ref[...])
    acc = _sigmoid(gt_ref[:, 0:d]) * _dot(y_a, wa_ref[...])

    ng = ng_ref[...]
    heads = []
    for h in range(DN_HEADS):
        lanes = slice(h * DN_HEAD_DIM, (h + 1) * DN_HEAD_DIM)
        o = od_ref[:, lanes]
        o = o * lax.rsqrt(jnp.mean(o * o, axis=-1, keepdims=True) + EPS) * ng
        heads.append((o * _silu(z_ref[:, lanes])).astype(BF16))
    y_b = jnp.concatenate(heads, axis=-1)
    acc = acc + _sigmoid(gt_ref[:, d:2 * d]) * jnp.dot(y_b, wb_ref[...], preferred_element_type=F32)

    y_c = _gelu_tanh(ys_ref[...] + sd_ref[...] * u_ref[...])
    y_c = y_c * _sigmoid(_dot(y_c, wglu_ref[...]) + bglu_ref[...])
    acc = acc + _sigmoid(gt_ref[:, 2 * d:3 * d]) * _dot(y_c, wc_ref[...])

    x_new = x_ref[...] + mod[2:3] * _dot(acc, wo_ref[...])
    xo_ref[...] = x_new
    h = _norm_mod(x_new, gf_ref[...], mod[3:4], mod[4:5])
    h1 = h.astype(BF16)
    h2 = (h - h1.astype(F32)).astype(BF16)
    h_ref[...] = h1
    lg_ref[...] = (jnp.dot(h1, wr1_ref[...], preferred_element_type=F32)
                   + jnp.dot(h1, wr2_ref[...], preferred_element_type=F32)
                   + jnp.dot(h2, wr1_ref[...], preferred_element_type=F32)) + br_ref[...]


def merge_stream(h_lru, proj, o_dn, y_s5, x, mod, p):
    b, t, d = x.shape
    tm = min(t, 256)
    row = lambda w, cb: pl.BlockSpec((None, tm, w), lambda i, r: (i, r, cb))
    const = lambda a: pl.BlockSpec(a.shape, lambda i, r: (0,) * a.ndim)
    consts = [p["dn_norm_g"], p["s5_d"], p["s5_w_glu"], p["s5_b_glu"], p["w_br_a"], p["w_br_b"], p["w_br_c"],
              p["w_out"], p["g_ffn"], p["w_r1"], p["w_r2"], p["b_router"]]
    return pl.pallas_call(
        _merge_kernel,
        out_shape=(jax.ShapeDtypeStruct((b, t, d), F32), jax.ShapeDtypeStruct((b, t, d), BF16),
                   jax.ShapeDtypeStruct((b, t, 128), F32)),
        grid=(b, t // tm),
        in_specs=[row(LRU_WIDTH, 0), row(LRU_WIDTH, COL_AY // LRU_WIDTH), row(DN_WIDTH, 0),
                  row(DN_WIDTH, COL_Z // DN_WIDTH), row(S5_WIDTH, 0), row(S5_WIDTH, COL_U // S5_WIDTH),
                  row(3 * d, COL_GATE // (3 * d)), row(d, 0),
                  pl.BlockSpec((None, 6, d), lambda i, r: (i, 0, 0))] + [const(a) for a in consts],
        out_specs=(row(d, 0), row(d, 0), row(128, 0)),
        compiler_params=_cparams(("arbitrary", "arbitrary")),
        name="merge_stream",
    )(h_lru, proj, o_dn, proj, y_s5, proj, proj, x, mod, *consts)


def _route_kernel(lg_ref, o_ref, ot_ref, cnt_ref):
    tm = lg_ref.shape[0]
    logits = lg_ref[...]
    lane = lax.broadcasted_iota(jnp.int32, logits.shape, 1)
    neg = jnp.float32(-jnp.inf)
    vals, idxs = [], []
    sel = jnp.zeros(logits.shape, F32)
    for _ in range(TOP_K):
        m = jnp.max(logits, axis=-1, keepdims=True)
        idx = jnp.min(jnp.where(logits == m, lane, 128), axis=-1, keepdims=True)
        hit = lane == idx
        logits = jnp.where(hit, neg, logits)
        sel = jnp.where(hit, 1.0, sel)
        vals.append(m)
        idxs.append(idx)
    exps = [jnp.exp(v - vals[0]) for v in vals]
    inv = 1.0 / (exps[0] + exps[1] + exps[2] + exps[3])

    ii = lax.broadcasted_iota(jnp.int32, (tm, tm), 0)
    jj = lax.broadcasted_iota(jnp.int32, (tm, tm), 1)
    before = jnp.dot((ii > jj).astype(BF16), sel.astype(BF16), preferred_element_type=F32)
    cnt = jnp.sum(sel, axis=0, keepdims=True)
    pieces = jnp.floor((cnt + (MOE_PIECE - 1)) * (1.0 / MOE_PIECE))
    ei = lax.broadcasted_iota(jnp.int32, (128, 128), 0)
    ej = lax.broadcasted_iota(jnp.int32, (128, 128), 1)
    start = MOE_PIECE * jnp.dot(jnp.broadcast_to(pieces, (SUBLANES, 128)).astype(BF16), (ei < ej).astype(BF16),
                                preferred_element_type=F32)[0:1]
    pos = before + start
    out = jnp.zeros(logits.shape, F32)
    for k in range(TOP_K):
        row = jnp.sum(jnp.where(lane == idxs[k], pos, 0.0), axis=-1, keepdims=True)
        out = jnp.where(lane == k, idxs[k].astype(F32), out)
        out = jnp.where(lane == TOP_K + k, row, out)
        out = jnp.where(lane == 2 * TOP_K + k, exps[k] * inv, out)
    o_ref[...] = out
    ot_ref[...] = out.T[:ot_ref.shape[0]]
    cnt_ref[...] = jnp.broadcast_to(cnt, cnt_ref.shape)


def route(logits):
    n = logits.shape[0]
    tm = MOE_TILE
    nt = n // tm
    return pl.pallas_call(
        _route_kernel,
        out_shape=(jax.ShapeDtypeStruct((n, 128), F32), jax.ShapeDtypeStruct((nt, 16, tm), F32),
                   jax.ShapeDtypeStruct((nt, SUBLANES, 128), F32)),
        grid=(nt,),
        in_specs=[pl.BlockSpec((tm, 128), lambda i: (i, 0))],
        out_specs=(pl.BlockSpec((tm, 128), lambda i: (i, 0)), pl.BlockSpec((None, 16, tm), lambda i: (i, 0, 0)),
                   pl.BlockSpec((None, SUBLANES, 128), lambda i: (i, 0, 0))),
        compiler_params=_cparams(("arbitrary",)),
        name="route",
    )(logits)


def _piece_copies(pc_ref, loc_ref, goff_ref, base, make_copy, wait):
    for e in range(N_EXPERTS):
        n = pc_ref[base + e]
        loc = loc_ref[base + e]
        goff = goff_ref[base + e]
        big = 16 * MOE_PIECE

        def body(j, carry, loc=loc, goff=goff):
            cp = make_copy(pl.multiple_of(loc + j * big, MOE_PIECE), pl.multiple_of(goff + j * big, MOE_PIECE), big)
            cp.wait() if wait else cp.start()
            return carry

        n_big = lax.shift_right_logical(n, 4)
        lax.fori_loop(0, n_big, body, 0)
        done = n_big * big
        for bit in (8, 4, 2, 1):
            rows = bit * MOE_PIECE
            off = done

            @pl.when((n & bit) != 0)
            def _(off=off, rows=rows, loc=loc, goff=goff):
                cp = make_copy(pl.multiple_of(loc + off, MOE_PIECE), pl.multiple_of(goff + off, MOE_PIECE), rows)
                cp.wait() if wait else cp.start()

            done = done + jnp.where((n & bit) != 0, rows, 0)


def _local_onehot_rows(rtt, c):
    rows = (lax.broadcasted_iota(jnp.int32, (MOE_CHUNK, rtt.shape[1]), 0) + c * MOE_CHUNK).astype(F32)
    return [rows == rtt[TOP_K + k:TOP_K + k + 1] for k in range(TOP_K)]


def _dispatch_kernel(pc_ref, loc_ref, goff_ref, h_ref, rtt_ref, xs_ref, xloc_ref, zero_ref, sem):
    tile = pl.program_id(0)
    n_tiles = pl.num_programs(0)
    rtt = rtt_ref[...]
    h = h_ref[...]
    for c in range(MOE_LOCAL // MOE_CHUNK):
        hit = _local_onehot_rows(rtt, c)
        p = jnp.where(hit[0], 1.0, jnp.where(hit[1], 1.0, jnp.where(hit[2], 1.0, jnp.where(hit[3], 1.0, 0.0))))
        xloc_ref[c * MOE_CHUNK:(c + 1) * MOE_CHUNK, :] = jnp.dot(
            p.astype(BF16), h, preferred_element_type=F32).astype(BF16)
    make = lambda src, dst, rows: pltpu.make_async_copy(
        xloc_ref.at[pl.ds(src, rows)], xs_ref.at[pl.ds(dst, rows)], sem)
    _piece_copies(pc_ref, loc_ref, goff_ref, tile * N_EXPERTS, make, wait=False)
    _piece_copies(pc_ref, loc_ref, goff_ref, tile * N_EXPERTS, make, wait=True)

    @pl.when(tile == n_tiles - 1)
    def _():
        zero_ref[...] = jnp.zeros_like(zero_ref)
        fill = lambda src, dst, rows: pltpu.make_async_copy(
            zero_ref.at[pl.ds(0, rows)], xs_ref.at[pl.ds(dst, rows)], sem)
        _piece_copies(pc_ref, loc_ref, goff_ref, n_tiles * N_EXPERTS, fill, wait=False)
        _piece_copies(pc_ref, loc_ref, goff_ref, n_tiles * N_EXPERTS, fill, wait=True)


def dispatch(h, rtt, pc, loc, goff, n_rows):
    n, d = h.shape
    nt = n // MOE_TILE
    return pl.pallas_call(
        _dispatch_kernel,
        out_shape=jax.ShapeDtypeStruct((n_rows, d), BF16),
        grid_spec=pltpu.PrefetchScalarGridSpec(
            num_scalar_prefetch=3,
            grid=(nt,),
            in_specs=[pl.BlockSpec((MOE_TILE, d), lambda i, *_: (i, 0)),
                      pl.BlockSpec((None, 16, MOE_TILE), lambda i, *_: (i, 0, 0))],
            out_specs=pl.BlockSpec(memory_space=pl.ANY),
            scratch_shapes=[pltpu.VMEM((MOE_LOCAL, d), BF16), pltpu.VMEM((MOE_BLOCK, d), BF16),
                            pltpu.SemaphoreType.DMA]),
        compiler_params=_cparams(("arbitrary",)),
        name="dispatch",
    )(pc, loc, goff, h, rtt)


def _expert_kernel(be_ref, nu_ref, x_ref, w1_ref, b1_ref, w2_ref, b2_ref, o_ref, w1b_ref, w2b_ref):
    i = pl.program_id(0)

    @pl.when(i < nu_ref[0])
    def _():
        @pl.when(jnp.logical_or(i == 0, be_ref[i] != be_ref[jnp.maximum(i - 1, 0)]))
        def _():
            w1b_ref[...] = w1_ref[...].astype(BF16)
            w2b_ref[...] = w2_ref[...].astype(BF16)

        de = w2_ref.shape[0]
        gu = jnp.dot(x_ref[...], w1b_ref[...], preferred_element_type=F32) + b1_ref[...]
        glu = jnp.minimum(gu[:, :de], SWIGLU_LIMIT)
        lin = jnp.clip(gu[:, de:], -SWIGLU_LIMIT, SWIGLU_LIMIT)
        act = glu * _sigmoid(SWIGLU_ALPHA * glu) * (lin + 1.0)
        o_ref[...] = jnp.dot(act.astype(BF16), w2b_ref[...], preferred_element_type=F32) + b2_ref[...]

    @pl.when(i >= nu_ref[0])
    def _():
        o_ref[...] = jnp.zeros_like(o_ref)


def expert_blocks(xs, block_e, n_used, w1, b1, w2, b2):
    rows, d = xs.shape
    n_blocks = rows // MOE_BLOCK
    de = w2.shape[1]
    blk = lambda i, be, nu: (jnp.minimum(i, nu[0] - 1), 0)
    ex = lambda i, be, nu: (be[jnp.minimum(i, nu[0] - 1)], 0, 0)
    return pl.pallas_call(
        _expert_kernel,
        out_shape=jax.ShapeDtypeStruct((rows, d), F32),
        grid_spec=pltpu.PrefetchScalarGridSpec(
            num_scalar_prefetch=2,
            grid=(n_blocks,),
            in_specs=[pl.BlockSpec((MOE_BLOCK, d), blk),
                      pl.BlockSpec((None, d, 2 * de), ex),
                      pl.BlockSpec((None, 1, 2 * de), ex),
                      pl.BlockSpec((None, de, d), ex),
                      pl.BlockSpec((None, 1, d), ex)],
            out_specs=pl.BlockSpec((MOE_BLOCK, d), lambda i, be, nu: (i, 0)),
            scratch_shapes=[pltpu.VMEM((d, 2 * de), BF16), pltpu.VMEM((de, d), BF16)]),
        compiler_params=_cparams(("arbitrary",)),
        name="expert_blocks",
    )(block_e, n_used, xs, w1, b1, w2, b2)


def _combine_kernel(pc_ref, loc_ref, goff_ref, yb_ref, rt_ref, rtt_ref, x_ref, gate_ref, gfin_ref, o_ref,
                    yloc_ref, sem, *, tile0, final_norm):
    step = pl.program_id(0)
    tile = step + tile0

    @pl.when(step == 0)
    def _():
        yloc_ref[...] = jnp.zeros_like(yloc_ref)

    make = lambda loc, glob, rows: pltpu.make_async_copy(
        yb_ref.at[pl.ds(glob, rows)], yloc_ref.at[pl.ds(loc, rows)], sem)
    _piece_copies(pc_ref, loc_ref, goff_ref, tile * N_EXPERTS, make, wait=False)
    _piece_copies(pc_ref, loc_ref, goff_ref, tile * N_EXPERTS, make, wait=True)

    rt = rt_ref[...]
    rtt = rtt_ref[...]
    tm = rt.shape[0]
    acc = jnp.zeros(x_ref.shape, F32)
    for c in range(MOE_LOCAL // MOE_CHUNK):
        hit = _local_onehot_rows(rtt, c)
        wgt = [rtt[2 * TOP_K + k:2 * TOP_K + k + 1] for k in range(TOP_K)]
        pw = jnp.where(hit[0], wgt[0], jnp.where(hit[1], wgt[1], jnp.where(hit[2], wgt[2],
                                                                           jnp.where(hit[3], wgt[3], 0.0))))
        ys = yloc_ref[c * MOE_CHUNK:(c + 1) * MOE_CHUNK, :] * jnp.sum(pw, axis=-1, keepdims=True)
        hi = ys.astype(BF16)
        lo = (ys - hi.astype(F32)).astype(BF16)
        cols = (lax.broadcasted_iota(jnp.int32, (tm, MOE_CHUNK), 1) + c * MOE_CHUNK).astype(F32)
        own = [cols == rt[:, TOP_K + k:TOP_K + k + 1] for k in range(TOP_K)]
        pt = jnp.where(own[0], 1.0, jnp.where(own[1], 1.0, jnp.where(own[2], 1.0,
                                                                     jnp.where(own[3], 1.0, 0.0)))).astype(BF16)
        acc = acc + jnp.dot(pt, hi, preferred_element_type=F32) + jnp.dot(pt, lo, preferred_element_type=F32)

    nseg = gate_ref.shape[0]
    seg = tm // nseg
    for r in range(nseg):
        rows = slice(r * seg, (r + 1) * seg)
        x_new = x_ref[rows, :] + gate_ref[r] * acc[rows]
        if final_norm:
            x_new = x_new * lax.rsqrt(jnp.mean(x_new * x_new, axis=-1, keepdims=True) + EPS) * gfin_ref[...]
        o_ref[rows, :] = x_new


def combine(yb, rt, rtt, pc, loc, goff, row0, x, mod, g_final, *, final_norm):
    b, t, d = x.shape
    tm = MOE_TILE
    assert row0 % tm == 0 and (b * t) % tm == 0 and (t % tm == 0 or tm % t == 0)
    tile0 = row0 // tm
    nseg = max(tm // t, 1)
    tiles_per_b = max(t // tm, 1)
    gate = mod[:, 5:6, :]
    out = pl.pallas_call(
        functools.partial(_combine_kernel, tile0=tile0, final_norm=final_norm),
        out_shape=jax.ShapeDtypeStruct((b * t, d), F32),
        grid_spec=pltpu.PrefetchScalarGridSpec(
            num_scalar_prefetch=3,
            grid=(b * t // tm,),
            in_specs=[pl.BlockSpec(memory_space=pl.ANY),
                      pl.BlockSpec((tm, 128), lambda i, *_: (i + tile0, 0)),
                      pl.BlockSpec((None, 16, tm), lambda i, *_: (i + tile0, 0, 0)),
                      pl.BlockSpec((tm, d), lambda i, *_: (i, 0)),
                      pl.BlockSpec((nseg, 1, d), lambda i, *_: (i // tiles_per_b, 0, 0)),
                      pl.BlockSpec((1, d), lambda i, *_: (0, 0))],
            out_specs=pl.BlockSpec((tm, d), lambda i, *_: (i, 0)),
            scratch_shapes=[pltpu.VMEM((MOE_LOCAL, d), F32), pltpu.SemaphoreType.DMA]),
        compiler_params=_cparams(("arbitrary",)),
        name="combine",
    )(pc, loc, goff, yb, rt, rtt, x.reshape(b * t, d), gate, g_final)
    return out.reshape(b, t, d)


def _pad_w_in(w_in):
    d = w_in.shape[0]
    n_ba = 4 * DN_HEADS
    pad = jnp.zeros((d, COL_U - COL_BA - n_ba), w_in.dtype)
    return jnp.concatenate([w_in[:, :COL_BA + n_ba], pad, w_in[:, COL_BA + n_ba:]], axis=1).astype(BF16)


def kernel(x, c, ctx, c_ctx, w_ada, b_ada, g_mix, g_ffn, w_in, lru_conv_w, lru_conv_b, lru_w_gate, lru_b_gate, lru_lam, dn_conv_w, dn_a_log, dn_dt_bias, dn_norm_g, s5_lam_re, s5_lam_im, s5_log_dt, s5_b_re, s5_b_im, s5_c_re, s5_c_im, s5_d, s5_w_glu, s5_b_glu, w_br_a, w_br_b, w_br_c, w_out, w_router, b_router, w_e1, b_e1, w_e2, b_e2, g_final):
    bsz, t, d = x.shape
    n_layers = w_in.shape[0]
    s = jnp.concatenate([c, jnp.broadcast_to(c_ctx[None], (SUBLANES, d))], axis=0)
    mods = ada_modulation(s, w_ada.astype(BF16), b_ada)
    x_lat, x_ctx = x, ctx
    for l in range(n_layers):
        last = l == n_layers - 1
        p = _layer_params(l, g_mix, g_ffn, w_in, lru_conv_w, lru_conv_b, lru_w_gate, lru_b_gate, lru_lam, dn_conv_w,
                          dn_a_log, dn_dt_bias, dn_norm_g, s5_lam_re, s5_lam_im, s5_log_dt, s5_b_re, s5_b_im,
                          s5_c_re, s5_c_im, s5_d, s5_w_glu, s5_b_glu, w_br_a, w_br_b, w_br_c, w_out, w_router,
                          b_router)
        m_lat = mods[l, :bsz].reshape(bsz, 6, d)
        m_ctx = jnp.broadcast_to(mods[l, bsz].reshape(1, 6, d), (bsz, 6, d))
        ctx_out, ctx_states = mix_stream(x_ctx, m_ctx, p, None, x_ctx.shape[1], not last)
        lat_out, _ = mix_stream(x_lat, m_lat, p, ctx_states, GRID_W, True)
        w1, b1 = w_e1[l], b_e1[l][:, None, :]
        w2, b2 = w_e2[l], b_e2[l][:, None, :]
        if last:
            x_mid, h, lg = lat_out
            routed = moe(h.reshape(-1, d), lg.reshape(-1, 128), w1, b1, w2, b2)
            x_lat = combine(*routed, 0, x_mid, m_lat, g_final[None], final_norm=True)
        else:
            xc_mid, hc, lgc = ctx_out
            xl_mid, hl, lgl = lat_out
            n_ctx = bsz * x_ctx.shape[1]
            routed = moe(jnp.concatenate([hc.reshape(-1, d), hl.reshape(-1, d)], axis=0),
                         jnp.concatenate([lgc.reshape(-1, 128), lgl.reshape(-1, 128)], axis=0), w1, b1, w2, b2)
            x_ctx = combine(*routed, 0, xc_mid, m_ctx, g_final[None], final_norm=False)
            x_lat = combine(*routed, n_ctx, xl_mid, m_lat, g_final[None], final_norm=False)
    return x_lat


def _layer_params(l, g_mix, g_ffn, w_in, lru_conv_w, lru_conv_b, lru_w_gate, lru_b_gate, lru_lam, dn_conv_w,
                  dn_a_log, dn_dt_bias, dn_norm_g, s5_lam_re, s5_lam_im, s5_log_dt, s5_b_re, s5_b_im, s5_c_re,
                  s5_c_im, s5_d, s5_w_glu, s5_b_glu, w_br_a, w_br_b, w_br_c, w_out, w_router, b_router):
    s5 = [s5_lam_re[l], s5_lam_im[l], s5_log_dt[l], s5_b_re[l], s5_b_im[l], s5_c_re[l], s5_c_im[l]]
    wr = jnp.zeros((w_router.shape[1], 128), F32).at[:, :N_EXPERTS].set(w_router[l])
    wr1 = wr.astype(BF16)
    return {
        "g_mix": g_mix[l][None], "w_in": _pad_w_in(w_in[l]),
        "lru_conv_w": lru_conv_w[l], "lru_conv_b": lru_conv_b[l][None],
        "lru_wg": [_lru_gate_dense(lru_w_gate[l, dr]) for dr in range(2)],
        "lru_bg": lru_b_gate[l].reshape(2, 1, 2 * LRU_WIDTH), "lru_lam": lru_lam[l][:, None, :],
        "dn_conv_w": dn_conv_w[l], "dn_a_log": _dn_lane_vec(dn_a_log[l]), "dn_dt_bias": _dn_lane_vec(dn_dt_bias[l]),
        "s5": [_s5_params(*(a[dr] for a in s5)) for dr in range(2)],
        "dn_norm_g": dn_norm_g[l][None], "s5_d": s5_d[l][None], "s5_w_glu": s5_w_glu[l].astype(BF16),
        "s5_b_glu": s5_b_glu[l][None], "w_br_a": w_br_a[l].astype(BF16), "w_br_b": w_br_b[l].astype(BF16),
        "w_br_c": w_br_c[l].astype(BF16), "w_out": w_out[l].astype(BF16), "g_ffn": g_ffn[l][None],
        "w_r1": wr1, "w_r2": (wr - wr1.astype(F32)).astype(BF16),
        "b_router": jnp.full((1, 128), -1e30, F32).at[0, :N_EXPERTS].set(b_router[l]),
    }


def mix_stream(x, mod, p, init, seg, emit):
    bsz = x.shape[0]
    if init is None:
        zl = jnp.zeros((bsz, LRU_WIDTH), F32)
        zd = jnp.zeros((bsz, DN_HEADS, DN_HEAD_DIM, DN_HEAD_DIM), F32)
        zs = jnp.zeros((bsz, S5_LANES), F32)
        init = ((zl, zl), (zd, zd), ((zs, zs), (zs, zs)))
    proj = inproj(x, mod, p["g_mix"], p["w_in"])

    lru = lambda dr, prev: lru_scan(proj, p["lru_conv_w"], p["lru_conv_b"], p["lru_wg"][dr], p["lru_bg"][dr],
                                    p["lru_lam"][dr], init[0][dr], prev, seg=seg, reverse=dr == 1)
    h_f, lru_f = lru(0, None)
    h_sum, lru_b = lru(1, h_f)

    dn = lambda dr, prev: dn_scan(proj, p["dn_conv_w"], p["dn_a_log"], p["dn_dt_bias"], init[1][dr], prev,
                                  seg=seg, reverse=dr == 1)
    o_f, dn_f = dn(0, None)
    o_sum, dn_b = dn(1, o_f)

    s5 = lambda dr, prev: s5_scan(proj, p["s5"][dr], init[2][dr][0], init[2][dr][1], prev, reverse=dr == 1)
    y_f, s5_fr, s5_fi = s5(0, None)
    y_sum, s5_br, s5_bi = s5(1, y_f)

    states = ((lru_f, lru_b), (dn_f, dn_b), ((s5_fr, s5_fi), (s5_br, s5_bi)))
    if not emit:
        return None, states
    return merge_stream(h_sum, proj, o_sum, y_sum, x, mod, p), states


def moe(h, logits, w1, b1, w2, b2):
    n, d = h.shape
    nt = n // MOE_TILE
    rt, rtt, counts = route(logits)
    cnt = counts[:, 0, :N_EXPERTS].astype(jnp.int32)
    pc = (cnt + MOE_PIECE - 1) // MOE_PIECE
    loc = MOE_PIECE * (jnp.cumsum(pc, axis=1) - pc)
    rows_e = MOE_PIECE * jnp.sum(pc, axis=0)
    padded = ((rows_e + MOE_BLOCK - 1) // MOE_BLOCK) * MOE_BLOCK
    ends_pad = jnp.cumsum(padded)
    goff = (ends_pad - padded)[None, :] + MOE_PIECE * (jnp.cumsum(pc, axis=0) - pc)
    n_blocks = -(-(n * TOP_K + (MOE_PIECE - 1) * N_EXPERTS * nt) // MOE_BLOCK) + N_EXPERTS
    block_start = jnp.arange(n_blocks, dtype=jnp.int32) * MOE_BLOCK
    block_e = jnp.minimum(jnp.sum(ends_pad[None, :] <= block_start[:, None], axis=1), N_EXPERTS - 1).astype(jnp.int32)
    n_used = (ends_pad[-1:] // MOE_BLOCK).astype(jnp.int32)
    region_end = ends_pad.at[N_EXPERTS - 1].set(n_blocks * MOE_BLOCK)
    pc = jnp.concatenate([pc, ((region_end - (ends_pad - padded + rows_e)) // MOE_PIECE)[None, :]], axis=0)
    loc = jnp.concatenate([loc, jnp.zeros((1, N_EXPERTS), loc.dtype)], axis=0)
    goff = jnp.concatenate([goff, (ends_pad - padded + rows_e)[None, :]], axis=0)
    pc, loc, goff = (a.reshape(-1).astype(jnp.int32) for a in (pc, loc, goff))
    xs = dispatch(h, rtt, pc, loc, goff, n_blocks * MOE_BLOCK)
    yb = expert_blocks(xs, block_e, n_used, w1, b1, w2, b2)
    return yb, rt, rtt, pc, loc, goff
```

```python
import functools
import math

import jax
import jax.numpy as jnp
from jax import lax
from jax.experimental import pallas as pl
from jax.experimental.pallas import tpu as pltpu

F32 = jnp.float32
BF16 = jnp.bfloat16

D_MODEL = 1024
EPS = 1e-6
CONV_W = 4
CONV_LEFT = CONV_W // 2
GRID_W = 64

LRU_WIDTH = 512
LRU_BLOCKS = 8
LRU_BLOCK = LRU_WIDTH // LRU_BLOCKS
LRU_C = 8.0

DN_HEAD_DIM = 128
DN_HEADS = 8
DN_WIDTH = DN_HEADS * DN_HEAD_DIM
DN_CHUNK = 64
DN_BATCH = 2

S5_WIDTH = 512
S5_GROUP = 16
S5_GROUPS = 32
S5_STATE = 64
S5_LANES = S5_GROUPS * S5_STATE
S5_SPLIT = 4

N_EXPERTS = 32
TOP_K = 4
D_EXPERT = 1024
SWIGLU_LIMIT = 7.0
SWIGLU_ALPHA = 1.702
MOE_BLOCK = 512
MOE_TILE = 512
MOE_PIECE = 16
MOE_LOCAL = MOE_TILE * TOP_K + MOE_PIECE * N_EXPERTS
MOE_CHUNK = 256

COL_AX, COL_AY, COL_Q, COL_K, COL_V, COL_Z = 0, 512, 1024, 2048, 3072, 4096
COL_BA, COL_U, COL_GATE, D_IN_PAD = 5120, 5632, 6144, 9216
D_IN = 8736

SUBLANES = 8
LANES = 128
VMEM_LIMIT = 52 * 1024 * 1024


def _cparams(sem):
    return pltpu.CompilerParams(dimension_semantics=sem, vmem_limit_bytes=VMEM_LIMIT)


def _sigmoid(x):
    return 1.0 / (1.0 + jnp.exp(-x))


def _softplus(x):
    return jnp.maximum(x, 0.0) + jnp.log1p(jnp.exp(-jnp.abs(x)))


def _silu(x):
    return x * _sigmoid(x)


def _gelu_tanh(x):
    return 0.5 * x * (1.0 + jnp.tanh(math.sqrt(2.0 / math.pi) * (x + 0.044715 * x * x * x)))


def _dot(a, b):
    return jnp.dot(a.astype(BF16), b.astype(BF16), preferred_element_type=F32)


def _ada_kernel(s_ref, w_ref, b_ref, o_ref):
    o_ref[...] = _dot(_silu(s_ref[...]), w_ref[...]) + b_ref[...]


def ada_modulation(s, w_ada, b_ada):
    n_layers, d, n = w_ada.shape
    r = s.shape[0]
    tn = 1536
    return pl.pallas_call(
        _ada_kernel,
        out_shape=jax.ShapeDtypeStruct((n_layers, r, n), F32),
        grid=(n_layers, n // tn),
        in_specs=[pl.BlockSpec((r, d), lambda l, j: (0, 0)),
                  pl.BlockSpec((None, d, tn), lambda l, j: (l, 0, j)),
                  pl.BlockSpec((None, 1, tn), lambda l, j: (l, 0, j))],
        out_specs=pl.BlockSpec((None, r, tn), lambda l, j: (l, 0, j)),
        compiler_params=_cparams(("arbitrary", "arbitrary")),
        name="ada_modulation",
    )(s, w_ada, b_ada.reshape(n_layers, 1, n))


def _norm_mod(x, g, shift, scale):
    y = x * lax.rsqrt(jnp.mean(x * x, axis=-1, keepdims=True) + EPS)
    return (y * g) * (1.0 + scale) + shift


def _conv_rows(x, w, seg):
    tc = x.shape[0]
    t_in_seg = lax.broadcasted_iota(jnp.int32, x.shape, 0) % seg
    acc = x * w[CONV_LEFT:CONV_LEFT + 1]
    for j in range(CONV_W):
        off = j - CONV_LEFT
        if off == 0:
            continue
        sh = pltpu.roll(x, (-off) % tc, 0)
        ok = (t_in_seg >= -off) if off < 0 else (t_in_seg < seg - off)
        acc = acc + jnp.where(ok, sh, 0.0) * w[j:j + 1]
    return acc


def _inproj_kernel(x_ref, mod_ref, g_ref, w_ref, cw_ref, o_ref, h_ref, *, seg, qkv_tiles):
    j = pl.program_id(2)

    @pl.when(j == 0)
    def _():
        mod = mod_ref[...]
        h_ref[...] = _norm_mod(x_ref[...], g_ref[...], mod[0:1], mod[1:2]).astype(BF16)

    is_qkv = jnp.logical_and(j >= qkv_tiles[0], j < qkv_tiles[1])

    @pl.when(is_qkv)
    def _():
        y = jnp.dot(h_ref[...], w_ref[...], preferred_element_type=F32)
        o_ref[...] = _silu(_conv_rows(y, cw_ref[...], seg))

    @pl.when(jnp.logical_not(is_qkv))
    def _():
        o_ref[...] = jnp.dot(h_ref[...], w_ref[...], preferred_element_type=F32)


def inproj(x, mod, g, w_pad, dn_conv_w, seg):
    b, t, d = x.shape
    n = w_pad.shape[1]
    tm = min(t, 1024)
    tn = DN_WIDTH
    assert tm % seg == 0 and t % tm == 0 and COL_Q % tn == 0
    q0, q1 = COL_Q // tn, COL_Z // tn
    return pl.pallas_call(
        functools.partial(_inproj_kernel, seg=seg, qkv_tiles=(q0, q1)),
        out_shape=jax.ShapeDtypeStruct((b, t, n), F32),
        grid=(b, t // tm, n // tn),
        in_specs=[pl.BlockSpec((None, tm, d), lambda i, r, j: (i, r, 0)),
                  pl.BlockSpec((None, 6, d), lambda i, r, j: (i, 0, 0)),
                  pl.BlockSpec((1, d), lambda i, r, j: (0, 0)),
                  pl.BlockSpec((d, tn), lambda i, r, j: (0, j)),
                  pl.BlockSpec((CONV_W, tn), lambda i, r, j: (0, jnp.clip(j - q0, 0, q1 - q0 - 1)))],
        out_specs=pl.BlockSpec((None, tm, tn), lambda i, r, j: (i, r, j)),
        scratch_shapes=[pltpu.VMEM((tm, d), BF16)],
        compiler_params=_cparams(("arbitrary", "arbitrary", "arbitrary")),
        name="inproj",
    )(x, mod, g, w_pad, dn_conv_w)


def _conv_time_major(x, w, seg):
    tc = x.shape[0]
    t_in_seg = lax.broadcasted_iota(jnp.int32, x.shape, 0) % seg
    acc = x * w[CONV_LEFT:CONV_LEFT + 1][None]
    for j in range(CONV_W):
        off = j - CONV_LEFT
        if off == 0:
            continue
        if off < 0:
            sh = jnp.concatenate([jnp.zeros((-off,) + x.shape[1:], x.dtype), x[:tc + off]], axis=0)
            ok = t_in_seg >= -off
        else:
            sh = jnp.concatenate([x[off:], jnp.zeros((off,) + x.shape[1:], x.dtype)], axis=0)
            ok = t_in_seg < seg - off
        acc = acc + jnp.where(ok, sh, 0.0) * w[j:j + 1][None]
    return acc


def _to_time_major(x_ref, xt_ref):
    bc, tc, w = x_ref.shape
    for j in range(w // LANES):
        for b in range(bc):
            xt_ref[j, pl.ds(b, tc, stride=bc), :] = x_ref[b, :, j * LANES:(j + 1) * LANES]


def _from_time_major(xt_ref, o_ref, prev_ref):
    bc, tc, w = o_ref.shape
    for j in range(w // LANES):
        for b in range(bc):
            v = xt_ref[j, pl.ds(b, tc, stride=bc), :]
            if prev_ref is not None:
                v = v + prev_ref[b, :, j * LANES:(j + 1) * LANES]
            o_ref[b, :, j * LANES:(j + 1) * LANES] = v


def _lru_kernel(x_ref, cw_ref, cb_ref, wg_ref, bg_ref, lam_ref, h0_ref, *rest, seg, reverse, add_prev):
    if add_prev:
        prev_ref, o_ref, hfin_ref, xt_ref, a_ref, b_ref, h_ref = rest
    else:
        prev_ref = None
        o_ref, hfin_ref, xt_ref, a_ref, b_ref, h_ref = rest
    bc, tc, w = x_ref.shape
    step = pl.program_id(1)

    @pl.when(step == 0)
    def _():
        h_ref[...] = h0_ref[...]

    _to_time_major(x_ref, xt_ref)
    x = jnp.concatenate([xt_ref[j] for j in range(w // LANES)], axis=-1).reshape(tc, bc, w)
    xc = _conv_time_major(x, cw_ref[...], seg) + cb_ref[...][None]
    gates = _dot(xc.reshape(tc * bc, w), wg_ref[...]) + bg_ref[...]
    r = _sigmoid(gates[:, :w]).reshape(tc, bc, w)
    i = _sigmoid(gates[:, w:]).reshape(tc, bc, w)
    log_a = (-LRU_C * _softplus(-lam_ref[...]))[None] * r
    a_ref[...] = jnp.exp(log_a)
    b_ref[...] = jnp.sqrt(1.0 - jnp.exp(2.0 * log_a)) * (i * xc)

    def body(s, h):
        t = tc - 1 - s if reverse else s
        h = a_ref[t] * h + b_ref[t]
        rows = pl.ds(pl.multiple_of(t * bc, bc), bc)
        for j in range(w // LANES):
            xt_ref[j, rows, :] = h[:, j * LANES:(j + 1) * LANES]
        return h

    h = lax.fori_loop(0, tc, body, h_ref[...], unroll=8)
    h_ref[...] = h
    hfin_ref[...] = h
    _from_time_major(xt_ref, o_ref, prev_ref)


def lru_scan(proj, conv_w, conv_b, wg, bg, lam, h0, prev, *, seg, reverse):
    b, t, _ = proj.shape
    w = LRU_WIDTH
    tc = min(t, 256)
    bc = SUBLANES
    assert tc % seg == 0 and t % tc == 0 and b % bc == 0
    nt = t // tc
    tix = (lambda s: nt - 1 - s) if reverse else (lambda s: s)
    blk = lambda cb: pl.BlockSpec((bc, tc, w), lambda i, s: (i, tix(s), cb))
    const = lambda shape: pl.BlockSpec(shape, lambda i, s: (0,) * len(shape))
    in_specs = [blk(COL_AX // w), const((CONV_W, w)), const((1, w)), const((w, 2 * w)), const((1, 2 * w)),
                const((1, w)), pl.BlockSpec((bc, w), lambda i, s: (i, 0))]
    args = [proj, conv_w, conv_b, wg, bg, lam, h0]
    if prev is not None:
        in_specs.append(blk(0))
        args.append(prev)
    return pl.pallas_call(
        functools.partial(_lru_kernel, seg=seg, reverse=reverse, add_prev=prev is not None),
        out_shape=(jax.ShapeDtypeStruct((b, t, w), F32), jax.ShapeDtypeStruct((b, w), F32)),
        grid=(b // bc, nt),
        in_specs=in_specs,
        out_specs=(blk(0), pl.BlockSpec((bc, w), lambda i, s: (i, 0))),
        scratch_shapes=[pltpu.VMEM((w // LANES, tc * bc, LANES), F32), pltpu.VMEM((tc, bc, w), F32),
                        pltpu.VMEM((tc, bc, w), F32), pltpu.VMEM((bc, w), F32)],
        compiler_params=_cparams(("arbitrary", "arbitrary")),
        name="lru_scan_bwd" if reverse else "lru_scan_fwd",
    )(*args)


def _lru_gate_dense(w_gate):
    eye = jnp.eye(LRU_BLOCKS, dtype=w_gate.dtype)
    dense = jnp.einsum("gnij,nm->gnimj", w_gate, eye).reshape(2, LRU_WIDTH, LRU_WIDTH)
    return jnp.concatenate([dense[0], dense[1]], axis=1).astype(BF16)


def _s5_kernel(u_ref, bre_ref, bim_ref, cre_ref, cim_ref, ar_ref, ai_ref, h0r_ref, h0i_ref, *rest,
               reverse, add_prev):
    if add_prev:
        prev_ref, o_ref, fr_ref, fi_ref, ut_ref, xr_ref, xi_ref, hr_ref, hi_ref = rest
    else:
        prev_ref = None
        o_ref, fr_ref, fi_ref, ut_ref, xr_ref, xi_ref, hr_ref, hi_ref = rest
    bc, tc, w = u_ref.shape
    blk_in = w // S5_SPLIT
    blk_st = S5_LANES // S5_SPLIT
    assert blk_in == LANES

    @pl.when(pl.program_id(1) == 0)
    def _():
        hr_ref[...] = h0r_ref[...]
        hi_ref[...] = h0i_ref[...]

    _to_time_major(u_ref, ut_ref)
    for j in range(S5_SPLIT):
        uj = ut_ref[j].astype(BF16)
        xr_ref[:, :, j * blk_st:(j + 1) * blk_st] = jnp.dot(
            uj, bre_ref[j], preferred_element_type=F32).reshape(tc, bc, blk_st)
        xi_ref[:, :, j * blk_st:(j + 1) * blk_st] = jnp.dot(
            uj, bim_ref[j], preferred_element_type=F32).reshape(tc, bc, blk_st)

    ar = jnp.broadcast_to(ar_ref[...], (bc, S5_LANES))
    ai = jnp.broadcast_to(ai_ref[...], (bc, S5_LANES))

    def body(s, carry):
        hr, hi = carry
        t = tc - 1 - s if reverse else s
        nr = ar * hr - ai * hi + xr_ref[t]
        ni = ar * hi + ai * hr + xi_ref[t]
        xr_ref[t] = nr
        xi_ref[t] = ni
        return nr, ni

    hr, hi = lax.fori_loop(0, tc, body, (hr_ref[...], hi_ref[...]))
    hr_ref[...] = hr
    hi_ref[...] = hi
    fr_ref[...] = hr
    fi_ref[...] = hi

    for j in range(S5_SPLIT):
        hrj = xr_ref[:, :, j * blk_st:(j + 1) * blk_st].reshape(tc * bc, blk_st).astype(BF16)
        hij = xi_ref[:, :, j * blk_st:(j + 1) * blk_st].reshape(tc * bc, blk_st).astype(BF16)
        ut_ref[j] = (jnp.dot(hrj, cre_ref[j], preferred_element_type=F32)
                     - jnp.dot(hij, cim_ref[j], preferred_element_type=F32))
    _from_time_major(ut_ref, o_ref, prev_ref)


def s5_scan(proj, prm, h0r, h0i, prev, *, reverse):
    b, t, _ = proj.shape
    w = S5_WIDTH
    tc = min(t, 64)
    bc = SUBLANES
    assert t % tc == 0 and b % bc == 0
    nt = t // tc
    tix = (lambda s: nt - 1 - s) if reverse else (lambda s: s)
    blk = lambda cb: pl.BlockSpec((bc, tc, w), lambda i, s: (i, tix(s), cb))
    const = lambda shape: pl.BlockSpec(shape, lambda i, s: (0,) * len(shape))
    st = pl.BlockSpec((bc, S5_LANES), lambda i, s: (i, 0))
    bre, bim, cre, cim, ar, ai = prm
    in_specs = [blk(COL_U // w), const(bre.shape), const(bim.shape), const(cre.shape), const(cim.shape),
                const(ar.shape), const(ai.shape), st, st]
    args = [proj, bre, bim, cre, cim, ar, ai, h0r, h0i]
    if prev is not None:
        in_specs.append(blk(0))
        args.append(prev)
    st_shape = jax.ShapeDtypeStruct((b, S5_LANES), F32)
    return pl.pallas_call(
        functools.partial(_s5_kernel, reverse=reverse, add_prev=prev is not None),
        out_shape=(jax.ShapeDtypeStruct((b, t, w), F32), st_shape, st_shape),
        grid=(b // bc, nt),
        in_specs=in_specs,
        out_specs=(blk(0), st, st),
        scratch_shapes=[pltpu.VMEM((w // LANES, tc * bc, LANES), F32),
                        pltpu.VMEM((tc, bc, S5_LANES), F32), pltpu.VMEM((tc, bc, S5_LANES), F32),
                        pltpu.VMEM((bc, S5_LANES), F32), pltpu.VMEM((bc, S5_LANES), F32)],
        compiler_params=_cparams(("arbitrary", "arbitrary")),
        name="s5_scan_bwd" if reverse else "s5_scan_fwd",
    )(*args)


def _s5_params(lam_re, lam_im, log_dt, b_re, b_im, c_re, c_im):
    dt = jnp.exp(log_dt)[:, None]
    mag = jnp.exp(lam_re * dt)
    ar, ai = mag * jnp.cos(lam_im * dt), mag * jnp.sin(lam_im * dt)
    den = lam_re * lam_re + lam_im * lam_im
    fr = ((ar - 1.0) * lam_re + ai * lam_im) / den
    fi = (ai * lam_re - (ar - 1.0) * lam_im) / den
    bb_re = fr[..., None] * b_re - fi[..., None] * b_im
    bb_im = fr[..., None] * b_im + fi[..., None] * b_re
    gl = S5_GROUPS // S5_SPLIT
    eye = jnp.eye(gl, dtype=F32)

    def pack_in(bb):
        bb = jnp.transpose(bb, (0, 2, 1)).reshape(S5_SPLIT, gl, S5_GROUP, S5_STATE)
        return jnp.einsum("jghp,gm->jghmp", bb, eye).reshape(S5_SPLIT, gl * S5_GROUP, gl * S5_STATE).astype(BF16)

    def pack_out(c):
        c = jnp.transpose(c, (0, 2, 1)).reshape(S5_SPLIT, gl, S5_STATE, S5_GROUP)
        return jnp.einsum("jgph,gm->jgpmh", c, eye).reshape(S5_SPLIT, gl * S5_STATE, gl * S5_GROUP).astype(BF16)

    return (pack_in(bb_re), pack_in(bb_im), pack_out(c_re), pack_out(c_im),
            ar.reshape(1, S5_LANES), ai.reshape(1, S5_LANES))


def _l2norm(x):
    return x * lax.rsqrt(jnp.sum(x * x, axis=-1, keepdims=True) + EPS)


def _dot_nt(a, b):
    return lax.dot_general(a.astype(BF16), b.astype(BF16), (((1,), (1,)), ((), ())), preferred_element_type=F32)


def _dot_tn(a, b):
    return lax.dot_general(a.astype(BF16), b.astype(BF16), (((0,), (0,)), ((), ())), preferred_element_type=F32)


def _split3_dot(m, x):
    x1 = x.astype(BF16)
    r1 = x - x1.astype(F32)
    x2 = r1.astype(BF16)
    x3 = (r1 - x2.astype(F32)).astype(BF16)
    mb = m.astype(BF16)
    dot = lambda v: jnp.dot(mb, v, preferred_element_type=F32)
    return dot(x1) + dot(x2) + dot(x3)


def _dn_kernel(q_ref, k_ref, v_ref, ba_ref, alog_ref, dtb_ref, s0_ref, *rest, reverse, add_prev, dirn):
    if add_prev:
        prev_ref, o_ref, sfin_ref, beta_ref, g_ref, s_ref = rest
    else:
        o_ref, sfin_ref, beta_ref, g_ref, s_ref = rest
    nb, tc, wq = q_ref.shape
    n_chunks = tc // DN_CHUNK
    step = pl.program_id(1)

    @pl.when(step == 0)
    def _():
        s_ref[...] = s0_ref[...]

    ba = ba_ref[...].reshape(nb * tc, ba_ref.shape[-1])
    beta_ref[...] = _sigmoid(ba)
    g_ref[...] = -jnp.exp(alog_ref[...]) * _softplus(ba + dtb_ref[...])

    ii = lax.broadcasted_iota(jnp.int32, (DN_CHUNK, DN_CHUNK), 0)
    jj = lax.broadcasted_iota(jnp.int32, (DN_CHUNK, DN_CHUNK), 1)
    incl = (ii <= jj) if reverse else (ii >= jj)
    strict = (ii < jj) if reverse else (ii > jj)
    eye = (ii == jj).astype(F32)
    tri = incl.astype(F32)
    last = 0 if reverse else DN_CHUNK - 1
    neg = jnp.float32(-1e30)

    def chunk(ci, carry):
        cidx = n_chunks - 1 - ci if reverse else ci
        r0 = pl.multiple_of(cidx * DN_CHUNK, DN_CHUNK)
        rows = pl.ds(r0, DN_CHUNK)
        each = lambda f, *cols: [f(*args) for args in zip(*cols)]
        srows = [pl.ds(pl.multiple_of(bb * tc + r0, DN_CHUNK), DN_CHUNK) for bb in range(nb)]
        g_cum = [_split3_dot(tri, g_ref[sr, :]) for sr in srows]
        g_t = [g.T for g in g_cum]
        e_g = [jnp.exp(g) for g in g_cum]
        e_rest = [jnp.exp(g[last:last + 1] - g) for g in g_cum]
        beta_all = [beta_ref[sr, :] for sr in srows]
        pairs = [(bb, h) for bb in range(nb) for h in range(DN_HEADS)]
        lanes = [slice(h * DN_HEAD_DIM, (h + 1) * DN_HEAD_DIM) for _, h in pairs]
        cb = [dirn * DN_HEADS + h for _, h in pairs]
        cg = [2 * DN_HEADS + c for c in cb]
        q = [_l2norm(q_ref[bb, rows, ln]) * (DN_HEAD_DIM ** -0.5) for (bb, _), ln in zip(pairs, lanes)]
        k = [_l2norm(k_ref[bb, rows, ln]) for (bb, _), ln in zip(pairs, lanes)]
        v = [v_ref[bb, rows, ln] for (bb, _), ln in zip(pairs, lanes)]
        beta = [beta_all[bb][:, c:c + 1] for (bb, _), c in zip(pairs, cb)]
        decay = [jnp.exp(jnp.where(incl, g_cum[bb][:, c:c + 1] - g_t[bb][c:c + 1, :], neg))
                 for (bb, _), c in zip(pairs, cg)]
        nc = DN_CHUNK
        kb = each(lambda a, b: a * b, k, beta)
        kq = each(lambda a, b, c: _dot_nt(jnp.concatenate([a, b], axis=0), c), kb, q, k)
        a_low = each(lambda r, dc: jnp.where(strict, r[:nc] * dc, 0.0), kq, decay)
        attn = each(lambda r, dc: jnp.where(incl, r[nc:] * dc, 0.0), kq, decay)
        x = [eye - a for a in a_low]
        p = [_dot(a, a) for a in a_low]
        for _ in range(4):
            xp = each(lambda a, b: _dot(jnp.concatenate([a, b], axis=0), b), x, p)
            x = each(lambda a, r: a + r[:nc], x, xp)
            p = [r[nc:] for r in xp]
        x = each(lambda a, b: a + _dot(a, b), x, p)
        e_col = [e_g[bb][:, c:c + 1] for (bb, _), c in zip(pairs, cg)]
        uw = each(lambda a, vv, bt, kk, ec: _dot(a, jnp.concatenate([vv * bt, kk * ec], axis=1)),
                  x, v, beta, kb, e_col)
        q_dec = each(lambda a, b: a * b, q, e_col)
        k_dec = [kk * e_rest[bb][:, c:c + 1] for kk, (bb, _), c in zip(k, pairs, cg)]
        e_tot = [e_g[bb][last:last + 1, c:c + 1] for (bb, _), c in zip(pairs, cg)]
        s = [s_ref[bb, h] for bb, h in pairs]
        ws = each(lambda r, qd, ss: _dot(jnp.concatenate([r[:, DN_HEAD_DIM:], qd], axis=0), ss), uw, q_dec, s)
        v_new = each(lambda r, m: r[:, :DN_HEAD_DIM] - m[:nc], uw, ws)
        o = each(lambda m, at, vn: m[nc:] + _dot(at, vn), ws, attn, v_new)
        s_new = each(lambda ss, et, kd, vn: ss * et + _dot_tn(kd, vn), s, e_tot, k_dec, v_new)
        if add_prev:
            o = [oo + prev_ref[bb, rows, ln] for oo, (bb, _), ln in zip(o, pairs, lanes)]
        for (bb, h), ss in zip(pairs, s_new):
            s_ref[bb, h] = ss
        for (bb, _), ln, oo in zip(pairs, lanes, o):
            o_ref[bb, rows, ln] = oo
        return carry

    lax.fori_loop(0, n_chunks, chunk, 0)

    @pl.when(step == pl.num_programs(1) - 1)
    def _():
        sfin_ref[...] = s_ref[...]


def dn_scan(proj, a_log, dt_bias, s0, prev, *, reverse):
    b, t, _ = proj.shape
    tc = min(t, 256)
    nb = DN_BATCH
    assert tc % DN_CHUNK == 0 and t % tc == 0 and b % nb == 0
    nt = t // tc
    dirn = 1 if reverse else 0
    tix = (lambda s: nt - 1 - s) if reverse else (lambda s: s)
    wq = DN_WIDTH
    col = lambda cb: pl.BlockSpec((nb, tc, wq), lambda i, s: (i, tix(s), cb))
    vec = pl.BlockSpec((1, 128), lambda i, s: (0, 0))
    st = pl.BlockSpec((nb, DN_HEADS, DN_HEAD_DIM, DN_HEAD_DIM), lambda i, s: (i, 0, 0, 0))
    oblk = pl.BlockSpec((nb, tc, wq), lambda i, s: (i, tix(s), 0))
    in_specs = [col(COL_Q // wq), col(COL_K // wq), col(COL_V // wq),
                pl.BlockSpec((nb, tc, 128), lambda i, s: (i, tix(s), COL_BA // 128)), vec, vec, st]
    args = [proj, proj, proj, proj, a_log, dt_bias, s0]
    if prev is not None:
        in_specs.append(oblk)
        args.append(prev)
    return pl.pallas_call(
        functools.partial(_dn_kernel, reverse=reverse, add_prev=prev is not None, dirn=dirn),
        out_shape=(jax.ShapeDtypeStruct((b, t, wq), F32),
                   jax.ShapeDtypeStruct((b, DN_HEADS, DN_HEAD_DIM, DN_HEAD_DIM), F32)),
        grid=(b // nb, nt),
        in_specs=in_specs,
        out_specs=(oblk, st),
        scratch_shapes=[pltpu.VMEM((nb * tc, 128), F32), pltpu.VMEM((nb * tc, 128), F32),
                        pltpu.VMEM((nb, DN_HEADS, DN_HEAD_DIM, DN_HEAD_DIM), F32)],
        compiler_params=_cparams(("arbitrary", "arbitrary")),
        name="dn_scan_bwd" if reverse else "dn_scan_fwd",
    )(*args)


def _dn_lane_vec(p):
    return jnp.zeros((1, 128), F32).at[0, 2 * DN_HEADS:4 * DN_HEADS].set(p.reshape(-1))


def _merge_kernel(hl_ref, ay_ref, od_ref, z_ref, ys_ref, u_ref, gt_ref, x_ref, mod_ref, ng_ref, sd_ref,
                  wglu_ref, bglu_ref, wa_ref, wb_ref, wc_ref, wo_ref, gf_ref, wr1_ref, wr2_ref, br_ref,
                  xo_ref, h_ref, lg_ref):
    d = x_ref.shape[-1]
    mod = mod_ref[...]
    y_a = hl_ref[...] * _gelu_tanh(ay_ref[...])
    acc = _sigmoid(gt_ref[:, 0:d]) * _dot(y_a, wa_ref[...])

    ng = ng_ref[...]
    heads = []
    for h in range(DN_HEADS):
        lanes = slice(h * DN_HEAD_DIM, (h + 1) * DN_HEAD_DIM)
        o = od_ref[:, lanes]
        o = o * lax.rsqrt(jnp.mean(o * o, axis=-1, keepdims=True) + EPS) * ng
        heads.append((o * _silu(z_ref[:, lanes])).astype(BF16))
    y_b = jnp.concatenate(heads, axis=-1)
    acc = acc + _sigmoid(gt_ref[:, d:2 * d]) * jnp.dot(y_b, wb_ref[...], preferred_element_type=F32)

    y_c = _gelu_tanh(ys_ref[...] + sd_ref[...] * u_ref[...])
    y_c = y_c * _sigmoid(_dot(y_c, wglu_ref[...]) + bglu_ref[...])
    acc = acc + _sigmoid(gt_ref[:, 2 * d:3 * d]) * _dot(y_c, wc_ref[...])

    x_new = x_ref[...] + mod[2:3] * _dot(acc, wo_ref[...])
    xo_ref[...] = x_new
    h = _norm_mod(x_new, gf_ref[...], mod[3:4], mod[4:5])
    h1 = h.astype(BF16)
    h2 = (h - h1.astype(F32)).astype(BF16)
    h_ref[...] = h1
    lg_ref[...] = (jnp.dot(h1, wr1_ref[...], preferred_element_type=F32)
                   + jnp.dot(h1, wr2_ref[...], preferred_element_type=F32)
                   + jnp.dot(h2, wr1_ref[...], preferred_element_type=F32)) + br_ref[...]


def merge_stream(h_lru, proj, o_dn, y_s5, x, mod, p):
    b, t, d = x.shape
    tm = min(t, 256)
    row = lambda w, cb: pl.BlockSpec((None, tm, w), lambda i, r: (i, r, cb))
    const = lambda a: pl.BlockSpec(a.shape, lambda i, r: (0,) * a.ndim)
    consts = [p["dn_norm_g"], p["s5_d"], p["s5_w_glu"], p["s5_b_glu"], p["w_br_a"], p["w_br_b"], p["w_br_c"],
              p["w_out"], p["g_ffn"], p["w_r1"], p["w_r2"], p["b_router"]]
    return pl.pallas_call(
        _merge_kernel,
        out_shape=(jax.ShapeDtypeStruct((b, t, d), F32), jax.ShapeDtypeStruct((b, t, d), BF16),
                   jax.ShapeDtypeStruct((b, t, 128), F32)),
        grid=(b, t // tm),
        in_specs=[row(LRU_WIDTH, 0), row(LRU_WIDTH, COL_AY // LRU_WIDTH), row(DN_WIDTH, 0),
                  row(DN_WIDTH, COL_Z // DN_WIDTH), row(S5_WIDTH, 0), row(S5_WIDTH, COL_U // S5_WIDTH),
                  row(3 * d, COL_GATE // (3 * d)), row(d, 0),
                  pl.BlockSpec((None, 6, d), lambda i, r: (i, 0, 0))] + [const(a) for a in consts],
        out_specs=(row(d, 0), row(d, 0), row(128, 0)),
        compiler_params=_cparams(("arbitrary", "arbitrary")),
        name="merge_stream",
    )(h_lru, proj, o_dn, proj, y_s5, proj, proj, x, mod, *consts)


def _route_kernel(lg_ref, o_ref, ot_ref, cnt_ref):
    tm = lg_ref.shape[0]
    logits = lg_ref[...]
    lane = lax.broadcasted_iota(jnp.int32, logits.shape, 1)
    neg = jnp.float32(-jnp.inf)
    vals, idxs = [], []
    sel = jnp.zeros(logits.shape, F32)
    for _ in range(TOP_K):
        m = jnp.max(logits, axis=-1, keepdims=True)
        idx = jnp.min(jnp.where(logits == m, lane, 128), axis=-1, keepdims=True)
        hit = lane == idx
        logits = jnp.where(hit, neg, logits)
        sel = jnp.where(hit, 1.0, sel)
        vals.append(m)
        idxs.append(idx)
    exps = [jnp.exp(v - vals[0]) for v in vals]
    inv = 1.0 / (exps[0] + exps[1] + exps[2] + exps[3])

    ii = lax.broadcasted_iota(jnp.int32, (tm, tm), 0)
    jj = lax.broadcasted_iota(jnp.int32, (tm, tm), 1)
    before = jnp.dot((ii > jj).astype(BF16), sel.astype(BF16), preferred_element_type=F32)
    cnt = jnp.sum(sel, axis=0, keepdims=True)
    pieces = jnp.floor((cnt + (MOE_PIECE - 1)) * (1.0 / MOE_PIECE))
    ei = lax.broadcasted_iota(jnp.int32, (128, 128), 0)
    ej = lax.broadcasted_iota(jnp.int32, (128, 128), 1)
    start = MOE_PIECE * jnp.dot(jnp.broadcast_to(pieces, (SUBLANES, 128)).astype(BF16), (ei < ej).astype(BF16),
                                preferred_element_type=F32)[0:1]
    pos = before + start
    out = jnp.zeros(logits.shape, F32)
    for k in range(TOP_K):
        row = jnp.sum(jnp.where(lane == idxs[k], pos, 0.0), axis=-1, keepdims=True)
        out = jnp.where(lane == k, idxs[k].astype(F32), out)
        out = jnp.where(lane == TOP_K + k, row, out)
        out = jnp.where(lane == 2 * TOP_K + k, exps[k] * inv, out)
    o_ref[...] = out
    ot_ref[...] = out.T[:ot_ref.shape[0]]
    cnt_ref[...] = jnp.broadcast_to(cnt, cnt_ref.shape)


def route(logits):
    n = logits.shape[0]
    tm = MOE_TILE
    nt = n // tm
    return pl.pallas_call(
        _route_kernel,
        out_shape=(jax.ShapeDtypeStruct((n, 128), F32), jax.ShapeDtypeStruct((nt, 16, tm), F32),
                   jax.ShapeDtypeStruct((nt, SUBLANES, 128), F32)),
        grid=(nt,),
        in_specs=[pl.BlockSpec((tm, 128), lambda i: (i, 0))],
        out_specs=(pl.BlockSpec((tm, 128), lambda i: (i, 0)), pl.BlockSpec((None, 16, tm), lambda i: (i, 0, 0)),
                   pl.BlockSpec((None, SUBLANES, 128), lambda i: (i, 0, 0))),
        compiler_params=_cparams(("arbitrary",)),
        name="route",
    )(logits)


def _piece_copies(pc_ref, loc_ref, goff_ref, base, make_copy, wait):
    for e in range(N_EXPERTS):
        n = pc_ref[base + e]
        loc = loc_ref[base + e]
        goff = goff_ref[base + e]
        big = 16 * MOE_PIECE

        def body(j, carry, loc=loc, goff=goff):
            cp = make_copy(pl.multiple_of(loc + j * big, MOE_PIECE), pl.multiple_of(goff + j * big, MOE_PIECE), big)
            cp.wait() if wait else cp.start()
            return carry

        n_big = lax.shift_right_logical(n, 4)
        lax.fori_loop(0, n_big, body, 0)
        done = n_big * big
        for bit in (8, 4, 2, 1):
            rows = bit * MOE_PIECE
            off = done

            @pl.when((n & bit) != 0)
            def _(off=off, rows=rows, loc=loc, goff=goff):
                cp = make_copy(pl.multiple_of(loc + off, MOE_PIECE), pl.multiple_of(goff + off, MOE_PIECE), rows)
                cp.wait() if wait else cp.start()

            done = done + jnp.where((n & bit) != 0, rows, 0)


def _local_onehot_rows(rtt, c):
    rows = (lax.broadcasted_iota(jnp.int32, (MOE_CHUNK, rtt.shape[1]), 0) + c * MOE_CHUNK).astype(F32)
    return [rows == rtt[TOP_K + k:TOP_K + k + 1] for k in range(TOP_K)]


def _dispatch_kernel(pc_ref, loc_ref, goff_ref, h_ref, rtt_ref, xs_ref, xloc_ref, zero_ref, sem):
    tile = pl.program_id(0)
    n_tiles = pl.num_programs(0)
    rtt = rtt_ref[...]
    h = h_ref[...]
    for c in range(MOE_LOCAL // MOE_CHUNK):
        hit = _local_onehot_rows(rtt, c)
        p = jnp.where(hit[0], 1.0, jnp.where(hit[1], 1.0, jnp.where(hit[2], 1.0, jnp.where(hit[3], 1.0, 0.0))))
        xloc_ref[c * MOE_CHUNK:(c + 1) * MOE_CHUNK, :] = jnp.dot(
            p.astype(BF16), h, preferred_element_type=F32).astype(BF16)
    make = lambda src, dst, rows: pltpu.make_async_copy(
        xloc_ref.at[pl.ds(src, rows)], xs_ref.at[pl.ds(dst, rows)], sem)
    _piece_copies(pc_ref, loc_ref, goff_ref, tile * N_EXPERTS, make, wait=False)
    _piece_copies(pc_ref, loc_ref, goff_ref, tile * N_EXPERTS, make, wait=True)

    @pl.when(tile == n_tiles - 1)
    def _():
        zero_ref[...] = jnp.zeros_like(zero_ref)
        fill = lambda src, dst, rows: pltpu.make_async_copy(
            zero_ref.at[pl.ds(0, rows)], xs_ref.at[pl.ds(dst, rows)], sem)
        _piece_copies(pc_ref, loc_ref, goff_ref, n_tiles * N_EXPERTS, fill, wait=False)
        _piece_copies(pc_ref, loc_ref, goff_ref, n_tiles * N_EXPERTS, fill, wait=True)


def dispatch(h, rtt, pc, loc, goff, n_rows):
    n, d = h.shape
    nt = n // MOE_TILE
    return pl.pallas_call(
        _dispatch_kernel,
        out_shape=jax.ShapeDtypeStruct((n_rows, d), BF16),
        grid_spec=pltpu.PrefetchScalarGridSpec(
            num_scalar_prefetch=3,
            grid=(nt,),
            in_specs=[pl.BlockSpec((MOE_TILE, d), lambda i, *_: (i, 0)),
                      pl.BlockSpec((None, 16, MOE_TILE), lambda i, *_: (i, 0, 0))],
            out_specs=pl.BlockSpec(memory_space=pl.ANY),
            scratch_shapes=[pltpu.VMEM((MOE_LOCAL, d), BF16), pltpu.VMEM((MOE_BLOCK, d), BF16),
                            pltpu.SemaphoreType.DMA]),
        compiler_params=_cparams(("arbitrary",)),
        name="dispatch",
    )(pc, loc, goff, h, rtt)


def _expert_kernel(be_ref, nu_ref, x_ref, w1_ref, b1_ref, w2_ref, b2_ref, o_ref, w1b_ref, w2b_ref):
    i = pl.program_id(0)

    @pl.when(i < nu_ref[0])
    def _():
        @pl.when(jnp.logical_or(i == 0, be_ref[i] != be_ref[jnp.maximum(i - 1, 0)]))
        def _():
            w1b_ref[...] = w1_ref[...].astype(BF16)
            w2b_ref[...] = w2_ref[...].astype(BF16)

        de = w2_ref.shape[0]
        gu = jnp.dot(x_ref[...], w1b_ref[...], preferred_element_type=F32) + b1_ref[...]
        glu = jnp.minimum(gu[:, :de], SWIGLU_LIMIT)
        lin = jnp.clip(gu[:, de:], -SWIGLU_LIMIT, SWIGLU_LIMIT)
        act = glu * _sigmoid(SWIGLU_ALPHA * glu) * (lin + 1.0)
        o_ref[...] = jnp.dot(act.astype(BF16), w2b_ref[...], preferred_element_type=F32) + b2_ref[...]

    @pl.when(i >= nu_ref[0])
    def _():
        o_ref[...] = jnp.zeros_like(o_ref)


def expert_blocks(xs, block_e, n_used, w1, b1, w2, b2):
    rows, d = xs.shape
    n_blocks = rows // MOE_BLOCK
    de = w2.shape[1]
    blk = lambda i, be, nu: (jnp.minimum(i, nu[0] - 1), 0)
    ex = lambda i, be, nu: (be[jnp.minimum(i, nu[0] - 1)], 0, 0)
    return pl.pallas_call(
        _expert_kernel,
        out_shape=jax.ShapeDtypeStruct((rows, d), F32),
        grid_spec=pltpu.PrefetchScalarGridSpec(
            num_scalar_prefetch=2,
            grid=(n_blocks,),
            in_specs=[pl.BlockSpec((MOE_BLOCK, d), blk),
                      pl.BlockSpec((None, d, 2 * de), ex),
                      pl.BlockSpec((None, 1, 2 * de), ex),
                      pl.BlockSpec((None, de, d), ex),
                      pl.BlockSpec((None, 1, d), ex)],
            out_specs=pl.BlockSpec((MOE_BLOCK, d), lambda i, be, nu: (i, 0)),
            scratch_shapes=[pltpu.VMEM((d, 2 * de), BF16), pltpu.VMEM((de, d), BF16)]),
        compiler_params=_cparams(("arbitrary",)),
        name="expert_blocks",
    )(block_e, n_used, xs, w1, b1, w2, b2)


def _combine_kernel(pc_ref, loc_ref, goff_ref, yb_ref, rt_ref, rtt_ref, x_ref, gate_ref, gfin_ref, o_ref,
                    yloc_ref, sems, *, tile0, final_norm):
    step = pl.program_id(0)
    tile = step + tile0
    slot = step % 2

    def gather(tl, sl, wait):
        make = lambda loc, glob, rows: pltpu.make_async_copy(
            yb_ref.at[pl.ds(glob, rows)], yloc_ref.at[sl, pl.ds(loc, rows)], sems.at[sl])
        _piece_copies(pc_ref, loc_ref, goff_ref, tl * N_EXPERTS, make, wait=wait)

    @pl.when(step == 0)
    def _():
        yloc_ref[...] = jnp.zeros_like(yloc_ref)
        gather(tile, slot, False)

    @pl.when(step + 1 < pl.num_programs(0))
    def _():
        gather(tile + 1, 1 - slot, False)

    gather(tile, slot, True)

    rt = rt_ref[...]
    rtt = rtt_ref[...]
    tm = rt.shape[0]
    acc = jnp.zeros(x_ref.shape, F32)
    for c in range(MOE_LOCAL // MOE_CHUNK):
        hit = _local_onehot_rows(rtt, c)
        wgt = [rtt[2 * TOP_K + k:2 * TOP_K + k + 1] for k in range(TOP_K)]
        pw = jnp.where(hit[0], wgt[0], jnp.where(hit[1], wgt[1], jnp.where(hit[2], wgt[2],
                                                                           jnp.where(hit[3], wgt[3], 0.0))))
        ys = yloc_ref[slot, c * MOE_CHUNK:(c + 1) * MOE_CHUNK, :] * jnp.sum(pw, axis=-1, keepdims=True)
        cols = (lax.broadcasted_iota(jnp.int32, (tm, MOE_CHUNK), 1) + c * MOE_CHUNK).astype(F32)
        own = [cols == rt[:, TOP_K + k:TOP_K + k + 1] for k in range(TOP_K)]
        pt = jnp.where(own[0], 1.0, jnp.where(own[1], 1.0, jnp.where(own[2], 1.0,
                                                                     jnp.where(own[3], 1.0, 0.0)))).astype(BF16)
        acc = acc + jnp.dot(pt, ys.astype(BF16), preferred_element_type=F32)

    nseg = gate_ref.shape[0]
    seg = tm // nseg
    for r in range(nseg):
        rows = slice(r * seg, (r + 1) * seg)
        x_new = x_ref[rows, :] + gate_ref[r] * acc[rows]
        if final_norm:
            x_new = x_new * lax.rsqrt(jnp.mean(x_new * x_new, axis=-1, keepdims=True) + EPS) * gfin_ref[...]
        o_ref[rows, :] = x_new


def combine(yb, rt, rtt, pc, loc, goff, row0, x, mod, g_final, *, final_norm):
    b, t, d = x.shape
    tm = MOE_TILE
    assert row0 % tm == 0 and (b * t) % tm == 0 and (t % tm == 0 or tm % t == 0)
    tile0 = row0 // tm
    nseg = max(tm // t, 1)
    tiles_per_b = max(t // tm, 1)
    gate = mod[:, 5:6, :]
    out = pl.pallas_call(
        functools.partial(_combine_kernel, tile0=tile0, final_norm=final_norm),
        out_shape=jax.ShapeDtypeStruct((b * t, d), F32),
        grid_spec=pltpu.PrefetchScalarGridSpec(
            num_scalar_prefetch=3,
            grid=(b * t // tm,),
            in_specs=[pl.BlockSpec(memory_space=pl.ANY),
                      pl.BlockSpec((tm, 128), lambda i, *_: (i + tile0, 0)),
                      pl.BlockSpec((None, 16, tm), lambda i, *_: (i + tile0, 0, 0)),
                      pl.BlockSpec((tm, d), lambda i, *_: (i, 0)),
                      pl.BlockSpec((nseg, 1, d), lambda i, *_: (i // tiles_per_b, 0, 0)),
                      pl.BlockSpec((1, d), lambda i, *_: (0, 0))],
            out_specs=pl.BlockSpec((tm, d), lambda i, *_: (i, 0)),
            scratch_shapes=[pltpu.VMEM((2, MOE_LOCAL, d), F32), pltpu.SemaphoreType.DMA((2,))]),
        compiler_params=_cparams(("arbitrary",)),
        name="combine",
    )(pc, loc, goff, yb, rt, rtt, x.reshape(b * t, d), gate, g_final)
    return out.reshape(b, t, d)


def _pad_w_in(w_in):
    d = w_in.shape[0]
    n_ba = 4 * DN_HEADS
    pad = jnp.zeros((d, COL_U - COL_BA - n_ba), w_in.dtype)
    return jnp.concatenate([w_in[:, :COL_BA + n_ba], pad, w_in[:, COL_BA + n_ba:]], axis=1).astype(BF16)


def kernel(x, c, ctx, c_ctx, w_ada, b_ada, g_mix, g_ffn, w_in, lru_conv_w, lru_conv_b, lru_w_gate, lru_b_gate, lru_lam, dn_conv_w, dn_a_log, dn_dt_bias, dn_norm_g, s5_lam_re, s5_lam_im, s5_log_dt, s5_b_re, s5_b_im, s5_c_re, s5_c_im, s5_d, s5_w_glu, s5_b_glu, w_br_a, w_br_b, w_br_c, w_out, w_router, b_router, w_e1, b_e1, w_e2, b_e2, g_final):
    bsz, t, d = x.shape
    n_layers = w_in.shape[0]
    s = jnp.concatenate([c, jnp.broadcast_to(c_ctx[None], (SUBLANES, d))], axis=0)
    mods = ada_modulation(s, w_ada.astype(BF16), b_ada)
    x_lat, x_ctx = x, ctx
    for l in range(n_layers):
        last = l == n_layers - 1
        p = _layer_params(l, g_mix, g_ffn, w_in, lru_conv_w, lru_conv_b, lru_w_gate, lru_b_gate, lru_lam, dn_conv_w,
                          dn_a_log, dn_dt_bias, dn_norm_g, s5_lam_re, s5_lam_im, s5_log_dt, s5_b_re, s5_b_im,
                          s5_c_re, s5_c_im, s5_d, s5_w_glu, s5_b_glu, w_br_a, w_br_b, w_br_c, w_out, w_router,
                          b_router)
        m_lat = mods[l, :bsz].reshape(bsz, 6, d)
        m_ctx = jnp.broadcast_to(mods[l, bsz].reshape(1, 6, d), (bsz, 6, d))
        ctx_out, ctx_states = mix_stream(x_ctx, m_ctx, p, None, x_ctx.shape[1], not last)
        lat_out, _ = mix_stream(x_lat, m_lat, p, ctx_states, GRID_W, True)
        w1, b1 = w_e1[l], b_e1[l][:, None, :]
        w2, b2 = w_e2[l], b_e2[l][:, None, :]
        if last:
            x_mid, h, lg = lat_out
            routed = moe(h.reshape(-1, d), lg.reshape(-1, 128), w1, b1, w2, b2)
            x_lat = combine(*routed, 0, x_mid, m_lat, g_final[None], final_norm=True)
        else:
            xc_mid, hc, lgc = ctx_out
            xl_mid, hl, lgl = lat_out
            n_ctx = bsz * x_ctx.shape[1]
            routed = moe(jnp.concatenate([hc.reshape(-1, d), hl.reshape(-1, d)], axis=0),
                         jnp.concatenate([lgc.reshape(-1, 128), lgl.reshape(-1, 128)], axis=0), w1, b1, w2, b2)
            x_ctx = combine(*routed, 0, xc_mid, m_ctx, g_final[None], final_norm=False)
            x_lat = combine(*routed, n_ctx, xl_mid, m_lat, g_final[None], final_norm=False)
    return x_lat


def _layer_params(l, g_mix, g_ffn, w_in, lru_conv_w, lru_conv_b, lru_w_gate, lru_b_gate, lru_lam, dn_conv_w,
                  dn_a_log, dn_dt_bias, dn_norm_g, s5_lam_re, s5_lam_im, s5_log_dt, s5_b_re, s5_b_im, s5_c_re,
                  s5_c_im, s5_d, s5_w_glu, s5_b_glu, w_br_a, w_br_b, w_br_c, w_out, w_router, b_router):
    s5 = [s5_lam_re[l], s5_lam_im[l], s5_log_dt[l], s5_b_re[l], s5_b_im[l], s5_c_re[l], s5_c_im[l]]
    wr = jnp.zeros((w_router.shape[1], 128), F32).at[:, :N_EXPERTS].set(w_router[l])
    wr1 = wr.astype(BF16)
    return {
        "g_mix": g_mix[l][None], "w_in": _pad_w_in(w_in[l]),
        "lru_conv_w": lru_conv_w[l], "lru_conv_b": lru_conv_b[l][None],
        "lru_wg": [_lru_gate_dense(lru_w_gate[l, dr]) for dr in range(2)],
        "lru_bg": lru_b_gate[l].reshape(2, 1, 2 * LRU_WIDTH), "lru_lam": lru_lam[l][:, None, :],
        "dn_conv_w": dn_conv_w[l], "dn_a_log": _dn_lane_vec(dn_a_log[l]), "dn_dt_bias": _dn_lane_vec(dn_dt_bias[l]),
        "s5": [_s5_params(*(a[dr] for a in s5)) for dr in range(2)],
        "dn_norm_g": dn_norm_g[l][None], "s5_d": s5_d[l][None], "s5_w_glu": s5_w_glu[l].astype(BF16),
        "s5_b_glu": s5_b_glu[l][None], "w_br_a": w_br_a[l].astype(BF16), "w_br_b": w_br_b[l].astype(BF16),
        "w_br_c": w_br_c[l].astype(BF16), "w_out": w_out[l].astype(BF16), "g_ffn": g_ffn[l][None],
        "w_r1": wr1, "w_r2": (wr - wr1.astype(F32)).astype(BF16),
        "b_router": jnp.full((1, 128), -1e30, F32).at[0, :N_EXPERTS].set(b_router[l]),
    }


def mix_stream(x, mod, p, init, seg, emit):
    bsz = x.shape[0]
    if init is None:
        zl = jnp.zeros((bsz, LRU_WIDTH), F32)
        zd = jnp.zeros((bsz, DN_HEADS, DN_HEAD_DIM, DN_HEAD_DIM), F32)
        zs = jnp.zeros((bsz, S5_LANES), F32)
        init = ((zl, zl), (zd, zd), ((zs, zs), (zs, zs)))
    proj = inproj(x, mod, p["g_mix"], p["w_in"], p["dn_conv_w"], seg)

    lru = lambda dr, prev: lru_scan(proj, p["lru_conv_w"], p["lru_conv_b"], p["lru_wg"][dr], p["lru_bg"][dr],
                                    p["lru_lam"][dr], init[0][dr], prev, seg=seg, reverse=dr == 1)
    h_f, lru_f = lru(0, None)
    h_sum, lru_b = lru(1, h_f)

    dn = lambda dr, prev: dn_scan(proj, p["dn_a_log"], p["dn_dt_bias"], init[1][dr], prev, reverse=dr == 1)
    o_f, dn_f = dn(0, None)
    o_sum, dn_b = dn(1, o_f)

    s5 = lambda dr, prev: s5_scan(proj, p["s5"][dr], init[2][dr][0], init[2][dr][1], prev, reverse=dr == 1)
    y_f, s5_fr, s5_fi = s5(0, None)
    y_sum, s5_br, s5_bi = s5(1, y_f)

    states = ((lru_f, lru_b), (dn_f, dn_b), ((s5_fr, s5_fi), (s5_br, s5_bi)))
    if not emit:
        return None, states
    return merge_stream(h_sum, proj, o_sum, y_sum, x, mod, p), states


def moe(h, logits, w1, b1, w2, b2):
    n, d = h.shape
    nt = n // MOE_TILE
    rt, rtt, counts = route(logits)
    cnt = counts[:, 0, :N_EXPERTS].astype(jnp.int32)
    pc = (cnt + MOE_PIECE - 1) // MOE_PIECE
    loc = MOE_PIECE * (jnp.cumsum(pc, axis=1) - pc)
    rows_e = MOE_PIECE * jnp.sum(pc, axis=0)
    padded = ((rows_e + MOE_BLOCK - 1) // MOE_BLOCK) * MOE_BLOCK
    ends_pad = jnp.cumsum(padded)
    goff = (ends_pad - padded)[None, :] + MOE_PIECE * (jnp.cumsum(pc, axis=0) - pc)
    n_blocks = -(-(n * TOP_K + (MOE_PIECE - 1) * N_EXPERTS * nt) // MOE_BLOCK) + N_EXPERTS
    block_start = jnp.arange(n_blocks, dtype=jnp.int32) * MOE_BLOCK
    block_e = jnp.minimum(jnp.sum(ends_pad[None, :] <= block_start[:, None], axis=1), N_EXPERTS - 1).astype(jnp.int32)
    n_used = (ends_pad[-1:] // MOE_BLOCK).astype(jnp.int32)
    region_end = ends_pad.at[N_EXPERTS - 1].set(n_blocks * MOE_BLOCK)
    pc = jnp.concatenate([pc, ((region_end - (ends_pad - padded + rows_e)) // MOE_PIECE)[None, :]], axis=0)
    loc = jnp.concatenate([loc, jnp.zeros((1, N_EXPERTS), loc.dtype)], axis=0)
    goff = jnp.concatenate([goff, (ends_pad - padded + rows_e)[None, :]], axis=0)
    pc, loc, goff = (a.reshape(-1).astype(jnp.int32) for a in (pc, loc, goff))
    xs = dispatch(h, rtt, pc, loc, goff, n_blocks * MOE_BLOCK)
    yb = expert_blocks(xs, block_e, n_used, w1, b1, w2, b2)
    return yb, rt, rtt, pc, loc, goff
```

```python
import functools
import math

import jax
import jax.numpy as jnp
from jax import lax
from jax.experimental import pallas as pl
from jax.experimental.pallas import tpu as pltpu

F32 = jnp.float32
BF16 = jnp.bfloat16

D_MODEL = 1024
EPS = 1e-6
CONV_W = 4
CONV_LEFT = CONV_W // 2
GRID_W = 64

LRU_WIDTH = 512
LRU_BLOCKS = 8
LRU_BLOCK = LRU_WIDTH // LRU_BLOCKS
LRU_C = 8.0

DN_HEAD_DIM = 128
DN_HEADS = 8
DN_WIDTH = DN_HEADS * DN_HEAD_DIM
DN_CHUNK = 64
DN_BATCH = 4
DN_ROWS = 128

S5_WIDTH = 512
S5_GROUP = 16
S5_GROUPS = 32
S5_STATE = 64
S5_LANES = S5_GROUPS * S5_STATE
S5_SPLIT = 4

N_EXPERTS = 32
TOP_K = 4
D_EXPERT = 1024
SWIGLU_LIMIT = 7.0
SWIGLU_ALPHA = 1.702
MOE_BLOCK = 512
MOE_TILE = 512
MOE_PIECE = 16
MOE_LOCAL = MOE_TILE * TOP_K + MOE_PIECE * N_EXPERTS
MOE_CHUNK = 256

COL_AX, COL_AY, COL_Q, COL_K, COL_V, COL_Z = 0, 512, 1024, 2048, 3072, 4096
COL_BA, COL_U, COL_GATE, D_IN_PAD = 5120, 5632, 6144, 9216
D_IN = 8736

SUBLANES = 8
LANES = 128
VMEM_LIMIT = 52 * 1024 * 1024


def _cparams(sem):
    return pltpu.CompilerParams(dimension_semantics=sem, vmem_limit_bytes=VMEM_LIMIT)


def _sigmoid(x):
    return 1.0 / (1.0 + jnp.exp(-x))


def _softplus(x):
    return jnp.maximum(x, 0.0) + jnp.log1p(jnp.exp(-jnp.abs(x)))


def _silu(x):
    return x * _sigmoid(x)


def _gelu_tanh(x):
    return 0.5 * x * (1.0 + jnp.tanh(math.sqrt(2.0 / math.pi) * (x + 0.044715 * x * x * x)))


def _dot(a, b):
    return jnp.dot(a.astype(BF16), b.astype(BF16), preferred_element_type=F32)


def _ada_kernel(s_ref, w_ref, b_ref, o_ref):
    o_ref[...] = _dot(_silu(s_ref[...]), w_ref[...]) + b_ref[...]


def ada_modulation(s, w_ada, b_ada):
    n_layers, d, n = w_ada.shape
    r = s.shape[0]
    tn = 1536
    return pl.pallas_call(
        _ada_kernel,
        out_shape=jax.ShapeDtypeStruct((n_layers, r, n), F32),
        grid=(n_layers, n // tn),
        in_specs=[pl.BlockSpec((r, d), lambda l, j: (0, 0)),
                  pl.BlockSpec((None, d, tn), lambda l, j: (l, 0, j)),
                  pl.BlockSpec((None, 1, tn), lambda l, j: (l, 0, j))],
        out_specs=pl.BlockSpec((None, r, tn), lambda l, j: (l, 0, j)),
        compiler_params=_cparams(("arbitrary", "arbitrary")),
        name="ada_modulation",
    )(s, w_ada, b_ada.reshape(n_layers, 1, n))


def _norm_mod(x, g, shift, scale):
    y = x * lax.rsqrt(jnp.mean(x * x, axis=-1, keepdims=True) + EPS)
    return (y * g) * (1.0 + scale) + shift


def _conv_rows(x, w, seg):
    tc = x.shape[0]
    t_in_seg = lax.broadcasted_iota(jnp.int32, x.shape, 0) % seg
    acc = x * w[CONV_LEFT:CONV_LEFT + 1]
    for j in range(CONV_W):
        off = j - CONV_LEFT
        if off == 0:
            continue
        sh = pltpu.roll(x, (-off) % tc, 0)
        ok = (t_in_seg >= -off) if off < 0 else (t_in_seg < seg - off)
        acc = acc + jnp.where(ok, sh, 0.0) * w[j:j + 1]
    return acc


def _inproj_kernel(x_ref, mod_ref, g_ref, w_ref, cw_ref, o_ref, h_ref, *, seg, qkv_tiles):
    j = pl.program_id(2)

    @pl.when(j == 0)
    def _():
        mod = mod_ref[...]
        h_ref[...] = _norm_mod(x_ref[...], g_ref[...], mod[0:1], mod[1:2]).astype(BF16)

    is_qkv = jnp.logical_and(j >= qkv_tiles[0], j < qkv_tiles[1])

    @pl.when(is_qkv)
    def _():
        y = jnp.dot(h_ref[...], w_ref[...], preferred_element_type=F32)
        o_ref[...] = _silu(_conv_rows(y, cw_ref[...], seg))

    @pl.when(jnp.logical_not(is_qkv))
    def _():
        o_ref[...] = jnp.dot(h_ref[...], w_ref[...], preferred_element_type=F32)


def inproj(x, mod, g, w_pad, dn_conv_w, seg):
    b, t, d = x.shape
    n = w_pad.shape[1]
    tm = min(t, 1024)
    tn = DN_WIDTH
    assert tm % seg == 0 and t % tm == 0 and COL_Q % tn == 0
    q0, q1 = COL_Q // tn, COL_Z // tn
    return pl.pallas_call(
        functools.partial(_inproj_kernel, seg=seg, qkv_tiles=(q0, q1)),
        out_shape=jax.ShapeDtypeStruct((b, t, n), F32),
        grid=(b, t // tm, n // tn),
        in_specs=[pl.BlockSpec((None, tm, d), lambda i, r, j: (i, r, 0)),
                  pl.BlockSpec((None, 6, d), lambda i, r, j: (i, 0, 0)),
                  pl.BlockSpec((1, d), lambda i, r, j: (0, 0)),
                  pl.BlockSpec((d, tn), lambda i, r, j: (0, j)),
                  pl.BlockSpec((CONV_W, tn), lambda i, r, j: (0, jnp.clip(j - q0, 0, q1 - q0 - 1)))],
        out_specs=pl.BlockSpec((None, tm, tn), lambda i, r, j: (i, r, j)),
        scratch_shapes=[pltpu.VMEM((tm, d), BF16)],
        compiler_params=_cparams(("arbitrary", "arbitrary", "arbitrary")),
        name="inproj",
    )(x, mod, g, w_pad, dn_conv_w)


def _conv_time_major(x, w, seg):
    tc = x.shape[0]
    t_in_seg = lax.broadcasted_iota(jnp.int32, x.shape, 0) % seg
    acc = x * w[CONV_LEFT:CONV_LEFT + 1][None]
    for j in range(CONV_W):
        off = j - CONV_LEFT
        if off == 0:
            continue
        if off < 0:
            sh = jnp.concatenate([jnp.zeros((-off,) + x.shape[1:], x.dtype), x[:tc + off]], axis=0)
            ok = t_in_seg >= -off
        else:
            sh = jnp.concatenate([x[off:], jnp.zeros((off,) + x.shape[1:], x.dtype)], axis=0)
            ok = t_in_seg < seg - off
        acc = acc + jnp.where(ok, sh, 0.0) * w[j:j + 1][None]
    return acc


def _to_time_major(x_ref, xt_ref):
    bc, tc, w = x_ref.shape
    for j in range(w // LANES):
        for b in range(bc):
            xt_ref[j, pl.ds(b, tc, stride=bc), :] = x_ref[b, :, j * LANES:(j + 1) * LANES]


def _from_time_major(xt_ref, o_ref, prev_ref):
    bc, tc, w = o_ref.shape
    for j in range(w // LANES):
        for b in range(bc):
            v = xt_ref[j, pl.ds(b, tc, stride=bc), :]
            if prev_ref is not None:
                v = v + prev_ref[b, :, j * LANES:(j + 1) * LANES]
            o_ref[b, :, j * LANES:(j + 1) * LANES] = v


def _lru_kernel(x_ref, cw_ref, cb_ref, wg_ref, bg_ref, lam_ref, h0_ref, *rest, seg, reverse, add_prev):
    if add_prev:
        prev_ref, o_ref, hfin_ref, xt_ref, a_ref, b_ref, h_ref = rest
    else:
        prev_ref = None
        o_ref, hfin_ref, xt_ref, a_ref, b_ref, h_ref = rest
    bc, tc, w = x_ref.shape
    step = pl.program_id(1)

    @pl.when(step == 0)
    def _():
        h_ref[...] = h0_ref[...]

    _to_time_major(x_ref, xt_ref)
    x = jnp.concatenate([xt_ref[j] for j in range(w // LANES)], axis=-1).reshape(tc, bc, w)
    xc = _conv_time_major(x, cw_ref[...], seg) + cb_ref[...][None]
    gates = _dot(xc.reshape(tc * bc, w), wg_ref[...]) + bg_ref[...]
    r = _sigmoid(gates[:, :w]).reshape(tc, bc, w)
    i = _sigmoid(gates[:, w:]).reshape(tc, bc, w)
    log_a = (-LRU_C * _softplus(-lam_ref[...]))[None] * r
    a_ref[...] = jnp.exp(log_a)
    b_ref[...] = jnp.sqrt(1.0 - jnp.exp(2.0 * log_a)) * (i * xc)

    def body(s, h):
        t = tc - 1 - s if reverse else s
        h = a_ref[t] * h + b_ref[t]
        rows = pl.ds(pl.multiple_of(t * bc, bc), bc)
        for j in range(w // LANES):
            xt_ref[j, rows, :] = h[:, j * LANES:(j + 1) * LANES]
        return h

    h = lax.fori_loop(0, tc, body, h_ref[...], unroll=8)
    h_ref[...] = h
    hfin_ref[...] = h
    _from_time_major(xt_ref, o_ref, prev_ref)


def lru_scan(proj, conv_w, conv_b, wg, bg, lam, h0, prev, *, seg, reverse):
    b, t, _ = proj.shape
    w = LRU_WIDTH
    tc = min(t, 256)
    bc = SUBLANES
    assert tc % seg == 0 and t % tc == 0 and b % bc == 0
    nt = t // tc
    tix = (lambda s: nt - 1 - s) if reverse else (lambda s: s)
    blk = lambda cb: pl.BlockSpec((bc, tc, w), lambda i, s: (i, tix(s), cb))
    const = lambda shape: pl.BlockSpec(shape, lambda i, s: (0,) * len(shape))
    in_specs = [blk(COL_AX // w), const((CONV_W, w)), const((1, w)), const((w, 2 * w)), const((1, 2 * w)),
                const((1, w)), pl.BlockSpec((bc, w), lambda i, s: (i, 0))]
    args = [proj, conv_w, conv_b, wg, bg, lam, h0]
    if prev is not None:
        in_specs.append(blk(0))
        args.append(prev)
    return pl.pallas_call(
        functools.partial(_lru_kernel, seg=seg, reverse=reverse, add_prev=prev is not None),
        out_shape=(jax.ShapeDtypeStruct((b, t, w), F32), jax.ShapeDtypeStruct((b, w), F32)),
        grid=(b // bc, nt),
        in_specs=in_specs,
        out_specs=(blk(0), pl.BlockSpec((bc, w), lambda i, s: (i, 0))),
        scratch_shapes=[pltpu.VMEM((w // LANES, tc * bc, LANES), F32), pltpu.VMEM((tc, bc, w), F32),
                        pltpu.VMEM((tc, bc, w), F32), pltpu.VMEM((bc, w), F32)],
        compiler_params=_cparams(("arbitrary", "arbitrary")),
        name="lru_scan_bwd" if reverse else "lru_scan_fwd",
    )(*args)


def _lru_gate_dense(w_gate):
    eye = jnp.eye(LRU_BLOCKS, dtype=w_gate.dtype)
    dense = jnp.einsum("gnij,nm->gnimj", w_gate, eye).reshape(2, LRU_WIDTH, LRU_WIDTH)
    return jnp.concatenate([dense[0], dense[1]], axis=1).astype(BF16)


def _s5_kernel(u_ref, bre_ref, bim_ref, cre_ref, cim_ref, ar_ref, ai_ref, h0r_ref, h0i_ref, *rest,
               reverse, add_prev):
    if add_prev:
        prev_ref, o_ref, fr_ref, fi_ref, ut_ref, xr_ref, xi_ref, hr_ref, hi_ref = rest
    else:
        prev_ref = None
        o_ref, fr_ref, fi_ref, ut_ref, xr_ref, xi_ref, hr_ref, hi_ref = rest
    bc, tc, w = u_ref.shape
    blk_in = w // S5_SPLIT
    blk_st = S5_LANES // S5_SPLIT
    assert blk_in == LANES

    @pl.when(pl.program_id(1) == 0)
    def _():
        hr_ref[...] = h0r_ref[...]
        hi_ref[...] = h0i_ref[...]

    _to_time_major(u_ref, ut_ref)
    for j in range(S5_SPLIT):
        uj = ut_ref[j].astype(BF16)
        xr_ref[:, :, j * blk_st:(j + 1) * blk_st] = jnp.dot(
            uj, bre_ref[j], preferred_element_type=F32).reshape(tc, bc, blk_st)
        xi_ref[:, :, j * blk_st:(j + 1) * blk_st] = jnp.dot(
            uj, bim_ref[j], preferred_element_type=F32).reshape(tc, bc, blk_st)

    ar = jnp.broadcast_to(ar_ref[...], (bc, S5_LANES))
    ai = jnp.broadcast_to(ai_ref[...], (bc, S5_LANES))

    def body(s, carry):
        hr, hi = carry
        t = tc - 1 - s if reverse else s
        nr = ar * hr - ai * hi + xr_ref[t]
        ni = ar * hi + ai * hr + xi_ref[t]
        xr_ref[t] = nr
        xi_ref[t] = ni
        return nr, ni

    hr, hi = lax.fori_loop(0, tc, body, (hr_ref[...], hi_ref[...]))
    hr_ref[...] = hr
    hi_ref[...] = hi
    fr_ref[...] = hr
    fi_ref[...] = hi

    for j in range(S5_SPLIT):
        hrj = xr_ref[:, :, j * blk_st:(j + 1) * blk_st].reshape(tc * bc, blk_st).astype(BF16)
        hij = xi_ref[:, :, j * blk_st:(j + 1) * blk_st].reshape(tc * bc, blk_st).astype(BF16)
        ut_ref[j] = (jnp.dot(hrj, cre_ref[j], preferred_element_type=F32)
                     - jnp.dot(hij, cim_ref[j], preferred_element_type=F32))
    _from_time_major(ut_ref, o_ref, prev_ref)


def s5_scan(proj, prm, h0r, h0i, prev, *, reverse):
    b, t, _ = proj.shape
    w = S5_WIDTH
    tc = min(t, 64)
    bc = SUBLANES
    assert t % tc == 0 and b % bc == 0
    nt = t // tc
    tix = (lambda s: nt - 1 - s) if reverse else (lambda s: s)
    blk = lambda cb: pl.BlockSpec((bc, tc, w), lambda i, s: (i, tix(s), cb))
    const = lambda shape: pl.BlockSpec(shape, lambda i, s: (0,) * len(shape))
    st = pl.BlockSpec((bc, S5_LANES), lambda i, s: (i, 0))
    bre, bim, cre, cim, ar, ai = prm
    in_specs = [blk(COL_U // w), const(bre.shape), const(bim.shape), const(cre.shape), const(cim.shape),
                const(ar.shape), const(ai.shape), st, st]
    args = [proj, bre, bim, cre, cim, ar, ai, h0r, h0i]
    if prev is not None:
        in_specs.append(blk(0))
        args.append(prev)
    st_shape = jax.ShapeDtypeStruct((b, S5_LANES), F32)
    return pl.pallas_call(
        functools.partial(_s5_kernel, reverse=reverse, add_prev=prev is not None),
        out_shape=(jax.ShapeDtypeStruct((b, t, w), F32), st_shape, st_shape),
        grid=(b // bc, nt),
        in_specs=in_specs,
        out_specs=(blk(0), st, st),
        scratch_shapes=[pltpu.VMEM((w // LANES, tc * bc, LANES), F32),
                        pltpu.VMEM((tc, bc, S5_LANES), F32), pltpu.VMEM((tc, bc, S5_LANES), F32),
                        pltpu.VMEM((bc, S5_LANES), F32), pltpu.VMEM((bc, S5_LANES), F32)],
        compiler_params=_cparams(("arbitrary", "arbitrary")),
        name="s5_scan_bwd" if reverse else "s5_scan_fwd",
    )(*args)


def _s5_params(lam_re, lam_im, log_dt, b_re, b_im, c_re, c_im):
    dt = jnp.exp(log_dt)[:, None]
    mag = jnp.exp(lam_re * dt)
    ar, ai = mag * jnp.cos(lam_im * dt), mag * jnp.sin(lam_im * dt)
    den = lam_re * lam_re + lam_im * lam_im
    fr = ((ar - 1.0) * lam_re + ai * lam_im) / den
    fi = (ai * lam_re - (ar - 1.0) * lam_im) / den
    bb_re = fr[..., None] * b_re - fi[..., None] * b_im
    bb_im = fr[..., None] * b_im + fi[..., None] * b_re
    gl = S5_GROUPS // S5_SPLIT
    eye = jnp.eye(gl, dtype=F32)

    def pack_in(bb):
        bb = jnp.transpose(bb, (0, 2, 1)).reshape(S5_SPLIT, gl, S5_GROUP, S5_STATE)
        return jnp.einsum("jghp,gm->jghmp", bb, eye).reshape(S5_SPLIT, gl * S5_GROUP, gl * S5_STATE).astype(BF16)

    def pack_out(c):
        c = jnp.transpose(c, (0, 2, 1)).reshape(S5_SPLIT, gl, S5_STATE, S5_GROUP)
        return jnp.einsum("jgph,gm->jgpmh", c, eye).reshape(S5_SPLIT, gl * S5_STATE, gl * S5_GROUP).astype(BF16)

    return (pack_in(bb_re), pack_in(bb_im), pack_out(c_re), pack_out(c_im),
            ar.reshape(1, S5_LANES), ai.reshape(1, S5_LANES))


def _l2norm(x):
    return x * lax.rsqrt(jnp.sum(x * x, axis=-1, keepdims=True) + EPS)


def _dot_nt(a, b):
    return lax.dot_general(a.astype(BF16), b.astype(BF16), (((1,), (1,)), ((), ())), preferred_element_type=F32)


def _dot_tn(a, b):
    return lax.dot_general(a.astype(BF16), b.astype(BF16), (((0,), (0,)), ((), ())), preferred_element_type=F32)


def _split3_dot(m, x):
    x1 = x.astype(BF16)
    r1 = x - x1.astype(F32)
    x2 = r1.astype(BF16)
    x3 = (r1 - x2.astype(F32)).astype(BF16)
    mb = m.astype(BF16)
    dot = lambda v: jnp.dot(mb, v, preferred_element_type=F32)
    return dot(x1) + dot(x2) + dot(x3)


def _dn_kernel(q_ref, k_ref, v_ref, ba_ref, alog_ref, dtb_ref, s0_ref, *rest, reverse, add_prev, dirn):
    if add_prev:
        prev_ref, o_ref, sfin_ref, beta_ref, g_ref, s_ref = rest
    else:
        o_ref, sfin_ref, beta_ref, g_ref, s_ref = rest
    nb, tc, wq = q_ref.shape
    n_chunks = tc // DN_CHUNK
    step = pl.program_id(1)

    @pl.when(step == 0)
    def _():
        s_ref[...] = s0_ref[...]

    ba = ba_ref[...].reshape(nb * tc, ba_ref.shape[-1])
    beta_ref[...] = _sigmoid(ba)
    g_ref[...] = -jnp.exp(alog_ref[...]) * _softplus(ba + dtb_ref[...])

    ii = lax.broadcasted_iota(jnp.int32, (DN_CHUNK, DN_CHUNK), 0)
    jj = lax.broadcasted_iota(jnp.int32, (DN_CHUNK, DN_CHUNK), 1)
    incl = (ii <= jj) if reverse else (ii >= jj)
    strict = (ii < jj) if reverse else (ii > jj)
    eye = (ii == jj).astype(F32)
    tri = incl.astype(F32)
    last = 0 if reverse else DN_CHUNK - 1
    neg = jnp.float32(-1e30)

    def chunk(ci, carry):
        cidx = n_chunks - 1 - ci if reverse else ci
        r0 = pl.multiple_of(cidx * DN_CHUNK, DN_CHUNK)
        rows = pl.ds(r0, DN_CHUNK)
        each = lambda f, *cols: [f(*args) for args in zip(*cols)]
        srows = [pl.ds(pl.multiple_of(bb * tc + r0, DN_CHUNK), DN_CHUNK) for bb in range(nb)]
        g_cum = [_split3_dot(tri, g_ref[sr, :]) for sr in srows]
        g_t = [g.T for g in g_cum]
        e_g = [jnp.exp(g) for g in g_cum]
        e_rest = [jnp.exp(g[last:last + 1] - g) for g in g_cum]
        beta_all = [beta_ref[sr, :] for sr in srows]
        pairs = [(bb, h) for bb in range(nb) for h in range(DN_HEADS)]
        lanes = [slice(h * DN_HEAD_DIM, (h + 1) * DN_HEAD_DIM) for _, h in pairs]
        cb = [dirn * DN_HEADS + h for _, h in pairs]
        cg = [2 * DN_HEADS + c for c in cb]
        q = [_l2norm(q_ref[bb, rows, ln]) * (DN_HEAD_DIM ** -0.5) for (bb, _), ln in zip(pairs, lanes)]
        k = [_l2norm(k_ref[bb, rows, ln]) for (bb, _), ln in zip(pairs, lanes)]
        v = [v_ref[bb, rows, ln] for (bb, _), ln in zip(pairs, lanes)]
        beta = [beta_all[bb][:, c:c + 1] for (bb, _), c in zip(pairs, cb)]
        decay = [jnp.exp(jnp.where(incl, g_cum[bb][:, c:c + 1] - g_t[bb][c:c + 1, :], neg))
                 for (bb, _), c in zip(pairs, cg)]
        nc = DN_CHUNK
        kb = each(lambda a, b: a * b, k, beta)
        kq = each(lambda a, b, c: _dot_nt(jnp.concatenate([a, b], axis=0), c), kb, q, k)
        a_low = each(lambda r, dc: jnp.where(strict, r[:nc] * dc, 0.0), kq, decay)
        attn = each(lambda r, dc: jnp.where(incl, r[nc:] * dc, 0.0), kq, decay)
        x = [eye - a for a in a_low]
        p = [_dot(a, a) for a in a_low]
        for _ in range(4):
            xp = each(lambda a, b: _dot(jnp.concatenate([a, b], axis=0), b), x, p)
            x = each(lambda a, r: a + r[:nc], x, xp)
            p = [r[nc:] for r in xp]
        x = each(lambda a, b: a + _dot(a, b), x, p)
        e_col = [e_g[bb][:, c:c + 1] for (bb, _), c in zip(pairs, cg)]
        uw = each(lambda a, vv, bt, kk, ec: _dot(a, jnp.concatenate([vv * bt, kk * ec], axis=1)),
                  x, v, beta, kb, e_col)
        q_dec = each(lambda a, b: a * b, q, e_col)
        k_dec = [kk * e_rest[bb][:, c:c + 1] for kk, (bb, _), c in zip(k, pairs, cg)]
        e_tot = [e_g[bb][last:last + 1, c:c + 1] for (bb, _), c in zip(pairs, cg)]
        s = [s_ref[bb, h] for bb, h in pairs]
        ws = each(lambda r, qd, ss: _dot(jnp.concatenate([r[:, DN_HEAD_DIM:], qd], axis=0), ss), uw, q_dec, s)
        v_new = each(lambda r, m: r[:, :DN_HEAD_DIM] - m[:nc], uw, ws)
        o = each(lambda m, at, vn: m[nc:] + _dot(at, vn), ws, attn, v_new)
        s_new = each(lambda ss, et, kd, vn: ss * et + _dot_tn(kd, vn), s, e_tot, k_dec, v_new)
        if add_prev:
            o = [oo + prev_ref[bb, rows, ln] for oo, (bb, _), ln in zip(o, pairs, lanes)]
        for (bb, h), ss in zip(pairs, s_new):
            s_ref[bb, h] = ss
        for (bb, _), ln, oo in zip(pairs, lanes, o):
            o_ref[bb, rows, ln] = oo
        return carry

    lax.fori_loop(0, n_chunks, chunk, 0)

    @pl.when(step == pl.num_programs(1) - 1)
    def _():
        sfin_ref[...] = s_ref[...]


def dn_scan(proj, a_log, dt_bias, s0, prev, *, reverse):
    b, t, _ = proj.shape
    tc = min(t, DN_ROWS)
    nb = DN_BATCH
    assert tc % DN_CHUNK == 0 and t % tc == 0 and b % nb == 0
    nt = t // tc
    dirn = 1 if reverse else 0
    tix = (lambda s: nt - 1 - s) if reverse else (lambda s: s)
    wq = DN_WIDTH
    col = lambda cb: pl.BlockSpec((nb, tc, wq), lambda i, s: (i, tix(s), cb))
    vec = pl.BlockSpec((1, 128), lambda i, s: (0, 0))
    st = pl.BlockSpec((nb, DN_HEADS, DN_HEAD_DIM, DN_HEAD_DIM), lambda i, s: (i, 0, 0, 0))
    oblk = pl.BlockSpec((nb, tc, wq), lambda i, s: (i, tix(s), 0))
    in_specs = [col(COL_Q // wq), col(COL_K // wq), col(COL_V // wq),
                pl.BlockSpec((nb, tc, 128), lambda i, s: (i, tix(s), COL_BA // 128)), vec, vec, st]
    args = [proj, proj, proj, proj, a_log, dt_bias, s0]
    if prev is not None:
        in_specs.append(oblk)
        args.append(prev)
    return pl.pallas_call(
        functools.partial(_dn_kernel, reverse=reverse, add_prev=prev is not None, dirn=dirn),
        out_shape=(jax.ShapeDtypeStruct((b, t, wq), F32),
                   jax.ShapeDtypeStruct((b, DN_HEADS, DN_HEAD_DIM, DN_HEAD_DIM), F32)),
        grid=(b // nb, nt),
        in_specs=in_specs,
        out_specs=(oblk, st),
        scratch_shapes=[pltpu.VMEM((nb * tc, 128), F32), pltpu.VMEM((nb * tc, 128), F32),
                        pltpu.VMEM((nb, DN_HEADS, DN_HEAD_DIM, DN_HEAD_DIM), F32)],
        compiler_params=_cparams(("arbitrary", "arbitrary")),
        name="dn_scan_bwd" if reverse else "dn_scan_fwd",
    )(*args)


def _dn_lane_vec(p):
    return jnp.zeros((1, 128), F32).at[0, 2 * DN_HEADS:4 * DN_HEADS].set(p.reshape(-1))


def _merge_kernel(hl_ref, ay_ref, od_ref, z_ref, ys_ref, u_ref, gt_ref, x_ref, mod_ref, ng_ref, sd_ref,
                  wglu_ref, bglu_ref, wa_ref, wb_ref, wc_ref, wo_ref, gf_ref, wr1_ref, wr2_ref, br_ref,
                  xo_ref, h_ref, lg_ref):
    d = x_ref.shape[-1]
    mod = mod_ref[...]
    y_a = hl_ref[...] * _gelu_tanh(ay_ref[...])
    acc = _sigmoid(gt_ref[:, 0:d]) * _dot(y_a, wa_ref[...])

    ng = ng_ref[...]
    heads = []
    for h in range(DN_HEADS):
        lanes = slice(h * DN_HEAD_DIM, (h + 1) * DN_HEAD_DIM)
        o = od_ref[:, lanes]
        o = o * lax.rsqrt(jnp.mean(o * o, axis=-1, keepdims=True) + EPS) * ng
        heads.append((o * _silu(z_ref[:, lanes])).astype(BF16))
    y_b = jnp.concatenate(heads, axis=-1)
    acc = acc + _sigmoid(gt_ref[:, d:2 * d]) * jnp.dot(y_b, wb_ref[...], preferred_element_type=F32)

    y_c = _gelu_tanh(ys_ref[...] + sd_ref[...] * u_ref[...])
    y_c = y_c * _sigmoid(_dot(y_c, wglu_ref[...]) + bglu_ref[...])
    acc = acc + _sigmoid(gt_ref[:, 2 * d:3 * d]) * _dot(y_c, wc_ref[...])

    x_new = x_ref[...] + mod[2:3] * _dot(acc, wo_ref[...])
    xo_ref[...] = x_new
    h = _norm_mod(x_new, gf_ref[...], mod[3:4], mod[4:5])
    h1 = h.astype(BF16)
    h2 = (h - h1.astype(F32)).astype(BF16)
    h_ref[...] = h1
    lg_ref[...] = (jnp.dot(h1, wr1_ref[...], preferred_element_type=F32)
                   + jnp.dot(h1, wr2_ref[...], preferred_element_type=F32)
                   + jnp.dot(h2, wr1_ref[...], preferred_element_type=F32)) + br_ref[...]


def merge_stream(h_lru, proj, o_dn, y_s5, x, mod, p):
    b, t, d = x.shape
    tm = min(t, 256)
    row = lambda w, cb: pl.BlockSpec((None, tm, w), lambda i, r: (i, r, cb))
    const = lambda a: pl.BlockSpec(a.shape, lambda i, r: (0,) * a.ndim)
    consts = [p["dn_norm_g"], p["s5_d"], p["s5_w_glu"], p["s5_b_glu"], p["w_br_a"], p["w_br_b"], p["w_br_c"],
              p["w_out"], p["g_ffn"], p["w_r1"], p["w_r2"], p["b_router"]]
    return pl.pallas_call(
        _merge_kernel,
        out_shape=(jax.ShapeDtypeStruct((b, t, d), F32), jax.ShapeDtypeStruct((b, t, d), BF16),
                   jax.ShapeDtypeStruct((b, t, 128), F32)),
        grid=(b, t // tm),
        in_specs=[row(LRU_WIDTH, 0), row(LRU_WIDTH, COL_AY // LRU_WIDTH), row(DN_WIDTH, 0),
                  row(DN_WIDTH, COL_Z // DN_WIDTH), row(S5_WIDTH, 0), row(S5_WIDTH, COL_U // S5_WIDTH),
                  row(3 * d, COL_GATE // (3 * d)), row(d, 0),
                  pl.BlockSpec((None, 6, d), lambda i, r: (i, 0, 0))] + [const(a) for a in consts],
        out_specs=(row(d, 0), row(d, 0), row(128, 0)),
        compiler_params=_cparams(("arbitrary", "arbitrary")),
        name="merge_stream",
    )(h_lru, proj, o_dn, proj, y_s5, proj, proj, x, mod, *consts)


def _route_kernel(lg_ref, o_ref, ot_ref, cnt_ref):
    tm = lg_ref.shape[0]
    logits = lg_ref[...]
    lane = lax.broadcasted_iota(jnp.int32, logits.shape, 1)
    neg = jnp.float32(-jnp.inf)
    vals, idxs = [], []
    sel = jnp.zeros(logits.shape, F32)
    for _ in range(TOP_K):
        m = jnp.max(logits, axis=-1, keepdims=True)
        idx = jnp.min(jnp.where(logits == m, lane, 128), axis=-1, keepdims=True)
        hit = lane == idx
        logits = jnp.where(hit, neg, logits)
        sel = jnp.where(hit, 1.0, sel)
        vals.append(m)
        idxs.append(idx)
    exps = [jnp.exp(v - vals[0]) for v in vals]
    inv = 1.0 / (exps[0] + exps[1] + exps[2] + exps[3])

    ii = lax.broadcasted_iota(jnp.int32, (tm, tm), 0)
    jj = lax.broadcasted_iota(jnp.int32, (tm, tm), 1)
    before = jnp.dot((ii > jj).astype(BF16), sel.astype(BF16), preferred_element_type=F32)
    cnt = jnp.sum(sel, axis=0, keepdims=True)
    pieces = jnp.floor((cnt + (MOE_PIECE - 1)) * (1.0 / MOE_PIECE))
    ei = lax.broadcasted_iota(jnp.int32, (128, 128), 0)
    ej = lax.broadcasted_iota(jnp.int32, (128, 128), 1)
    start = MOE_PIECE * jnp.dot(jnp.broadcast_to(pieces, (SUBLANES, 128)).astype(BF16), (ei < ej).astype(BF16),
                                preferred_element_type=F32)[0:1]
    pos = before + start
    out = jnp.zeros(logits.shape, F32)
    for k in range(TOP_K):
        row = jnp.sum(jnp.where(lane == idxs[k], pos, 0.0), axis=-1, keepdims=True)
        out = jnp.where(lane == k, idxs[k].astype(F32), out)
        out = jnp.where(lane == TOP_K + k, row, out)
        out = jnp.where(lane == 2 * TOP_K + k, exps[k] * inv, out)
    o_ref[...] = out
    ot_ref[...] = out.T[:ot_ref.shape[0]]
    cnt_ref[...] = jnp.broadcast_to(cnt, cnt_ref.shape)


def route(logits):
    n = logits.shape[0]
    tm = MOE_TILE
    nt = n // tm
    return pl.pallas_call(
        _route_kernel,
        out_shape=(jax.ShapeDtypeStruct((n, 128), F32), jax.ShapeDtypeStruct((nt, 16, tm), F32),
                   jax.ShapeDtypeStruct((nt, SUBLANES, 128), F32)),
        grid=(nt,),
        in_specs=[pl.BlockSpec((tm, 128), lambda i: (i, 0))],
        out_specs=(pl.BlockSpec((tm, 128), lambda i: (i, 0)), pl.BlockSpec((None, 16, tm), lambda i: (i, 0, 0)),
                   pl.BlockSpec((None, SUBLANES, 128), lambda i: (i, 0, 0))),
        compiler_params=_cparams(("arbitrary",)),
        name="route",
    )(logits)


def _piece_copies(pc_ref, loc_ref, goff_ref, base, make_copy, wait):
    for e in range(N_EXPERTS):
        n = pc_ref[base + e]
        loc = loc_ref[base + e]
        goff = goff_ref[base + e]
        big = 16 * MOE_PIECE

        def body(j, carry, loc=loc, goff=goff):
            cp = make_copy(pl.multiple_of(loc + j * big, MOE_PIECE), pl.multiple_of(goff + j * big, MOE_PIECE), big)
            cp.wait() if wait else cp.start()
            return carry

        n_big = lax.shift_right_logical(n, 4)
        lax.fori_loop(0, n_big, body, 0)
        done = n_big * big
        for bit in (8, 4, 2, 1):
            rows = bit * MOE_PIECE
            off = done

            @pl.when((n & bit) != 0)
            def _(off=off, rows=rows, loc=loc, goff=goff):
                cp = make_copy(pl.multiple_of(loc + off, MOE_PIECE), pl.multiple_of(goff + off, MOE_PIECE), rows)
                cp.wait() if wait else cp.start()

            done = done + jnp.where((n & bit) != 0, rows, 0)


def _local_onehot_rows(rtt, c):
    rows = (lax.broadcasted_iota(jnp.int32, (MOE_CHUNK, rtt.shape[1]), 0) + c * MOE_CHUNK).astype(F32)
    return [rows == rtt[TOP_K + k:TOP_K + k + 1] for k in range(TOP_K)]


def _dispatch_kernel(pc_ref, loc_ref, goff_ref, h_ref, rtt_ref, xs_ref, xloc_ref, zero_ref, sem):
    tile = pl.program_id(0)
    n_tiles = pl.num_programs(0)
    rtt = rtt_ref[...]
    h = h_ref[...]
    for c in range(MOE_LOCAL // MOE_CHUNK):
        hit = _local_onehot_rows(rtt, c)
        p = jnp.where(hit[0], 1.0, jnp.where(hit[1], 1.0, jnp.where(hit[2], 1.0, jnp.where(hit[3], 1.0, 0.0))))
        xloc_ref[c * MOE_CHUNK:(c + 1) * MOE_CHUNK, :] = jnp.dot(
            p.astype(BF16), h, preferred_element_type=F32).astype(BF16)
    make = lambda src, dst, rows: pltpu.make_async_copy(
        xloc_ref.at[pl.ds(src, rows)], xs_ref.at[pl.ds(dst, rows)], sem)
    _piece_copies(pc_ref, loc_ref, goff_ref, tile * N_EXPERTS, make, wait=False)
    _piece_copies(pc_ref, loc_ref, goff_ref, tile * N_EXPERTS, make, wait=True)

    @pl.when(tile == n_tiles - 1)
    def _():
        zero_ref[...] = jnp.zeros_like(zero_ref)
        fill = lambda src, dst, rows: pltpu.make_async_copy(
            zero_ref.at[pl.ds(0, rows)], xs_ref.at[pl.ds(dst, rows)], sem)
        _piece_copies(pc_ref, loc_ref, goff_ref, n_tiles * N_EXPERTS, fill, wait=False)
        _piece_copies(pc_ref, loc_ref, goff_ref, n_tiles * N_EXPERTS, fill, wait=True)


def dispatch(h, rtt, pc, loc, goff, n_rows):
    n, d = h.shape
    nt = n // MOE_TILE
    return pl.pallas_call(
        _dispatch_kernel,
        out_shape=jax.ShapeDtypeStruct((n_rows, d), BF16),
        grid_spec=pltpu.PrefetchScalarGridSpec(
            num_scalar_prefetch=3,
            grid=(nt,),
            in_specs=[pl.BlockSpec((MOE_TILE, d), lambda i, *_: (i, 0)),
                      pl.BlockSpec((None, 16, MOE_TILE), lambda i, *_: (i, 0, 0))],
            out_specs=pl.BlockSpec(memory_space=pl.ANY),
            scratch_shapes=[pltpu.VMEM((MOE_LOCAL, d), BF16), pltpu.VMEM((MOE_BLOCK, d), BF16),
                            pltpu.SemaphoreType.DMA]),
        compiler_params=_cparams(("arbitrary",)),
        name="dispatch",
    )(pc, loc, goff, h, rtt)


def _expert_kernel(be_ref, nu_ref, x_ref, w1_ref, b1_ref, w2_ref, b2_ref, o_ref, w1b_ref, w2b_ref):
    i = pl.program_id(0)

    @pl.when(i < nu_ref[0])
    def _():
        @pl.when(jnp.logical_or(i == 0, be_ref[i] != be_ref[jnp.maximum(i - 1, 0)]))
        def _():
            w1b_ref[...] = w1_ref[...].astype(BF16)
            w2b_ref[...] = w2_ref[...].astype(BF16)

        de = w2_ref.shape[0]
        gu = jnp.dot(x_ref[...], w1b_ref[...], preferred_element_type=F32) + b1_ref[...]
        glu = jnp.minimum(gu[:, :de], SWIGLU_LIMIT)
        lin = jnp.clip(gu[:, de:], -SWIGLU_LIMIT, SWIGLU_LIMIT)
        act = glu * _sigmoid(SWIGLU_ALPHA * glu) * (lin + 1.0)
        o_ref[...] = jnp.dot(act.astype(BF16), w2b_ref[...], preferred_element_type=F32) + b2_ref[...]

    @pl.when(i >= nu_ref[0])
    def _():
        o_ref[...] = jnp.zeros_like(o_ref)


def expert_blocks(xs, block_e, n_used, layer, w1, b1, w2, b2):
    rows, d = xs.shape
    n_blocks = rows // MOE_BLOCK
    de = w2.shape[2]
    blk = lambda i, be, nu: (jnp.minimum(i, nu[0] - 1), 0)
    ex = lambda i, be, nu: (layer, be[jnp.minimum(i, nu[0] - 1)], 0, 0)
    return pl.pallas_call(
        _expert_kernel,
        out_shape=jax.ShapeDtypeStruct((rows, d), F32),
        grid_spec=pltpu.PrefetchScalarGridSpec(
            num_scalar_prefetch=2,
            grid=(n_blocks,),
            in_specs=[pl.BlockSpec((MOE_BLOCK, d), blk),
                      pl.BlockSpec((None, None, d, 2 * de), ex),
                      pl.BlockSpec((None, None, 1, 2 * de), ex),
                      pl.BlockSpec((None, None, de, d), ex),
                      pl.BlockSpec((None, None, 1, d), ex)],
            out_specs=pl.BlockSpec((MOE_BLOCK, d), lambda i, be, nu: (i, 0)),
            scratch_shapes=[pltpu.VMEM((d, 2 * de), BF16), pltpu.VMEM((de, d), BF16)]),
        compiler_params=_cparams(("arbitrary",)),
        name="expert_blocks",
    )(block_e, n_used, xs, w1, b1, w2, b2)


def _combine_kernel(pc_ref, loc_ref, goff_ref, yb_ref, rt_ref, rtt_ref, x_ref, gate_ref, gfin_ref, o_ref,
                    yloc_ref, sems, *, tile0, final_norm):
    step = pl.program_id(0)
    tile = step + tile0
    slot = step % 2

    def gather(tl, sl, wait):
        make = lambda loc, glob, rows: pltpu.make_async_copy(
            yb_ref.at[pl.ds(glob, rows)], yloc_ref.at[sl, pl.ds(loc, rows)], sems.at[sl])
        _piece_copies(pc_ref, loc_ref, goff_ref, tl * N_EXPERTS, make, wait=wait)

    @pl.when(step == 0)
    def _():
        yloc_ref[...] = jnp.zeros_like(yloc_ref)
        gather(tile, slot, False)

    @pl.when(step + 1 < pl.num_programs(0))
    def _():
        gather(tile + 1, 1 - slot, False)

    gather(tile, slot, True)

    rt = rt_ref[...]
    rtt = rtt_ref[...]
    tm = rt.shape[0]
    acc = jnp.zeros(x_ref.shape, F32)
    for c in range(MOE_LOCAL // MOE_CHUNK):
        hit = _local_onehot_rows(rtt, c)
        wgt = [rtt[2 * TOP_K + k:2 * TOP_K + k + 1] for k in range(TOP_K)]
        pw = jnp.where(hit[0], wgt[0], jnp.where(hit[1], wgt[1], jnp.where(hit[2], wgt[2],
                                                                           jnp.where(hit[3], wgt[3], 0.0))))
        ys = yloc_ref[slot, c * MOE_CHUNK:(c + 1) * MOE_CHUNK, :] * jnp.sum(pw, axis=-1, keepdims=True)
        cols = (lax.broadcasted_iota(jnp.int32, (tm, MOE_CHUNK), 1) + c * MOE_CHUNK).astype(F32)
        own = [cols == rt[:, TOP_K + k:TOP_K + k + 1] for k in range(TOP_K)]
        pt = jnp.where(own[0], 1.0, jnp.where(own[1], 1.0, jnp.where(own[2], 1.0,
                                                                     jnp.where(own[3], 1.0, 0.0)))).astype(BF16)
        acc = acc + jnp.dot(pt, ys.astype(BF16), preferred_element_type=F32)

    nseg = gate_ref.shape[0]
    seg = tm // nseg
    for r in range(nseg):
        rows = slice(r * seg, (r + 1) * seg)
        x_new = x_ref[rows, :] + gate_ref[r] * acc[rows]
        if final_norm:
            x_new = x_new * lax.rsqrt(jnp.mean(x_new * x_new, axis=-1, keepdims=True) + EPS) * gfin_ref[...]
        o_ref[rows, :] = x_new


def combine(yb, rt, rtt, pc, loc, goff, row0, x, mod, g_final, *, final_norm):
    b, t, d = x.shape
    tm = MOE_TILE
    assert row0 % tm == 0 and (b * t) % tm == 0 and (t % tm == 0 or tm % t == 0)
    tile0 = row0 // tm
    nseg = max(tm // t, 1)
    tiles_per_b = max(t // tm, 1)
    gate = mod[:, 5:6, :]
    out = pl.pallas_call(
        functools.partial(_combine_kernel, tile0=tile0, final_norm=final_norm),
        out_shape=jax.ShapeDtypeStruct((b * t, d), F32),
        grid_spec=pltpu.PrefetchScalarGridSpec(
            num_scalar_prefetch=3,
            grid=(b * t // tm,),
            in_specs=[pl.BlockSpec(memory_space=pl.ANY),
                      pl.BlockSpec((tm, 128), lambda i, *_: (i + tile0, 0)),
                      pl.BlockSpec((None, 16, tm), lambda i, *_: (i + tile0, 0, 0)),
                      pl.BlockSpec((tm, d), lambda i, *_: (i, 0)),
                      pl.BlockSpec((nseg, 1, d), lambda i, *_: (i // tiles_per_b, 0, 0)),
                      pl.BlockSpec((1, d), lambda i, *_: (0, 0))],
            out_specs=pl.BlockSpec((tm, d), lambda i, *_: (i, 0)),
            scratch_shapes=[pltpu.VMEM((2, MOE_LOCAL, d), F32), pltpu.SemaphoreType.DMA((2,))]),
        compiler_params=_cparams(("arbitrary",)),
        name="combine",
    )(pc, loc, goff, yb, rt, rtt, x.reshape(b * t, d), gate, g_final)
    return out.reshape(b, t, d)


def _pad_w_in(w_in):
    d = w_in.shape[0]
    n_ba = 4 * DN_HEADS
    pad = jnp.zeros((d, COL_U - COL_BA - n_ba), w_in.dtype)
    return jnp.concatenate([w_in[:, :COL_BA + n_ba], pad, w_in[:, COL_BA + n_ba:]], axis=1).astype(BF16)


def kernel(x, c, ctx, c_ctx, w_ada, b_ada, g_mix, g_ffn, w_in, lru_conv_w, lru_conv_b, lru_w_gate, lru_b_gate, lru_lam, dn_conv_w, dn_a_log, dn_dt_bias, dn_norm_g, s5_lam_re, s5_lam_im, s5_log_dt, s5_b_re, s5_b_im, s5_c_re, s5_c_im, s5_d, s5_w_glu, s5_b_glu, w_br_a, w_br_b, w_br_c, w_out, w_router, b_router, w_e1, b_e1, w_e2, b_e2, g_final):
    bsz, t, d = x.shape
    n_layers = w_in.shape[0]
    s = jnp.concatenate([c, jnp.broadcast_to(c_ctx[None], (SUBLANES, d))], axis=0)
    mods = ada_modulation(s, w_ada.astype(BF16), b_ada)
    x_lat, x_ctx = x, ctx
    for l in range(n_layers):
        last = l == n_layers - 1
        p = _layer_params(l, g_mix, g_ffn, w_in, lru_conv_w, lru_conv_b, lru_w_gate, lru_b_gate, lru_lam, dn_conv_w,
                          dn_a_log, dn_dt_bias, dn_norm_g, s5_lam_re, s5_lam_im, s5_log_dt, s5_b_re, s5_b_im,
                          s5_c_re, s5_c_im, s5_d, s5_w_glu, s5_b_glu, w_br_a, w_br_b, w_br_c, w_out, w_router,
                          b_router)
        m_lat = mods[l, :bsz].reshape(bsz, 6, d)
        m_ctx = jnp.broadcast_to(mods[l, bsz].reshape(1, 6, d), (bsz, 6, d))
        ctx_out, ctx_states = mix_stream(x_ctx, m_ctx, p, None, x_ctx.shape[1], not last)
        lat_out, _ = mix_stream(x_lat, m_lat, p, ctx_states, GRID_W, True)
        experts = (l, w_e1, b_e1[:, :, None, :], w_e2, b_e2[:, :, None, :])
        if last:
            x_mid, h, lg = lat_out
            routed = moe(h.reshape(-1, d), lg.reshape(-1, 128), *experts)
            x_lat = combine(*routed, 0, x_mid, m_lat, g_final[None], final_norm=True)
        else:
            xc_mid, hc, lgc = ctx_out
            xl_mid, hl, lgl = lat_out
            n_ctx = bsz * x_ctx.shape[1]
            routed = moe(jnp.concatenate([hc.reshape(-1, d), hl.reshape(-1, d)], axis=0),
                         jnp.concatenate([lgc.reshape(-1, 128), lgl.reshape(-1, 128)], axis=0), *experts)
            x_ctx = combine(*routed, 0, xc_mid, m_ctx, g_final[None], final_norm=False)
            x_lat = combine(*routed, n_ctx, xl_mid, m_lat, g_final[None], final_norm=False)
    return x_lat


def _layer_params(l, g_mix, g_ffn, w_in, lru_conv_w, lru_conv_b, lru_w_gate, lru_b_gate, lru_lam, dn_conv_w,
                  dn_a_log, dn_dt_bias, dn_norm_g, s5_lam_re, s5_lam_im, s5_log_dt, s5_b_re, s5_b_im, s5_c_re,
                  s5_c_im, s5_d, s5_w_glu, s5_b_glu, w_br_a, w_br_b, w_br_c, w_out, w_router, b_router):
    s5 = [s5_lam_re[l], s5_lam_im[l], s5_log_dt[l], s5_b_re[l], s5_b_im[l], s5_c_re[l], s5_c_im[l]]
    wr = jnp.zeros((w_router.shape[1], 128), F32).at[:, :N_EXPERTS].set(w_router[l])
    wr1 = wr.astype(BF16)
    return {
        "g_mix": g_mix[l][None], "w_in": _pad_w_in(w_in[l]),
        "lru_conv_w": lru_conv_w[l], "lru_conv_b": lru_conv_b[l][None],
        "lru_wg": [_lru_gate_dense(lru_w_gate[l, dr]) for dr in range(2)],
        "lru_bg": lru_b_gate[l].reshape(2, 1, 2 * LRU_WIDTH), "lru_lam": lru_lam[l][:, None, :],
        "dn_conv_w": dn_conv_w[l], "dn_a_log": _dn_lane_vec(dn_a_log[l]), "dn_dt_bias": _dn_lane_vec(dn_dt_bias[l]),
        "s5": [_s5_params(*(a[dr] for a in s5)) for dr in range(2)],
        "dn_norm_g": dn_norm_g[l][None], "s5_d": s5_d[l][None], "s5_w_glu": s5_w_glu[l].astype(BF16),
        "s5_b_glu": s5_b_glu[l][None], "w_br_a": w_br_a[l].astype(BF16), "w_br_b": w_br_b[l].astype(BF16),
        "w_br_c": w_br_c[l].astype(BF16), "w_out": w_out[l].astype(BF16), "g_ffn": g_ffn[l][None],
        "w_r1": wr1, "w_r2": (wr - wr1.astype(F32)).astype(BF16),
        "b_router": jnp.full((1, 128), -1e30, F32).at[0, :N_EXPERTS].set(b_router[l]),
    }


def mix_stream(x, mod, p, init, seg, emit):
    bsz = x.shape[0]
    if init is None:
        zl = jnp.zeros((bsz, LRU_WIDTH), F32)
        zd = jnp.zeros((bsz, DN_HEADS, DN_HEAD_DIM, DN_HEAD_DIM), F32)
        zs = jnp.zeros((bsz, S5_LANES), F32)
        init = ((zl, zl), (zd, zd), ((zs, zs), (zs, zs)))
    proj = inproj(x, mod, p["g_mix"], p["w_in"], p["dn_conv_w"], seg)

    lru = lambda dr, prev: lru_scan(proj, p["lru_conv_w"], p["lru_conv_b"], p["lru_wg"][dr], p["lru_bg"][dr],
                                    p["lru_lam"][dr], init[0][dr], prev, seg=seg, reverse=dr == 1)
    h_f, lru_f = lru(0, None)
    h_sum, lru_b = lru(1, h_f)

    dn = lambda dr, prev: dn_scan(proj, p["dn_a_log"], p["dn_dt_bias"], init[1][dr], prev, reverse=dr == 1)
    o_f, dn_f = dn(0, None)
    o_sum, dn_b = dn(1, o_f)

    s5 = lambda dr, prev: s5_scan(proj, p["s5"][dr], init[2][dr][0], init[2][dr][1], prev, reverse=dr == 1)
    y_f, s5_fr, s5_fi = s5(0, None)
    y_sum, s5_br, s5_bi = s5(1, y_f)

    states = ((lru_f, lru_b), (dn_f, dn_b), ((s5_fr, s5_fi), (s5_br, s5_bi)))
    if not emit:
        return None, states
    return merge_stream(h_sum, proj, o_sum, y_sum, x, mod, p), states


def moe(h, logits, layer, w1, b1, w2, b2):
    n, d = h.shape
    nt = n // MOE_TILE
    rt, rtt, counts = route(logits)
    cnt = counts[:, 0, :N_EXPERTS].astype(jnp.int32)
    pc = (cnt + MOE_PIECE - 1) // MOE_PIECE
    loc = MOE_PIECE * (jnp.cumsum(pc, axis=1) - pc)
    rows_e = MOE_PIECE * jnp.sum(pc, axis=0)
    padded = ((rows_e + MOE_BLOCK - 1) // MOE_BLOCK) * MOE_BLOCK
    ends_pad = jnp.cumsum(padded)
    goff = (ends_pad - padded)[None, :] + MOE_PIECE * (jnp.cumsum(pc, axis=0) - pc)
    n_blocks = -(-(n * TOP_K + (MOE_PIECE - 1) * N_EXPERTS * nt) // MOE_BLOCK) + N_EXPERTS
    block_start = jnp.arange(n_blocks, dtype=jnp.int32) * MOE_BLOCK
    block_e = jnp.minimum(jnp.sum(ends_pad[None, :] <= block_start[:, None], axis=1), N_EXPERTS - 1).astype(jnp.int32)
    n_used = (ends_pad[-1:] // MOE_BLOCK).astype(jnp.int32)
    region_end = ends_pad.at[N_EXPERTS - 1].set(n_blocks * MOE_BLOCK)
    pc = jnp.concatenate([pc, ((region_end - (ends_pad - padded + rows_e)) // MOE_PIECE)[None, :]], axis=0)
    loc = jnp.concatenate([loc, jnp.zeros((1, N_EXPERTS), loc.dtype)], axis=0)
    goff = jnp.concatenate([goff, (ends_pad - padded + rows_e)[None, :]], axis=0)
    pc, loc, goff = (a.reshape(-1).astype(jnp.int32) for a in (pc, loc, goff))
    xs = dispatch(h, rtt, pc, loc, goff, n_blocks * MOE_BLOCK)
    yb = expert_blocks(xs, block_e, n_used, layer, w1, b1, w2, b2)
    return yb, rt, rtt, pc, loc, goff
```

```python
import functools
import math

import jax
import jax.numpy as jnp
from jax import lax
from jax.experimental import pallas as pl
from jax.experimental.pallas import tpu as pltpu

F32 = jnp.float32
BF16 = jnp.bfloat16

D_MODEL = 1024
EPS = 1e-6
CONV_W = 4
CONV_LEFT = CONV_W // 2
GRID_W = 64

LRU_WIDTH = 512
LRU_BLOCKS = 8
LRU_BLOCK = LRU_WIDTH // LRU_BLOCKS
LRU_C = 8.0

DN_HEAD_DIM = 128
DN_HEADS = 8
DN_WIDTH = DN_HEADS * DN_HEAD_DIM
DN_CHUNK = 64
DN_BATCH = 4
DN_ROWS = 128

S5_WIDTH = 512
S5_GROUP = 16
S5_GROUPS = 32
S5_STATE = 64
S5_LANES = S5_GROUPS * S5_STATE
S5_SPLIT = 4

N_EXPERTS = 32
TOP_K = 4
D_EXPERT = 1024
SWIGLU_LIMIT = 7.0
SWIGLU_ALPHA = 1.702
MOE_BLOCK = 512
MOE_TILE = 512
MOE_PIECE = 16
MOE_LOCAL = MOE_TILE * TOP_K + MOE_PIECE * N_EXPERTS
MOE_CHUNK = 256

COL_AX, COL_AY, COL_Q, COL_K, COL_V, COL_Z = 0, 512, 1024, 2048, 3072, 4096
COL_BA, COL_U, COL_GATE, D_IN_PAD = 5120, 5632, 6144, 9216
D_IN = 8736

SUBLANES = 8
LANES = 128
VMEM_LIMIT = 52 * 1024 * 1024


def _cparams(sem):
    return pltpu.CompilerParams(dimension_semantics=sem, vmem_limit_bytes=VMEM_LIMIT)


def _sigmoid(x):
    return 1.0 / (1.0 + jnp.exp(-x))


def _softplus(x):
    return jnp.maximum(x, 0.0) + jnp.log1p(jnp.exp(-jnp.abs(x)))


def _silu(x):
    return x * _sigmoid(x)


def _gelu_tanh(x):
    return 0.5 * x * (1.0 + jnp.tanh(math.sqrt(2.0 / math.pi) * (x + 0.044715 * x * x * x)))


def _dot(a, b):
    return jnp.dot(a.astype(BF16), b.astype(BF16), preferred_element_type=F32)


def _ada_kernel(s_ref, w_ref, b_ref, o_ref):
    o_ref[...] = _dot(_silu(s_ref[...]), w_ref[...]) + b_ref[...]


def ada_modulation(s, w_ada, b_ada):
    n_layers, d, n = w_ada.shape
    r = s.shape[0]
    tn = 1536
    return pl.pallas_call(
        _ada_kernel,
        out_shape=jax.ShapeDtypeStruct((n_layers, r, n), F32),
        grid=(n_layers, n // tn),
        in_specs=[pl.BlockSpec((r, d), lambda l, j: (0, 0)),
                  pl.BlockSpec((None, d, tn), lambda l, j: (l, 0, j)),
                  pl.BlockSpec((None, 1, tn), lambda l, j: (l, 0, j))],
        out_specs=pl.BlockSpec((None, r, tn), lambda l, j: (l, 0, j)),
        compiler_params=_cparams(("arbitrary", "arbitrary")),
        name="ada_modulation",
    )(s, w_ada, b_ada.reshape(n_layers, 1, n))


def _norm_mod(x, g, shift, scale):
    y = x * lax.rsqrt(jnp.mean(x * x, axis=-1, keepdims=True) + EPS)
    return (y * g) * (1.0 + scale) + shift


def _conv_rows(x, w, seg):
    tc = x.shape[0]
    t_in_seg = lax.broadcasted_iota(jnp.int32, x.shape, 0) % seg
    acc = x * w[CONV_LEFT:CONV_LEFT + 1]
    for j in range(CONV_W):
        off = j - CONV_LEFT
        if off == 0:
            continue
        sh = pltpu.roll(x, (-off) % tc, 0)
        ok = (t_in_seg >= -off) if off < 0 else (t_in_seg < seg - off)
        acc = acc + jnp.where(ok, sh, 0.0) * w[j:j + 1]
    return acc


def _inproj_kernel(x_ref, mod_ref, g_ref, w_ref, cw_ref, o_ref, h_ref, *, seg, qkv_tiles):
    j = pl.program_id(1)

    @pl.when(j == 0)
    def _():
        nseg = mod_ref.shape[0]
        rows_per = x_ref.shape[0] // nseg
        for s in range(nseg):
            rows = slice(s * rows_per, (s + 1) * rows_per)
            mod = mod_ref[s]
            h_ref[rows, :] = _norm_mod(x_ref[rows, :], g_ref[...], mod[0:1], mod[1:2]).astype(BF16)

    is_qkv = jnp.logical_and(j >= qkv_tiles[0], j < qkv_tiles[1])

    @pl.when(is_qkv)
    def _():
        y = jnp.dot(h_ref[...], w_ref[...], preferred_element_type=F32)
        o_ref[...] = _silu(_conv_rows(y, cw_ref[...], seg))

    @pl.when(jnp.logical_not(is_qkv))
    def _():
        o_ref[...] = jnp.dot(h_ref[...], w_ref[...], preferred_element_type=F32)


def inproj(x, mod, g, w_pad, dn_conv_w, seg):
    b, t, d = x.shape
    n = w_pad.shape[1]
    tm = min(b * t, 1024)
    tn = DN_WIDTH
    assert tm % seg == 0 and (b * t) % tm == 0 and (t % tm == 0 or tm % t == 0) and COL_Q % tn == 0
    nseg = max(tm // t, 1)
    tiles_per_b = max(t // tm, 1)
    q0, q1 = COL_Q // tn, COL_Z // tn
    out = pl.pallas_call(
        functools.partial(_inproj_kernel, seg=seg, qkv_tiles=(q0, q1)),
        out_shape=jax.ShapeDtypeStruct((b * t, n), F32),
        grid=(b * t // tm, n // tn),
        in_specs=[pl.BlockSpec((tm, d), lambda r, j: (r, 0)),
                  pl.BlockSpec((nseg, 6, d), lambda r, j: (r // tiles_per_b, 0, 0)),
                  pl.BlockSpec((1, d), lambda r, j: (0, 0)),
                  pl.BlockSpec((d, tn), lambda r, j: (0, j)),
                  pl.BlockSpec((CONV_W, tn), lambda r, j: (0, jnp.clip(j - q0, 0, q1 - q0 - 1)))],
        out_specs=pl.BlockSpec((tm, tn), lambda r, j: (r, j)),
        scratch_shapes=[pltpu.VMEM((tm, d), BF16)],
        compiler_params=_cparams(("arbitrary", "arbitrary")),
        name="inproj",
    )(x.reshape(b * t, d), mod, g, w_pad, dn_conv_w)
    return out.reshape(b, t, n)


def _conv_time_major(x, w, seg):
    tc = x.shape[0]
    t_in_seg = lax.broadcasted_iota(jnp.int32, x.shape, 0) % seg
    acc = x * w[CONV_LEFT:CONV_LEFT + 1][None]
    for j in range(CONV_W):
        off = j - CONV_LEFT
        if off == 0:
            continue
        if off < 0:
            sh = jnp.concatenate([jnp.zeros((-off,) + x.shape[1:], x.dtype), x[:tc + off]], axis=0)
            ok = t_in_seg >= -off
        else:
            sh = jnp.concatenate([x[off:], jnp.zeros((off,) + x.shape[1:], x.dtype)], axis=0)
            ok = t_in_seg < seg - off
        acc = acc + jnp.where(ok, sh, 0.0) * w[j:j + 1][None]
    return acc


def _to_time_major(x_ref, xt_ref):
    bc, tc, w = x_ref.shape
    for j in range(w // LANES):
        for b in range(bc):
            xt_ref[j, pl.ds(b, tc, stride=bc), :] = x_ref[b, :, j * LANES:(j + 1) * LANES]


def _from_time_major(xt_ref, o_ref, prev_ref):
    bc, tc, w = o_ref.shape
    for j in range(w // LANES):
        for b in range(bc):
            v = xt_ref[j, pl.ds(b, tc, stride=bc), :]
            if prev_ref is not None:
                v = v + prev_ref[b, :, j * LANES:(j + 1) * LANES]
            o_ref[b, :, j * LANES:(j + 1) * LANES] = v


def _lru_kernel(x_ref, cw_ref, cb_ref, wg_ref, bg_ref, lam_ref, h0_ref, *rest, seg, reverse, add_prev):
    if add_prev:
        prev_ref, o_ref, hfin_ref, xt_ref, a_ref, b_ref, h_ref = rest
    else:
        prev_ref = None
        o_ref, hfin_ref, xt_ref, a_ref, b_ref, h_ref = rest
    bc, tc, w = x_ref.shape
    step = pl.program_id(1)

    @pl.when(step == 0)
    def _():
        h_ref[...] = h0_ref[...]

    _to_time_major(x_ref, xt_ref)
    x = jnp.concatenate([xt_ref[j] for j in range(w // LANES)], axis=-1).reshape(tc, bc, w)
    xc = _conv_time_major(x, cw_ref[...], seg) + cb_ref[...][None]
    gates = _dot(xc.reshape(tc * bc, w), wg_ref[...]) + bg_ref[...]
    r = _sigmoid(gates[:, :w]).reshape(tc, bc, w)
    i = _sigmoid(gates[:, w:]).reshape(tc, bc, w)
    log_a = (-LRU_C * _softplus(-lam_ref[...]))[None] * r
    a_ref[...] = jnp.exp(log_a)
    b_ref[...] = jnp.sqrt(1.0 - jnp.exp(2.0 * log_a)) * (i * xc)

    def body(s, h):
        t = tc - 1 - s if reverse else s
        h = a_ref[t] * h + b_ref[t]
        rows = pl.ds(pl.multiple_of(t * bc, bc), bc)
        for j in range(w // LANES):
            xt_ref[j, rows, :] = h[:, j * LANES:(j + 1) * LANES]
        return h

    h = lax.fori_loop(0, tc, body, h_ref[...], unroll=8)
    h_ref[...] = h
    hfin_ref[...] = h
    _from_time_major(xt_ref, o_ref, prev_ref)


def lru_scan(proj, conv_w, conv_b, wg, bg, lam, h0, prev, *, seg, reverse):
    b, t, _ = proj.shape
    w = LRU_WIDTH
    tc = min(t, 256)
    bc = SUBLANES
    assert tc % seg == 0 and t % tc == 0 and b % bc == 0
    nt = t // tc
    tix = (lambda s: nt - 1 - s) if reverse else (lambda s: s)
    blk = lambda cb: pl.BlockSpec((bc, tc, w), lambda i, s: (i, tix(s), cb))
    const = lambda shape: pl.BlockSpec(shape, lambda i, s: (0,) * len(shape))
    in_specs = [blk(COL_AX // w), const((CONV_W, w)), const((1, w)), const((w, 2 * w)), const((1, 2 * w)),
                const((1, w)), pl.BlockSpec((bc, w), lambda i, s: (i, 0))]
    args = [proj, conv_w, conv_b, wg, bg, lam, h0]
    if prev is not None:
        in_specs.append(blk(0))
        args.append(prev)
    return pl.pallas_call(
        functools.partial(_lru_kernel, seg=seg, reverse=reverse, add_prev=prev is not None),
        out_shape=(jax.ShapeDtypeStruct((b, t, w), F32), jax.ShapeDtypeStruct((b, w), F32)),
        grid=(b // bc, nt),
        in_specs=in_specs,
        out_specs=(blk(0), pl.BlockSpec((bc, w), lambda i, s: (i, 0))),
        scratch_shapes=[pltpu.VMEM((w // LANES, tc * bc, LANES), F32), pltpu.VMEM((tc, bc, w), F32),
                        pltpu.VMEM((tc, bc, w), F32), pltpu.VMEM((bc, w), F32)],
        compiler_params=_cparams(("arbitrary", "arbitrary")),
        name="lru_scan_bwd" if reverse else "lru_scan_fwd",
    )(*args)


def _lru_gate_dense(w_gate):
    eye = jnp.eye(LRU_BLOCKS, dtype=w_gate.dtype)
    dense = jnp.einsum("gnij,nm->gnimj", w_gate, eye).reshape(2, LRU_WIDTH, LRU_WIDTH)
    return jnp.concatenate([dense[0], dense[1]], axis=1).astype(BF16)


def _s5_kernel(u_ref, bre_ref, bim_ref, cre_ref, cim_ref, ar_ref, ai_ref, h0r_ref, h0i_ref, *rest,
               reverse, add_prev):
    if add_prev:
        prev_ref, o_ref, fr_ref, fi_ref, ut_ref, xr_ref, xi_ref, hr_ref, hi_ref = rest
    else:
        prev_ref = None
        o_ref, fr_ref, fi_ref, ut_ref, xr_ref, xi_ref, hr_ref, hi_ref = rest
    bc, tc, w = u_ref.shape
    blk_in = w // S5_SPLIT
    blk_st = S5_LANES // S5_SPLIT
    assert blk_in == LANES

    @pl.when(pl.program_id(1) == 0)
    def _():
        hr_ref[...] = h0r_ref[...]
        hi_ref[...] = h0i_ref[...]

    _to_time_major(u_ref, ut_ref)
    for j in range(S5_SPLIT):
        uj = ut_ref[j].astype(BF16)
        xr_ref[:, :, j * blk_st:(j + 1) * blk_st] = jnp.dot(
            uj, bre_ref[j], preferred_element_type=F32).reshape(tc, bc, blk_st)
        xi_ref[:, :, j * blk_st:(j + 1) * blk_st] = jnp.dot(
            uj, bim_ref[j], preferred_element_type=F32).reshape(tc, bc, blk_st)

    ar = jnp.broadcast_to(ar_ref[...], (bc, S5_LANES))
    ai = jnp.broadcast_to(ai_ref[...], (bc, S5_LANES))

    def body(s, carry):
        hr, hi = carry
        t = tc - 1 - s if reverse else s
        nr = ar * hr - ai * hi + xr_ref[t]
        ni = ar * hi + ai * hr + xi_ref[t]
        xr_ref[t] = nr
        xi_ref[t] = ni
        return nr, ni

    hr, hi = lax.fori_loop(0, tc, body, (hr_ref[...], hi_ref[...]))
    hr_ref[...] = hr
    hi_ref[...] = hi
    fr_ref[...] = hr
    fi_ref[...] = hi

    for j in range(S5_SPLIT):
        hrj = xr_ref[:, :, j * blk_st:(j + 1) * blk_st].reshape(tc * bc, blk_st).astype(BF16)
        hij = xi_ref[:, :, j * blk_st:(j + 1) * blk_st].reshape(tc * bc, blk_st).astype(BF16)
        ut_ref[j] = (jnp.dot(hrj, cre_ref[j], preferred_element_type=F32)
                     - jnp.dot(hij, cim_ref[j], preferred_element_type=F32))
    _from_time_major(ut_ref, o_ref, prev_ref)


def s5_scan(proj, prm, h0r, h0i, prev, *, reverse):
    b, t, _ = proj.shape
    w = S5_WIDTH
    tc = min(t, 64)
    bc = SUBLANES
    assert t % tc == 0 and b % bc == 0
    nt = t // tc
    tix = (lambda s: nt - 1 - s) if reverse else (lambda s: s)
    blk = lambda cb: pl.BlockSpec((bc, tc, w), lambda i, s: (i, tix(s), cb))
    const = lambda shape: pl.BlockSpec(shape, lambda i, s: (0,) * len(shape))
    st = pl.BlockSpec((bc, S5_LANES), lambda i, s: (i, 0))
    bre, bim, cre, cim, ar, ai = prm
    in_specs = [blk(COL_U // w), const(bre.shape), const(bim.shape), const(cre.shape), const(cim.shape),
                const(ar.shape), const(ai.shape), st, st]
    args = [proj, bre, bim, cre, cim, ar, ai, h0r, h0i]
    if prev is not None:
        in_specs.append(blk(0))
        args.append(prev)
    st_shape = jax.ShapeDtypeStruct((b, S5_LANES), F32)
    return pl.pallas_call(
        functools.partial(_s5_kernel, reverse=reverse, add_prev=prev is not None),
        out_shape=(jax.ShapeDtypeStruct((b, t, w), F32), st_shape, st_shape),
        grid=(b // bc, nt),
        in_specs=in_specs,
        out_specs=(blk(0), st, st),
        scratch_shapes=[pltpu.VMEM((w // LANES, tc * bc, LANES), F32),
                        pltpu.VMEM((tc, bc, S5_LANES), F32), pltpu.VMEM((tc, bc, S5_LANES), F32),
                        pltpu.VMEM((bc, S5_LANES), F32), pltpu.VMEM((bc, S5_LANES), F32)],
        compiler_params=_cparams(("arbitrary", "arbitrary")),
        name="s5_scan_bwd" if reverse else "s5_scan_fwd",
    )(*args)


def _s5_params(lam_re, lam_im, log_dt, b_re, b_im, c_re, c_im):
    dt = jnp.exp(log_dt)[:, None]
    mag = jnp.exp(lam_re * dt)
    ar, ai = mag * jnp.cos(lam_im * dt), mag * jnp.sin(lam_im * dt)
    den = lam_re * lam_re + lam_im * lam_im
    fr = ((ar - 1.0) * lam_re + ai * lam_im) / den
    fi = (ai * lam_re - (ar - 1.0) * lam_im) / den
    bb_re = fr[..., None] * b_re - fi[..., None] * b_im
    bb_im = fr[..., None] * b_im + fi[..., None] * b_re
    gl = S5_GROUPS // S5_SPLIT
    eye = jnp.eye(gl, dtype=F32)

    def pack_in(bb):
        bb = jnp.transpose(bb, (0, 2, 1)).reshape(S5_SPLIT, gl, S5_GROUP, S5_STATE)
        return jnp.einsum("jghp,gm->jghmp", bb, eye).reshape(S5_SPLIT, gl * S5_GROUP, gl * S5_STATE).astype(BF16)

    def pack_out(c):
        c = jnp.transpose(c, (0, 2, 1)).reshape(S5_SPLIT, gl, S5_STATE, S5_GROUP)
        return jnp.einsum("jgph,gm->jgpmh", c, eye).reshape(S5_SPLIT, gl * S5_STATE, gl * S5_GROUP).astype(BF16)

    return (pack_in(bb_re), pack_in(bb_im), pack_out(c_re), pack_out(c_im),
            ar.reshape(1, S5_LANES), ai.reshape(1, S5_LANES))


def _l2norm(x):
    return x * lax.rsqrt(jnp.sum(x * x, axis=-1, keepdims=True) + EPS)


def _dot_nt(a, b):
    return lax.dot_general(a.astype(BF16), b.astype(BF16), (((1,), (1,)), ((), ())), preferred_element_type=F32)


def _dot_tn(a, b):
    return lax.dot_general(a.astype(BF16), b.astype(BF16), (((0,), (0,)), ((), ())), preferred_element_type=F32)


def _split3_dot(m, x):
    x1 = x.astype(BF16)
    r1 = x - x1.astype(F32)
    x2 = r1.astype(BF16)
    x3 = (r1 - x2.astype(F32)).astype(BF16)
    mb = m.astype(BF16)
    dot = lambda v: jnp.dot(mb, v, preferred_element_type=F32)
    return dot(x1) + dot(x2) + dot(x3)


def _dn_kernel(q_ref, k_ref, v_ref, ba_ref, alog_ref, dtb_ref, s0_ref, *rest, reverse, add_prev, dirn):
    if add_prev:
        prev_ref, o_ref, sfin_ref, beta_ref, g_ref, s_ref = rest
    else:
        o_ref, sfin_ref, beta_ref, g_ref, s_ref = rest
    nb, tc, wq = q_ref.shape
    n_chunks = tc // DN_CHUNK
    step = pl.program_id(1)

    @pl.when(step == 0)
    def _():
        s_ref[...] = s0_ref[...]

    ba = ba_ref[...].reshape(nb * tc, ba_ref.shape[-1])
    beta_ref[...] = _sigmoid(ba)
    g_ref[...] = -jnp.exp(alog_ref[...]) * _softplus(ba + dtb_ref[...])

    ii = lax.broadcasted_iota(jnp.int32, (DN_CHUNK, DN_CHUNK), 0)
    jj = lax.broadcasted_iota(jnp.int32, (DN_CHUNK, DN_CHUNK), 1)
    incl = (ii <= jj) if reverse else (ii >= jj)
    strict = (ii < jj) if reverse else (ii > jj)
    eye = (ii == jj).astype(F32)
    tri = incl.astype(F32)
    last = 0 if reverse else DN_CHUNK - 1
    neg = jnp.float32(-1e30)

    def chunk(ci, carry):
        cidx = n_chunks - 1 - ci if reverse else ci
        r0 = pl.multiple_of(cidx * DN_CHUNK, DN_CHUNK)
        rows = pl.ds(r0, DN_CHUNK)
        each = lambda f, *cols: [f(*args) for args in zip(*cols)]
        srows = [pl.ds(pl.multiple_of(bb * tc + r0, DN_CHUNK), DN_CHUNK) for bb in range(nb)]
        g_cum = [_split3_dot(tri, g_ref[sr, :]) for sr in srows]
        g_t = [g.T for g in g_cum]
        e_g = [jnp.exp(g) for g in g_cum]
        e_rest = [jnp.exp(g[last:last + 1] - g) for g in g_cum]
        beta_all = [beta_ref[sr, :] for sr in srows]
        pairs = [(bb, h) for bb in range(nb) for h in range(DN_HEADS)]
        lanes = [slice(h * DN_HEAD_DIM, (h + 1) * DN_HEAD_DIM) for _, h in pairs]
        cb = [dirn * DN_HEADS + h for _, h in pairs]
        cg = [2 * DN_HEADS + c for c in cb]
        q = [_l2norm(q_ref[bb, rows, ln]) * (DN_HEAD_DIM ** -0.5) for (bb, _), ln in zip(pairs, lanes)]
        k = [_l2norm(k_ref[bb, rows, ln]) for (bb, _), ln in zip(pairs, lanes)]
        v = [v_ref[bb, rows, ln] for (bb, _), ln in zip(pairs, lanes)]
        beta = [beta_all[bb][:, c:c + 1] for (bb, _), c in zip(pairs, cb)]
        decay = [jnp.exp(jnp.where(incl, g_cum[bb][:, c:c + 1] - g_t[bb][c:c + 1, :], neg))
                 for (bb, _), c in zip(pairs, cg)]
        nc = DN_CHUNK
        kb = each(lambda a, b: a * b, k, beta)
        kq = each(lambda a, b, c: _dot_nt(jnp.concatenate([a, b], axis=0), c), kb, q, k)
        a_low = each(lambda r, dc: jnp.where(strict, r[:nc] * dc, 0.0), kq, decay)
        attn = each(lambda r, dc: jnp.where(incl, r[nc:] * dc, 0.0), kq, decay)
        x = [eye - a for a in a_low]
        p = [_dot(a, a) for a in a_low]
        for _ in range(4):
            xp = each(lambda a, b: _dot(jnp.concatenate([a, b], axis=0), b), x, p)
            x = each(lambda a, r: a + r[:nc], x, xp)
            p = [r[nc:] for r in xp]
        x = each(lambda a, b: a + _dot(a, b), x, p)
        e_col = [e_g[bb][:, c:c + 1] for (bb, _), c in zip(pairs, cg)]
        uw = each(lambda a, vv, bt, kk, ec: _dot(a, jnp.concatenate([vv * bt, kk * ec], axis=1)),
                  x, v, beta, kb, e_col)
        q_dec = each(lambda a, b: a * b, q, e_col)
        k_dec = [kk * e_rest[bb][:, c:c + 1] for kk, (bb, _), c in zip(k, pairs, cg)]
        e_tot = [e_g[bb][last:last + 1, c:c + 1] for (bb, _), c in zip(pairs, cg)]
        s = [s_ref[bb, h] for bb, h in pairs]
        ws = each(lambda r, qd, ss: _dot(jnp.concatenate([r[:, DN_HEAD_DIM:], qd], axis=0), ss), uw, q_dec, s)
        v_new = each(lambda r, m: r[:, :DN_HEAD_DIM] - m[:nc], uw, ws)
        o = each(lambda m, at, vn: m[nc:] + _dot(at, vn), ws, attn, v_new)
        s_new = each(lambda ss, et, kd, vn: ss * et + _dot_tn(kd, vn), s, e_tot, k_dec, v_new)
        if add_prev:
            o = [oo + prev_ref[bb, rows, ln] for oo, (bb, _), ln in zip(o, pairs, lanes)]
        for (bb, h), ss in zip(pairs, s_new):
            s_ref[bb, h] = ss
        for (bb, _), ln, oo in zip(pairs, lanes, o):
            o_ref[bb, rows, ln] = oo
        return carry

    lax.fori_loop(0, n_chunks, chunk, 0)

    @pl.when(step == pl.num_programs(1) - 1)
    def _():
        sfin_ref[...] = s_ref[...]


def dn_scan(proj, a_log, dt_bias, s0, prev, *, reverse):
    b, t, _ = proj.shape
    tc = min(t, DN_ROWS)
    nb = DN_BATCH
    assert tc % DN_CHUNK == 0 and t % tc == 0 and b % nb == 0
    nt = t // tc
    dirn = 1 if reverse else 0
    tix = (lambda s: nt - 1 - s) if reverse else (lambda s: s)
    wq = DN_WIDTH
    col = lambda cb: pl.BlockSpec((nb, tc, wq), lambda i, s: (i, tix(s), cb))
    vec = pl.BlockSpec((1, 128), lambda i, s: (0, 0))
    st = pl.BlockSpec((nb, DN_HEADS, DN_HEAD_DIM, DN_HEAD_DIM), lambda i, s: (i, 0, 0, 0))
    oblk = pl.BlockSpec((nb, tc, wq), lambda i, s: (i, tix(s), 0))
    in_specs = [col(COL_Q // wq), col(COL_K // wq), col(COL_V // wq),
                pl.BlockSpec((nb, tc, 128), lambda i, s: (i, tix(s), COL_BA // 128)), vec, vec, st]
    args = [proj, proj, proj, proj, a_log, dt_bias, s0]
    if prev is not None:
        in_specs.append(oblk)
        args.append(prev)
    return pl.pallas_call(
        functools.partial(_dn_kernel, reverse=reverse, add_prev=prev is not None, dirn=dirn),
        out_shape=(jax.ShapeDtypeStruct((b, t, wq), F32),
                   jax.ShapeDtypeStruct((b, DN_HEADS, DN_HEAD_DIM, DN_HEAD_DIM), F32)),
        grid=(b // nb, nt),
        in_specs=in_specs,
        out_specs=(oblk, st),
        scratch_shapes=[pltpu.VMEM((nb * tc, 128), F32), pltpu.VMEM((nb * tc, 128), F32),
                        pltpu.VMEM((nb, DN_HEADS, DN_HEAD_DIM, DN_HEAD_DIM), F32)],
        compiler_params=_cparams(("arbitrary", "arbitrary")),
        name="dn_scan_bwd" if reverse else "dn_scan_fwd",
    )(*args)


def _dn_lane_vec(p):
    return jnp.zeros((1, 128), F32).at[0, 2 * DN_HEADS:4 * DN_HEADS].set(p.reshape(-1))


def _merge_kernel(hl_ref, ay_ref, od_ref, z_ref, ys_ref, u_ref, gt_ref, x_ref, mod_ref, ng_ref, sd_ref,
                  wglu_ref, bglu_ref, wa_ref, wb_ref, wc_ref, wo_ref, gf_ref, wr1_ref, wr2_ref, br_ref,
                  xo_ref, h_ref, lg_ref):
    d = x_ref.shape[-1]
    mod = mod_ref[...]
    y_a = hl_ref[...] * _gelu_tanh(ay_ref[...])
    acc = _sigmoid(gt_ref[:, 0:d]) * _dot(y_a, wa_ref[...])

    ng = ng_ref[...]
    heads = []
    for h in range(DN_HEADS):
        lanes = slice(h * DN_HEAD_DIM, (h + 1) * DN_HEAD_DIM)
        o = od_ref[:, lanes]
        o = o * lax.rsqrt(jnp.mean(o * o, axis=-1, keepdims=True) + EPS) * ng
        heads.append((o * _silu(z_ref[:, lanes])).astype(BF16))
    y_b = jnp.concatenate(heads, axis=-1)
    acc = acc + _sigmoid(gt_ref[:, d:2 * d]) * jnp.dot(y_b, wb_ref[...], preferred_element_type=F32)

    y_c = _gelu_tanh(ys_ref[...] + sd_ref[...] * u_ref[...])
    y_c = y_c * _sigmoid(_dot(y_c, wglu_ref[...]) + bglu_ref[...])
    acc = acc + _sigmoid(gt_ref[:, 2 * d:3 * d]) * _dot(y_c, wc_ref[...])

    x_new = x_ref[...] + mod[2:3] * _dot(acc, wo_ref[...])
    xo_ref[...] = x_new
    h = _norm_mod(x_new, gf_ref[...], mod[3:4], mod[4:5])
    h1 = h.astype(BF16)
    h2 = (h - h1.astype(F32)).astype(BF16)
    h_ref[...] = h1
    lg_ref[...] = (jnp.dot(h1, wr1_ref[...], preferred_element_type=F32)
                   + jnp.dot(h1, wr2_ref[...], preferred_element_type=F32)
                   + jnp.dot(h2, wr1_ref[...], preferred_element_type=F32)) + br_ref[...]


def merge_stream(h_lru, proj, o_dn, y_s5, x, mod, p):
    b, t, d = x.shape
    tm = min(t, 256)
    row = lambda w, cb: pl.BlockSpec((None, tm, w), lambda i, r: (i, r, cb))
    const = lambda a: pl.BlockSpec(a.shape, lambda i, r: (0,) * a.ndim)
    consts = [p["dn_norm_g"], p["s5_d"], p["s5_w_glu"], p["s5_b_glu"], p["w_br_a"], p["w_br_b"], p["w_br_c"],
              p["w_out"], p["g_ffn"], p["w_r1"], p["w_r2"], p["b_router"]]
    return pl.pallas_call(
        _merge_kernel,
        out_shape=(jax.ShapeDtypeStruct((b, t, d), F32), jax.ShapeDtypeStruct((b, t, d), BF16),
                   jax.ShapeDtypeStruct((b, t, 128), F32)),
        grid=(b, t // tm),
        in_specs=[row(LRU_WIDTH, 0), row(LRU_WIDTH, COL_AY // LRU_WIDTH), row(DN_WIDTH, 0),
                  row(DN_WIDTH, COL_Z // DN_WIDTH), row(S5_WIDTH, 0), row(S5_WIDTH, COL_U // S5_WIDTH),
                  row(3 * d, COL_GATE // (3 * d)), row(d, 0),
                  pl.BlockSpec((None, 6, d), lambda i, r: (i, 0, 0))] + [const(a) for a in consts],
        out_specs=(row(d, 0), row(d, 0), row(128, 0)),
        compiler_params=_cparams(("arbitrary", "arbitrary")),
        name="merge_stream",
    )(h_lru, proj, o_dn, proj, y_s5, proj, proj, x, mod, *consts)


def _route_kernel(lg_ref, o_ref, ot_ref, cnt_ref):
    tm = lg_ref.shape[0]
    logits = lg_ref[...]
    lane = lax.broadcasted_iota(jnp.int32, logits.shape, 1)
    neg = jnp.float32(-jnp.inf)
    vals, idxs = [], []
    sel = jnp.zeros(logits.shape, F32)
    for _ in range(TOP_K):
        m = jnp.max(logits, axis=-1, keepdims=True)
        idx = jnp.min(jnp.where(logits == m, lane, 128), axis=-1, keepdims=True)
        hit = lane == idx
        logits = jnp.where(hit, neg, logits)
        sel = jnp.where(hit, 1.0, sel)
        vals.append(m)
        idxs.append(idx)
    exps = [jnp.exp(v - vals[0]) for v in vals]
    inv = 1.0 / (exps[0] + exps[1] + exps[2] + exps[3])

    ii = lax.broadcasted_iota(jnp.int32, (tm, tm), 0)
    jj = lax.broadcasted_iota(jnp.int32, (tm, tm), 1)
    before = jnp.dot((ii > jj).astype(BF16), sel.astype(BF16), preferred_element_type=F32)
    cnt = jnp.sum(sel, axis=0, keepdims=True)
    pieces = jnp.floor((cnt + (MOE_PIECE - 1)) * (1.0 / MOE_PIECE))
    ei = lax.broadcasted_iota(jnp.int32, (128, 128), 0)
    ej = lax.broadcasted_iota(jnp.int32, (128, 128), 1)
    start = MOE_PIECE * jnp.dot(jnp.broadcast_to(pieces, (SUBLANES, 128)).astype(BF16), (ei < ej).astype(BF16),
                                preferred_element_type=F32)[0:1]
    pos = before + start
    out = jnp.zeros(logits.shape, F32)
    for k in range(TOP_K):
        row = jnp.sum(jnp.where(lane == idxs[k], pos, 0.0), axis=-1, keepdims=True)
        out = jnp.where(lane == k, idxs[k].astype(F32), out)
        out = jnp.where(lane == TOP_K + k, row, out)
        out = jnp.where(lane == 2 * TOP_K + k, exps[k] * inv, out)
    o_ref[...] = out
    ot_ref[...] = out.T[:ot_ref.shape[0]]
    cnt_ref[...] = jnp.broadcast_to(cnt, cnt_ref.shape)


def route(logits):
    n = logits.shape[0]
    tm = MOE_TILE
    nt = n // tm
    return pl.pallas_call(
        _route_kernel,
        out_shape=(jax.ShapeDtypeStruct((n, 128), F32), jax.ShapeDtypeStruct((nt, 16, tm), F32),
                   jax.ShapeDtypeStruct((nt, SUBLANES, 128), F32)),
        grid=(nt,),
        in_specs=[pl.BlockSpec((tm, 128), lambda i: (i, 0))],
        out_specs=(pl.BlockSpec((tm, 128), lambda i: (i, 0)), pl.BlockSpec((None, 16, tm), lambda i: (i, 0, 0)),
                   pl.BlockSpec((None, SUBLANES, 128), lambda i: (i, 0, 0))),
        compiler_params=_cparams(("arbitrary",)),
        name="route",
    )(logits)


def _piece_copies(pc_ref, loc_ref, goff_ref, base, make_copy, wait):
    for e in range(N_EXPERTS):
        n = pc_ref[base + e]
        loc = loc_ref[base + e]
        goff = goff_ref[base + e]
        big = 16 * MOE_PIECE

        def body(j, carry, loc=loc, goff=goff):
            cp = make_copy(pl.multiple_of(loc + j * big, MOE_PIECE), pl.multiple_of(goff + j * big, MOE_PIECE), big)
            cp.wait() if wait else cp.start()
            return carry

        n_big = lax.shift_right_logical(n, 4)
        lax.fori_loop(0, n_big, body, 0)
        done = n_big * big
        for bit in (8, 4, 2, 1):
            rows = bit * MOE_PIECE
            off = done

            @pl.when((n & bit) != 0)
            def _(off=off, rows=rows, loc=loc, goff=goff):
                cp = make_copy(pl.multiple_of(loc + off, MOE_PIECE), pl.multiple_of(goff + off, MOE_PIECE), rows)
                cp.wait() if wait else cp.start()

            done = done + jnp.where((n & bit) != 0, rows, 0)


def _local_onehot_rows(rtt, c):
    rows = (lax.broadcasted_iota(jnp.int32, (MOE_CHUNK, rtt.shape[1]), 0) + c * MOE_CHUNK).astype(F32)
    return [rows == rtt[TOP_K + k:TOP_K + k + 1] for k in range(TOP_K)]


def _dispatch_kernel(pc_ref, loc_ref, goff_ref, h_ref, rtt_ref, xs_ref, xloc_ref, zero_ref, sem):
    tile = pl.program_id(0)
    n_tiles = pl.num_programs(0)
    rtt = rtt_ref[...]
    h = h_ref[...]
    for c in range(MOE_LOCAL // MOE_CHUNK):
        hit = _local_onehot_rows(rtt, c)
        p = jnp.where(hit[0], 1.0, jnp.where(hit[1], 1.0, jnp.where(hit[2], 1.0, jnp.where(hit[3], 1.0, 0.0))))
        xloc_ref[c * MOE_CHUNK:(c + 1) * MOE_CHUNK, :] = jnp.dot(
            p.astype(BF16), h, preferred_element_type=F32).astype(BF16)
    make = lambda src, dst, rows: pltpu.make_async_copy(
        xloc_ref.at[pl.ds(src, rows)], xs_ref.at[pl.ds(dst, rows)], sem)
    _piece_copies(pc_ref, loc_ref, goff_ref, tile * N_EXPERTS, make, wait=False)
    _piece_copies(pc_ref, loc_ref, goff_ref, tile * N_EXPERTS, make, wait=True)

    @pl.when(tile == n_tiles - 1)
    def _():
        zero_ref[...] = jnp.zeros_like(zero_ref)
        fill = lambda src, dst, rows: pltpu.make_async_copy(
            zero_ref.at[pl.ds(0, rows)], xs_ref.at[pl.ds(dst, rows)], sem)
        _piece_copies(pc_ref, loc_ref, goff_ref, n_tiles * N_EXPERTS, fill, wait=False)
        _piece_copies(pc_ref, loc_ref, goff_ref, n_tiles * N_EXPERTS, fill, wait=True)


def dispatch(h, rtt, pc, loc, goff, n_rows):
    n, d = h.shape
    nt = n // MOE_TILE
    return pl.pallas_call(
        _dispatch_kernel,
        out_shape=jax.ShapeDtypeStruct((n_rows, d), BF16),
        grid_spec=pltpu.PrefetchScalarGridSpec(
            num_scalar_prefetch=3,
            grid=(nt,),
            in_specs=[pl.BlockSpec((MOE_TILE, d), lambda i, *_: (i, 0)),
                      pl.BlockSpec((None, 16, MOE_TILE), lambda i, *_: (i, 0, 0))],
            out_specs=pl.BlockSpec(memory_space=pl.ANY),
            scratch_shapes=[pltpu.VMEM((MOE_LOCAL, d), BF16), pltpu.VMEM((MOE_BLOCK, d), BF16),
                            pltpu.SemaphoreType.DMA]),
        compiler_params=_cparams(("arbitrary",)),
        name="dispatch",
    )(pc, loc, goff, h, rtt)


def _expert_kernel(be_ref, nu_ref, x_ref, w1_ref, b1_ref, w2_ref, b2_ref, o_ref, w1b_ref, w2b_ref):
    i = pl.program_id(0)

    @pl.when(i < nu_ref[0])
    def _():
        @pl.when(jnp.logical_or(i == 0, be_ref[i] != be_ref[jnp.maximum(i - 1, 0)]))
        def _():
            w1b_ref[...] = w1_ref[...].astype(BF16)
            w2b_ref[...] = w2_ref[...].astype(BF16)

        de = w2_ref.shape[0]
        gu = jnp.dot(x_ref[...], w1b_ref[...], preferred_element_type=F32) + b1_ref[...]
        glu = jnp.minimum(gu[:, :de], SWIGLU_LIMIT)
        lin = jnp.clip(gu[:, de:], -SWIGLU_LIMIT, SWIGLU_LIMIT)
        act = glu * _sigmoid(SWIGLU_ALPHA * glu) * (lin + 1.0)
        o_ref[...] = jnp.dot(act.astype(BF16), w2b_ref[...], preferred_element_type=F32) + b2_ref[...]

    @pl.when(i >= nu_ref[0])
    def _():
        o_ref[...] = jnp.zeros_like(o_ref)


def expert_blocks(xs, block_e, n_used, layer, w1, b1, w2, b2):
    rows, d = xs.shape
    n_blocks = rows // MOE_BLOCK
    de = w2.shape[2]
    blk = lambda i, be, nu: (jnp.minimum(i, nu[0] - 1), 0)
    ex = lambda i, be, nu: (layer, be[jnp.minimum(i, nu[0] - 1)], 0, 0)
    return pl.pallas_call(
        _expert_kernel,
        out_shape=jax.ShapeDtypeStruct((rows, d), F32),
        grid_spec=pltpu.PrefetchScalarGridSpec(
            num_scalar_prefetch=2,
            grid=(n_blocks,),
            in_specs=[pl.BlockSpec((MOE_BLOCK, d), blk),
                      pl.BlockSpec((None, None, d, 2 * de), ex),
                      pl.BlockSpec((None, None, 1, 2 * de), ex),
                      pl.BlockSpec((None, None, de, d), ex),
                      pl.BlockSpec((None, None, 1, d), ex)],
            out_specs=pl.BlockSpec((MOE_BLOCK, d), lambda i, be, nu: (i, 0)),
            scratch_shapes=[pltpu.VMEM((d, 2 * de), BF16), pltpu.VMEM((de, d), BF16)]),
        compiler_params=_cparams(("arbitrary",)),
        name="expert_blocks",
    )(block_e, n_used, xs, w1, b1, w2, b2)


def _combine_kernel(pc_ref, loc_ref, goff_ref, yb_ref, rt_ref, rtt_ref, x_ref, gate_ref, gfin_ref, o_ref,
                    yloc_ref, sems, *, tile0, final_norm):
    step = pl.program_id(0)
    tile = step + tile0
    slot = step % 2

    def gather(tl, sl, wait):
        make = lambda loc, glob, rows: pltpu.make_async_copy(
            yb_ref.at[pl.ds(glob, rows)], yloc_ref.at[sl, pl.ds(loc, rows)], sems.at[sl])
        _piece_copies(pc_ref, loc_ref, goff_ref, tl * N_EXPERTS, make, wait=wait)

    @pl.when(step == 0)
    def _():
        yloc_ref[...] = jnp.zeros_like(yloc_ref)
        gather(tile, slot, False)

    @pl.when(step + 1 < pl.num_programs(0))
    def _():
        gather(tile + 1, 1 - slot, False)

    gather(tile, slot, True)

    rt = rt_ref[...]
    rtt = rtt_ref[...]
    tm = rt.shape[0]
    acc = jnp.zeros(x_ref.shape, F32)
    for c in range(MOE_LOCAL // MOE_CHUNK):
        hit = _local_onehot_rows(rtt, c)
        wgt = [rtt[2 * TOP_K + k:2 * TOP_K + k + 1] for k in range(TOP_K)]
        pw = jnp.where(hit[0], wgt[0], jnp.where(hit[1], wgt[1], jnp.where(hit[2], wgt[2],
                                                                           jnp.where(hit[3], wgt[3], 0.0))))
        ys = yloc_ref[slot, c * MOE_CHUNK:(c + 1) * MOE_CHUNK, :] * jnp.sum(pw, axis=-1, keepdims=True)
        cols = (lax.broadcasted_iota(jnp.int32, (tm, MOE_CHUNK), 1) + c * MOE_CHUNK).astype(F32)
        own = [cols == rt[:, TOP_K + k:TOP_K + k + 1] for k in range(TOP_K)]
        pt = jnp.where(own[0], 1.0, jnp.where(own[1], 1.0, jnp.where(own[2], 1.0,
                                                                     jnp.where(own[3], 1.0, 0.0)))).astype(BF16)
        acc = acc + jnp.dot(pt, ys.astype(BF16), preferred_element_type=F32)

    nseg = gate_ref.shape[0]
    seg = tm // nseg
    for r in range(nseg):
        rows = slice(r * seg, (r + 1) * seg)
        x_new = x_ref[rows, :] + gate_ref[r] * acc[rows]
        if final_norm:
            x_new = x_new * lax.rsqrt(jnp.mean(x_new * x_new, axis=-1, keepdims=True) + EPS) * gfin_ref[...]
        o_ref[rows, :] = x_new


def combine(yb, rt, rtt, pc, loc, goff, row0, x, mod, g_final, *, final_norm):
    b, t, d = x.shape
    tm = MOE_TILE
    assert row0 % tm == 0 and (b * t) % tm == 0 and (t % tm == 0 or tm % t == 0)
    tile0 = row0 // tm
    nseg = max(tm // t, 1)
    tiles_per_b = max(t // tm, 1)
    gate = mod[:, 5:6, :]
    out = pl.pallas_call(
        functools.partial(_combine_kernel, tile0=tile0, final_norm=final_norm),
        out_shape=jax.ShapeDtypeStruct((b * t, d), F32),
        grid_spec=pltpu.PrefetchScalarGridSpec(
            num_scalar_prefetch=3,
            grid=(b * t // tm,),
            in_specs=[pl.BlockSpec(memory_space=pl.ANY),
                      pl.BlockSpec((tm, 128), lambda i, *_: (i + tile0, 0)),
                      pl.BlockSpec((None, 16, tm), lambda i, *_: (i + tile0, 0, 0)),
                      pl.BlockSpec((tm, d), lambda i, *_: (i, 0)),
                      pl.BlockSpec((nseg, 1, d), lambda i, *_: (i // tiles_per_b, 0, 0)),
                      pl.BlockSpec((1, d), lambda i, *_: (0, 0))],
            out_specs=pl.BlockSpec((tm, d), lambda i, *_: (i, 0)),
            scratch_shapes=[pltpu.VMEM((2, MOE_LOCAL, d), F32), pltpu.SemaphoreType.DMA((2,))]),
        compiler_params=_cparams(("arbitrary",)),
        name="combine",
    )(pc, loc, goff, yb, rt, rtt, x.reshape(b * t, d), gate, g_final)
    return out.reshape(b, t, d)


def _pad_w_in(w_in):
    d = w_in.shape[0]
    n_ba = 4 * DN_HEADS
    pad = jnp.zeros((d, COL_U - COL_BA - n_ba), w_in.dtype)
    return jnp.concatenate([w_in[:, :COL_BA + n_ba], pad, w_in[:, COL_BA + n_ba:]], axis=1).astype(BF16)


def kernel(x, c, ctx, c_ctx, w_ada, b_ada, g_mix, g_ffn, w_in, lru_conv_w, lru_conv_b, lru_w_gate, lru_b_gate, lru_lam, dn_conv_w, dn_a_log, dn_dt_bias, dn_norm_g, s5_lam_re, s5_lam_im, s5_log_dt, s5_b_re, s5_b_im, s5_c_re, s5_c_im, s5_d, s5_w_glu, s5_b_glu, w_br_a, w_br_b, w_br_c, w_out, w_router, b_router, w_e1, b_e1, w_e2, b_e2, g_final):
    bsz, t, d = x.shape
    n_layers = w_in.shape[0]
    s = jnp.concatenate([c, jnp.broadcast_to(c_ctx[None], (SUBLANES, d))], axis=0)
    mods = ada_modulation(s, w_ada.astype(BF16), b_ada)
    x_lat, x_ctx = x, ctx
    for l in range(n_layers):
        last = l == n_layers - 1
        p = _layer_params(l, g_mix, g_ffn, w_in, lru_conv_w, lru_conv_b, lru_w_gate, lru_b_gate, lru_lam, dn_conv_w,
                          dn_a_log, dn_dt_bias, dn_norm_g, s5_lam_re, s5_lam_im, s5_log_dt, s5_b_re, s5_b_im,
                          s5_c_re, s5_c_im, s5_d, s5_w_glu, s5_b_glu, w_br_a, w_br_b, w_br_c, w_out, w_router,
                          b_router)
        m_lat = mods[l, :bsz].reshape(bsz, 6, d)
        m_ctx = jnp.broadcast_to(mods[l, bsz].reshape(1, 6, d), (bsz, 6, d))
        ctx_out, ctx_states = mix_stream(x_ctx, m_ctx, p, None, x_ctx.shape[1], not last)
        lat_out, _ = mix_stream(x_lat, m_lat, p, ctx_states, GRID_W, True)
        experts = (l, w_e1, b_e1[:, :, None, :], w_e2, b_e2[:, :, None, :])
        if last:
            x_mid, h, lg = lat_out
            routed = moe(h.reshape(-1, d), lg.reshape(-1, 128), *experts)
            x_lat = combine(*routed, 0, x_mid, m_lat, g_final[None], final_norm=True)
        else:
            xc_mid, hc, lgc = ctx_out
            xl_mid, hl, lgl = lat_out
            n_ctx = bsz * x_ctx.shape[1]
            routed = moe(jnp.concatenate([hc.reshape(-1, d), hl.reshape(-1, d)], axis=0),
                         jnp.concatenate([lgc.reshape(-1, 128), lgl.reshape(-1, 128)], axis=0), *experts)
            x_ctx = combine(*routed, 0, xc_mid, m_ctx, g_final[None], final_norm=False)
            x_lat = combine(*routed, n_ctx, xl_mid, m_lat, g_final[None], final_norm=False)
    return x_lat


def _layer_params(l, g_mix, g_ffn, w_in, lru_conv_w, lru_conv_b, lru_w_gate, lru_b_gate, lru_lam, dn_conv_w,
                  dn_a_log, dn_dt_bias, dn_norm_g, s5_lam_re, s5_lam_im, s5_log_dt, s5_b_re, s5_b_im, s5_c_re,
                  s5_c_im, s5_d, s5_w_glu, s5_b_glu, w_br_a, w_br_b, w_br_c, w_out, w_router, b_router):
    s5 = [s5_lam_re[l], s5_lam_im[l], s5_log_dt[l], s5_b_re[l], s5_b_im[l], s5_c_re[l], s5_c_im[l]]
    wr = jnp.zeros((w_router.shape[1], 128), F32).at[:, :N_EXPERTS].set(w_router[l])
    wr1 = wr.astype(BF16)
    return {
        "g_mix": g_mix[l][None], "w_in": _pad_w_in(w_in[l]),
        "lru_conv_w": lru_conv_w[l], "lru_conv_b": lru_conv_b[l][None],
        "lru_wg": [_lru_gate_dense(lru_w_gate[l, dr]) for dr in range(2)],
        "lru_bg": lru_b_gate[l].reshape(2, 1, 2 * LRU_WIDTH), "lru_lam": lru_lam[l][:, None, :],
        "dn_conv_w": dn_conv_w[l], "dn_a_log": _dn_lane_vec(dn_a_log[l]), "dn_dt_bias": _dn_lane_vec(dn_dt_bias[l]),
        "s5": [_s5_params(*(a[dr] for a in s5)) for dr in range(2)],
        "dn_norm_g": dn_norm_g[l][None], "s5_d": s5_d[l][None], "s5_w_glu": s5_w_glu[l].astype(BF16),
        "s5_b_glu": s5_b_glu[l][None], "w_br_a": w_br_a[l].astype(BF16), "w_br_b": w_br_b[l].astype(BF16),
        "w_br_c": w_br_c[l].astype(BF16), "w_out": w_out[l].astype(BF16), "g_ffn": g_ffn[l][None],
        "w_r1": wr1, "w_r2": (wr - wr1.astype(F32)).astype(BF16),
        "b_router": jnp.full((1, 128), -1e30, F32).at[0, :N_EXPERTS].set(b_router[l]),
    }


def mix_stream(x, mod, p, init, seg, emit):
    bsz = x.shape[0]
    if init is None:
        zl = jnp.zeros((bsz, LRU_WIDTH), F32)
        zd = jnp.zeros((bsz, DN_HEADS, DN_HEAD_DIM, DN_HEAD_DIM), F32)
        zs = jnp.zeros((bsz, S5_LANES), F32)
        init = ((zl, zl), (zd, zd), ((zs, zs), (zs, zs)))
    proj = inproj(x, mod, p["g_mix"], p["w_in"], p["dn_conv_w"], seg)

    lru = lambda dr, prev: lru_scan(proj, p["lru_conv_w"], p["lru_conv_b"], p["lru_wg"][dr], p["lru_bg"][dr],
                                    p["lru_lam"][dr], init[0][dr], prev, seg=seg, reverse=dr == 1)
    h_f, lru_f = lru(0, None)
    h_sum, lru_b = lru(1, h_f)

    dn = lambda dr, prev: dn_scan(proj, p["dn_a_log"], p["dn_dt_bias"], init[1][dr], prev, reverse=dr == 1)
    o_f, dn_f = dn(0, None)
    o_sum, dn_b = dn(1, o_f)

    s5 = lambda dr, prev: s5_scan(proj, p["s5"][dr], init[2][dr][0], init[2][dr][1], prev, reverse=dr == 1)
    y_f, s5_fr, s5_fi = s5(0, None)
    y_sum, s5_br, s5_bi = s5(1, y_f)

    states = ((lru_f, lru_b), (dn_f, dn_b), ((s5_fr, s5_fi), (s5_br, s5_bi)))
    if not emit:
        return None, states
    return merge_stream(h_sum, proj, o_sum, y_sum, x, mod, p), states


def moe(h, logits, layer, w1, b1, w2, b2):
    n, d = h.shape
    nt = n // MOE_TILE
    rt, rtt, counts = route(logits)
    cnt = counts[:, 0, :N_EXPERTS].astype(jnp.int32)
    pc = (cnt + MOE_PIECE - 1) // MOE_PIECE
    loc = MOE_PIECE * (jnp.cumsum(pc, axis=1) - pc)
    rows_e = MOE_PIECE * jnp.sum(pc, axis=0)
    padded = ((rows_e + MOE_BLOCK - 1) // MOE_BLOCK) * MOE_BLOCK
    ends_pad = jnp.cumsum(padded)
    goff = (ends_pad - padded)[None, :] + MOE_PIECE * (jnp.cumsum(pc, axis=0) - pc)
    n_blocks = -(-(n * TOP_K + (MOE_PIECE - 1) * N_EXPERTS * nt) // MOE_BLOCK) + N_EXPERTS
    block_start = jnp.arange(n_blocks, dtype=jnp.int32) * MOE_BLOCK
    block_e = jnp.minimum(jnp.sum(ends_pad[None, :] <= block_start[:, None], axis=1), N_EXPERTS - 1).astype(jnp.int32)
    n_used = (ends_pad[-1:] // MOE_BLOCK).astype(jnp.int32)
    region_end = ends_pad.at[N_EXPERTS - 1].set(n_blocks * MOE_BLOCK)
    pc = jnp.concatenate([pc, ((region_end - (ends_pad - padded + rows_e)) // MOE_PIECE)[None, :]], axis=0)
    loc = jnp.concatenate([loc, jnp.zeros((1, N_EXPERTS), loc.dtype)], axis=0)
    goff = jnp.concatenate([goff, (ends_pad - padded + rows_e)[None, :]], axis=0)
    pc, loc, goff = (a.reshape(-1).astype(jnp.int32) for a in (pc, loc, goff))
    xs = dispatch(h, rtt, pc, loc, goff, n_blocks * MOE_BLOCK)
    yb = expert_blocks(xs, block_e, n_used, layer, w1, b1, w2, b2)
    return yb, rt, rtt, pc, loc, goff
```

```python
import functools
import math

import jax
import jax.numpy as jnp
from jax import lax
from jax.experimental import pallas as pl
from jax.experimental.pallas import tpu as pltpu

F32 = jnp.float32
BF16 = jnp.bfloat16

D_MODEL = 1024
EPS = 1e-6
CONV_W = 4
CONV_LEFT = CONV_W // 2
GRID_W = 64

LRU_WIDTH = 512
LRU_BLOCKS = 8
LRU_BLOCK = LRU_WIDTH // LRU_BLOCKS
LRU_C = 8.0

DN_HEAD_DIM = 128
DN_HEADS = 8
DN_WIDTH = DN_HEADS * DN_HEAD_DIM
DN_CHUNK = 64
DN_BATCH = 4
DN_ROWS = 128

S5_WIDTH = 512
S5_GROUP = 16
S5_GROUPS = 32
S5_STATE = 64
S5_LANES = S5_GROUPS * S5_STATE
S5_SPLIT = 4

N_EXPERTS = 32
TOP_K = 4
D_EXPERT = 1024
SWIGLU_LIMIT = 7.0
SWIGLU_ALPHA = 1.702
MOE_BLOCK = 512
MOE_TILE = 512
MOE_PIECE = 16
MOE_LOCAL = MOE_TILE * TOP_K + MOE_PIECE * N_EXPERTS
MOE_CHUNK = 256

COL_AX, COL_AY, COL_Q, COL_K, COL_V, COL_Z = 0, 512, 1024, 2048, 3072, 4096
COL_BA, COL_U, COL_GATE, D_IN_PAD = 5120, 5632, 6144, 9216
D_IN = 8736

SUBLANES = 8
LANES = 128
VMEM_LIMIT = 52 * 1024 * 1024


def _cparams(sem):
    return pltpu.CompilerParams(dimension_semantics=sem, vmem_limit_bytes=VMEM_LIMIT)


def _sigmoid(x):
    return 1.0 / (1.0 + jnp.exp(-x))


def _softplus(x):
    return jnp.maximum(x, 0.0) + jnp.log1p(jnp.exp(-jnp.abs(x)))


def _silu(x):
    return x * _sigmoid(x)


def _gelu_tanh(x):
    return 0.5 * x * (1.0 + jnp.tanh(math.sqrt(2.0 / math.pi) * (x + 0.044715 * x * x * x)))


def _dot(a, b):
    return jnp.dot(a.astype(BF16), b.astype(BF16), preferred_element_type=F32)


def _ada_kernel(s_ref, w_ref, b_ref, o_ref):
    o_ref[...] = _dot(_silu(s_ref[...]), w_ref[...]) + b_ref[...]


def ada_modulation(s, w_ada, b_ada):
    n_layers, d, n = w_ada.shape
    r = s.shape[0]
    tn = 1536
    return pl.pallas_call(
        _ada_kernel,
        out_shape=jax.ShapeDtypeStruct((n_layers, r, n), F32),
        grid=(n_layers, n // tn),
        in_specs=[pl.BlockSpec((r, d), lambda l, j: (0, 0)),
                  pl.BlockSpec((None, d, tn), lambda l, j: (l, 0, j)),
                  pl.BlockSpec((None, 1, tn), lambda l, j: (l, 0, j))],
        out_specs=pl.BlockSpec((None, r, tn), lambda l, j: (l, 0, j)),
        compiler_params=_cparams(("arbitrary", "arbitrary")),
        name="ada_modulation",
    )(s, w_ada, b_ada.reshape(n_layers, 1, n))


def _norm_mod(x, g, shift, scale):
    y = x * lax.rsqrt(jnp.mean(x * x, axis=-1, keepdims=True) + EPS)
    return (y * g) * (1.0 + scale) + shift


def _conv_rows(x, w, seg):
    tc = x.shape[0]
    t_in_seg = lax.broadcasted_iota(jnp.int32, x.shape, 0) % seg
    acc = x * w[CONV_LEFT:CONV_LEFT + 1]
    for j in range(CONV_W):
        off = j - CONV_LEFT
        if off == 0:
            continue
        sh = pltpu.roll(x, (-off) % tc, 0)
        ok = (t_in_seg >= -off) if off < 0 else (t_in_seg < seg - off)
        acc = acc + jnp.where(ok, sh, 0.0) * w[j:j + 1]
    return acc


def _inproj_kernel(x_ref, mod_ref, g_ref, w_ref, cw_ref, o_ref, h_ref, *, seg, qkv_tiles):
    j = pl.program_id(1)

    @pl.when(j == 0)
    def _():
        nseg = mod_ref.shape[0]
        rows_per = x_ref.shape[0] // nseg
        for s in range(nseg):
            rows = slice(s * rows_per, (s + 1) * rows_per)
            mod = mod_ref[s]
            h_ref[rows, :] = _norm_mod(x_ref[rows, :], g_ref[...], mod[0:1], mod[1:2]).astype(BF16)

    is_qkv = jnp.logical_and(j >= qkv_tiles[0], j < qkv_tiles[1])

    @pl.when(is_qkv)
    def _():
        y = jnp.dot(h_ref[...], w_ref[...], preferred_element_type=F32)
        o_ref[...] = _silu(_conv_rows(y, cw_ref[...], seg))

    @pl.when(jnp.logical_not(is_qkv))
    def _():
        o_ref[...] = jnp.dot(h_ref[...], w_ref[...], preferred_element_type=F32)


def inproj(x, mod, g, w_pad, dn_conv_w, seg):
    b, t, d = x.shape
    n = w_pad.shape[1]
    tm = min(b * t, 1024)
    tn = DN_WIDTH
    assert tm % seg == 0 and (b * t) % tm == 0 and (t % tm == 0 or tm % t == 0) and COL_Q % tn == 0
    nseg = max(tm // t, 1)
    tiles_per_b = max(t // tm, 1)
    q0, q1 = COL_Q // tn, COL_Z // tn
    out = pl.pallas_call(
        functools.partial(_inproj_kernel, seg=seg, qkv_tiles=(q0, q1)),
        out_shape=jax.ShapeDtypeStruct((b * t, n), F32),
        grid=(b * t // tm, n // tn),
        in_specs=[pl.BlockSpec((tm, d), lambda r, j: (r, 0)),
                  pl.BlockSpec((nseg, 6, d), lambda r, j: (r // tiles_per_b, 0, 0)),
                  pl.BlockSpec((1, d), lambda r, j: (0, 0)),
                  pl.BlockSpec((d, tn), lambda r, j: (0, j)),
                  pl.BlockSpec((CONV_W, tn), lambda r, j: (0, jnp.clip(j - q0, 0, q1 - q0 - 1)))],
        out_specs=pl.BlockSpec((tm, tn), lambda r, j: (r, j)),
        scratch_shapes=[pltpu.VMEM((tm, d), BF16)],
        compiler_params=_cparams(("arbitrary", "arbitrary")),
        name="inproj",
    )(x.reshape(b * t, d), mod, g, w_pad, dn_conv_w)
    return out.reshape(b, t, n)


def _conv_time_major(x, w, seg):
    tc = x.shape[0]
    t_in_seg = lax.broadcasted_iota(jnp.int32, x.shape, 0) % seg
    acc = x * w[CONV_LEFT:CONV_LEFT + 1][None]
    for j in range(CONV_W):
        off = j - CONV_LEFT
        if off == 0:
            continue
        if off < 0:
            sh = jnp.concatenate([jnp.zeros((-off,) + x.shape[1:], x.dtype), x[:tc + off]], axis=0)
            ok = t_in_seg >= -off
        else:
            sh = jnp.concatenate([x[off:], jnp.zeros((off,) + x.shape[1:], x.dtype)], axis=0)
            ok = t_in_seg < seg - off
        acc = acc + jnp.where(ok, sh, 0.0) * w[j:j + 1][None]
    return acc


def _to_time_major(x_ref, xt_ref):
    bc, tc, w = x_ref.shape
    for j in range(w // LANES):
        for b in range(bc):
            xt_ref[j, pl.ds(b, tc, stride=bc), :] = x_ref[b, :, j * LANES:(j + 1) * LANES]


def _from_time_major(xt_ref, o_ref, prev_ref):
    bc, tc, w = o_ref.shape
    for j in range(w // LANES):
        for b in range(bc):
            v = xt_ref[j, pl.ds(b, tc, stride=bc), :]
            if prev_ref is not None:
                v = v + prev_ref[b, :, j * LANES:(j + 1) * LANES]
            o_ref[b, :, j * LANES:(j + 1) * LANES] = v


def _lru_kernel(x_ref, cw_ref, cb_ref, wg_ref, bg_ref, lam_ref, h0_ref, *rest, seg, reverse, add_prev):
    if add_prev:
        prev_ref, o_ref, hfin_ref, xt_ref, a_ref, b_ref, h_ref = rest
    else:
        prev_ref = None
        o_ref, hfin_ref, xt_ref, a_ref, b_ref, h_ref = rest
    bc, tc, w = x_ref.shape
    step = pl.program_id(1)

    @pl.when(step == 0)
    def _():
        h_ref[...] = h0_ref[...]

    _to_time_major(x_ref, xt_ref)
    x = jnp.concatenate([xt_ref[j] for j in range(w // LANES)], axis=-1).reshape(tc, bc, w)
    xc = _conv_time_major(x, cw_ref[...], seg) + cb_ref[...][None]
    gates = _dot(xc.reshape(tc * bc, w), wg_ref[...]) + bg_ref[...]
    r = _sigmoid(gates[:, :w]).reshape(tc, bc, w)
    i = _sigmoid(gates[:, w:]).reshape(tc, bc, w)
    log_a = (-LRU_C * _softplus(-lam_ref[...]))[None] * r
    a_ref[...] = jnp.exp(log_a)
    b_ref[...] = jnp.sqrt(1.0 - jnp.exp(2.0 * log_a)) * (i * xc)

    def body(s, h):
        t = tc - 1 - s if reverse else s
        h = a_ref[t] * h + b_ref[t]
        rows = pl.ds(pl.multiple_of(t * bc, bc), bc)
        for j in range(w // LANES):
            xt_ref[j, rows, :] = h[:, j * LANES:(j + 1) * LANES]
        return h

    h = lax.fori_loop(0, tc, body, h_ref[...], unroll=8)
    h_ref[...] = h
    hfin_ref[...] = h
    _from_time_major(xt_ref, o_ref, prev_ref)


def lru_scan(proj, conv_w, conv_b, wg, bg, lam, h0, prev, *, seg, reverse):
    b, t, _ = proj.shape
    w = LRU_WIDTH
    tc = min(t, 256)
    bc = SUBLANES
    assert tc % seg == 0 and t % tc == 0 and b % bc == 0
    nt = t // tc
    tix = (lambda s: nt - 1 - s) if reverse else (lambda s: s)
    blk = lambda cb: pl.BlockSpec((bc, tc, w), lambda i, s: (i, tix(s), cb))
    const = lambda shape: pl.BlockSpec(shape, lambda i, s: (0,) * len(shape))
    in_specs = [blk(COL_AX // w), const((CONV_W, w)), const((1, w)), const((w, 2 * w)), const((1, 2 * w)),
                const((1, w)), pl.BlockSpec((bc, w), lambda i, s: (i, 0))]
    args = [proj, conv_w, conv_b, wg, bg, lam, h0]
    if prev is not None:
        in_specs.append(blk(0))
        args.append(prev)
    return pl.pallas_call(
        functools.partial(_lru_kernel, seg=seg, reverse=reverse, add_prev=prev is not None),
        out_shape=(jax.ShapeDtypeStruct((b, t, w), F32), jax.ShapeDtypeStruct((b, w), F32)),
        grid=(b // bc, nt),
        in_specs=in_specs,
        out_specs=(blk(0), pl.BlockSpec((bc, w), lambda i, s: (i, 0))),
        scratch_shapes=[pltpu.VMEM((w // LANES, tc * bc, LANES), F32), pltpu.VMEM((tc, bc, w), F32),
                        pltpu.VMEM((tc, bc, w), F32), pltpu.VMEM((bc, w), F32)],
        compiler_params=_cparams(("arbitrary", "arbitrary")),
        name="lru_scan_bwd" if reverse else "lru_scan_fwd",
    )(*args)


def _lru_gate_dense(w_gate):
    eye = jnp.eye(LRU_BLOCKS, dtype=w_gate.dtype)
    dense = jnp.einsum("gnij,nm->gnimj", w_gate, eye).reshape(2, LRU_WIDTH, LRU_WIDTH)
    return jnp.concatenate([dense[0], dense[1]], axis=1).astype(BF16)


def _s5_kernel(u_ref, bre_ref, bim_ref, cre_ref, cim_ref, ar_ref, ai_ref, h0r_ref, h0i_ref, *rest,
               reverse, add_prev):
    if add_prev:
        prev_ref, o_ref, fr_ref, fi_ref, ut_ref, xr_ref, xi_ref, hr_ref, hi_ref = rest
    else:
        prev_ref = None
        o_ref, fr_ref, fi_ref, ut_ref, xr_ref, xi_ref, hr_ref, hi_ref = rest
    bc, tc, w = u_ref.shape
    blk_in = w // S5_SPLIT
    blk_st = S5_LANES // S5_SPLIT
    assert blk_in == LANES

    @pl.when(pl.program_id(1) == 0)
    def _():
        hr_ref[...] = h0r_ref[...]
        hi_ref[...] = h0i_ref[...]

    _to_time_major(u_ref, ut_ref)
    for j in range(S5_SPLIT):
        uj = ut_ref[j].astype(BF16)
        xr_ref[:, :, j * blk_st:(j + 1) * blk_st] = jnp.dot(
            uj, bre_ref[j], preferred_element_type=F32).reshape(tc, bc, blk_st)
        xi_ref[:, :, j * blk_st:(j + 1) * blk_st] = jnp.dot(
            uj, bim_ref[j], preferred_element_type=F32).reshape(tc, bc, blk_st)

    part = S5_LANES // 4
    for q in range(4):
        ln = slice(q * part, (q + 1) * part)
        ar = jnp.broadcast_to(ar_ref[:, ln], (bc, part))
        ai = jnp.broadcast_to(ai_ref[:, ln], (bc, part))

        def body(s, carry, ar=ar, ai=ai, ln=ln):
            hr, hi = carry
            t = tc - 1 - s if reverse else s
            nr = ar * hr - ai * hi + xr_ref[t, :, ln]
            ni = ar * hi + ai * hr + xi_ref[t, :, ln]
            xr_ref[t, :, ln] = nr
            xi_ref[t, :, ln] = ni
            return nr, ni

        hr, hi = lax.fori_loop(0, tc, body, (hr_ref[:, ln], hi_ref[:, ln]), unroll=2)
        hr_ref[:, ln] = hr
        hi_ref[:, ln] = hi
        fr_ref[:, ln] = hr
        fi_ref[:, ln] = hi

    for j in range(S5_SPLIT):
        hrj = xr_ref[:, :, j * blk_st:(j + 1) * blk_st].reshape(tc * bc, blk_st).astype(BF16)
        hij = xi_ref[:, :, j * blk_st:(j + 1) * blk_st].reshape(tc * bc, blk_st).astype(BF16)
        ut_ref[j] = (jnp.dot(hrj, cre_ref[j], preferred_element_type=F32)
                     - jnp.dot(hij, cim_ref[j], preferred_element_type=F32))
    _from_time_major(ut_ref, o_ref, prev_ref)


def s5_scan(proj, prm, h0r, h0i, prev, *, reverse):
    b, t, _ = proj.shape
    w = S5_WIDTH
    tc = min(t, 64)
    bc = SUBLANES
    assert t % tc == 0 and b % bc == 0
    nt = t // tc
    tix = (lambda s: nt - 1 - s) if reverse else (lambda s: s)
    blk = lambda cb: pl.BlockSpec((bc, tc, w), lambda i, s: (i, tix(s), cb))
    const = lambda shape: pl.BlockSpec(shape, lambda i, s: (0,) * len(shape))
    st = pl.BlockSpec((bc, S5_LANES), lambda i, s: (i, 0))
    bre, bim, cre, cim, ar, ai = prm
    in_specs = [blk(COL_U // w), const(bre.shape), const(bim.shape), const(cre.shape), const(cim.shape),
                const(ar.shape), const(ai.shape), st, st]
    args = [proj, bre, bim, cre, cim, ar, ai, h0r, h0i]
    if prev is not None:
        in_specs.append(blk(0))
        args.append(prev)
    st_shape = jax.ShapeDtypeStruct((b, S5_LANES), F32)
    return pl.pallas_call(
        functools.partial(_s5_kernel, reverse=reverse, add_prev=prev is not None),
        out_shape=(jax.ShapeDtypeStruct((b, t, w), F32), st_shape, st_shape),
        grid=(b // bc, nt),
        in_specs=in_specs,
        out_specs=(blk(0), st, st),
        scratch_shapes=[pltpu.VMEM((w // LANES, tc * bc, LANES), F32),
                        pltpu.VMEM((tc, bc, S5_LANES), F32), pltpu.VMEM((tc, bc, S5_LANES), F32),
                        pltpu.VMEM((bc, S5_LANES), F32), pltpu.VMEM((bc, S5_LANES), F32)],
        compiler_params=_cparams(("arbitrary", "arbitrary")),
        name="s5_scan_bwd" if reverse else "s5_scan_fwd",
    )(*args)


def _s5_params(lam_re, lam_im, log_dt, b_re, b_im, c_re, c_im):
    dt = jnp.exp(log_dt)[:, None]
    mag = jnp.exp(lam_re * dt)
    ar, ai = mag * jnp.cos(lam_im * dt), mag * jnp.sin(lam_im * dt)
    den = lam_re * lam_re + lam_im * lam_im
    fr = ((ar - 1.0) * lam_re + ai * lam_im) / den
    fi = (ai * lam_re - (ar - 1.0) * lam_im) / den
    bb_re = fr[..., None] * b_re - fi[..., None] * b_im
    bb_im = fr[..., None] * b_im + fi[..., None] * b_re
    gl = S5_GROUPS // S5_SPLIT
    eye = jnp.eye(gl, dtype=F32)

    def pack_in(bb):
        bb = jnp.transpose(bb, (0, 2, 1)).reshape(S5_SPLIT, gl, S5_GROUP, S5_STATE)
        return jnp.einsum("jghp,gm->jghmp", bb, eye).reshape(S5_SPLIT, gl * S5_GROUP, gl * S5_STATE).astype(BF16)

    def pack_out(c):
        c = jnp.transpose(c, (0, 2, 1)).reshape(S5_SPLIT, gl, S5_STATE, S5_GROUP)
        return jnp.einsum("jgph,gm->jgpmh", c, eye).reshape(S5_SPLIT, gl * S5_STATE, gl * S5_GROUP).astype(BF16)

    return (pack_in(bb_re), pack_in(bb_im), pack_out(c_re), pack_out(c_im),
            ar.reshape(1, S5_LANES), ai.reshape(1, S5_LANES))


def _l2norm(x):
    return x * lax.rsqrt(jnp.sum(x * x, axis=-1, keepdims=True) + EPS)


def _dot_nt(a, b):
    return lax.dot_general(a.astype(BF16), b.astype(BF16), (((1,), (1,)), ((), ())), preferred_element_type=F32)


def _dot_tn(a, b):
    return lax.dot_general(a.astype(BF16), b.astype(BF16), (((0,), (0,)), ((), ())), preferred_element_type=F32)


def _split3_dot(m, x):
    x1 = x.astype(BF16)
    r1 = x - x1.astype(F32)
    x2 = r1.astype(BF16)
    x3 = (r1 - x2.astype(F32)).astype(BF16)
    mb = m.astype(BF16)
    dot = lambda v: jnp.dot(mb, v, preferred_element_type=F32)
    return dot(x1) + dot(x2) + dot(x3)


def _dn_kernel(q_ref, k_ref, v_ref, ba_ref, alog_ref, dtb_ref, s0_ref, *rest, reverse, add_prev, dirn):
    if add_prev:
        prev_ref, o_ref, sfin_ref, beta_ref, g_ref, s_ref = rest
    else:
        o_ref, sfin_ref, beta_ref, g_ref, s_ref = rest
    nb, tc, wq = q_ref.shape
    n_chunks = tc // DN_CHUNK
    step = pl.program_id(1)

    @pl.when(step == 0)
    def _():
        s_ref[...] = s0_ref[...]

    ba = ba_ref[...].reshape(nb * tc, ba_ref.shape[-1])
    beta_ref[...] = _sigmoid(ba)
    g_ref[...] = -jnp.exp(alog_ref[...]) * _softplus(ba + dtb_ref[...])

    ii = lax.broadcasted_iota(jnp.int32, (DN_CHUNK, DN_CHUNK), 0)
    jj = lax.broadcasted_iota(jnp.int32, (DN_CHUNK, DN_CHUNK), 1)
    incl = (ii <= jj) if reverse else (ii >= jj)
    strict = (ii < jj) if reverse else (ii > jj)
    eye = (ii == jj).astype(F32)
    tri = incl.astype(F32)
    last = 0 if reverse else DN_CHUNK - 1
    neg = jnp.float32(-1e30)

    def chunk(ci, carry):
        cidx = n_chunks - 1 - ci if reverse else ci
        r0 = pl.multiple_of(cidx * DN_CHUNK, DN_CHUNK)
        rows = pl.ds(r0, DN_CHUNK)
        each = lambda f, *cols: [f(*args) for args in zip(*cols)]
        srows = [pl.ds(pl.multiple_of(bb * tc + r0, DN_CHUNK), DN_CHUNK) for bb in range(nb)]
        g_cum = [_split3_dot(tri, g_ref[sr, :]) for sr in srows]
        g_t = [g.T for g in g_cum]
        e_g = [jnp.exp(g) for g in g_cum]
        e_rest = [jnp.exp(g[last:last + 1] - g) for g in g_cum]
        beta_all = [beta_ref[sr, :] for sr in srows]
        pairs = [(bb, h) for bb in range(nb) for h in range(DN_HEADS)]
        lanes = [slice(h * DN_HEAD_DIM, (h + 1) * DN_HEAD_DIM) for _, h in pairs]
        cb = [dirn * DN_HEADS + h for _, h in pairs]
        cg = [2 * DN_HEADS + c for c in cb]
        q = [_l2norm(q_ref[bb, rows, ln]) * (DN_HEAD_DIM ** -0.5) for (bb, _), ln in zip(pairs, lanes)]
        k = [_l2norm(k_ref[bb, rows, ln]) for (bb, _), ln in zip(pairs, lanes)]
        v = [v_ref[bb, rows, ln] for (bb, _), ln in zip(pairs, lanes)]
        beta = [beta_all[bb][:, c:c + 1] for (bb, _), c in zip(pairs, cb)]
        decay = [jnp.exp(jnp.where(incl, g_cum[bb][:, c:c + 1] - g_t[bb][c:c + 1, :], neg))
                 for (bb, _), c in zip(pairs, cg)]
        nc = DN_CHUNK
        kb = each(lambda a, b: a * b, k, beta)
        kq = each(lambda a, b, c: _dot_nt(jnp.concatenate([a, b], axis=0), c), kb, q, k)
        a_low = each(lambda r, dc: jnp.where(strict, r[:nc] * dc, 0.0), kq, decay)
        attn = each(lambda r, dc: jnp.where(incl, r[nc:] * dc, 0.0), kq, decay)
        x = [eye - a for a in a_low]
        p = [_dot(a, a) for a in a_low]
        for _ in range(4):
            xp = each(lambda a, b: _dot(jnp.concatenate([a, b], axis=0), b), x, p)
            x = each(lambda a, r: a + r[:nc], x, xp)
            p = [r[nc:] for r in xp]
        x = each(lambda a, b: a + _dot(a, b), x, p)
        e_col = [e_g[bb][:, c:c + 1] for (bb, _), c in zip(pairs, cg)]
        uw = each(lambda a, vv, bt, kk, ec: _dot(a, jnp.concatenate([vv * bt, kk * ec], axis=1)),
                  x, v, beta, kb, e_col)
        q_dec = each(lambda a, b: a * b, q, e_col)
        k_dec = [kk * e_rest[bb][:, c:c + 1] for kk, (bb, _), c in zip(k, pairs, cg)]
        e_tot = [e_g[bb][last:last + 1, c:c + 1] for (bb, _), c in zip(pairs, cg)]
        s = [s_ref[bb, h] for bb, h in pairs]
        ws = each(lambda r, qd, ss: _dot(jnp.concatenate([r[:, DN_HEAD_DIM:], qd], axis=0), ss), uw, q_dec, s)
        v_new = each(lambda r, m: r[:, :DN_HEAD_DIM] - m[:nc], uw, ws)
        o = each(lambda m, at, vn: m[nc:] + _dot(at, vn), ws, attn, v_new)
        s_new = each(lambda ss, et, kd, vn: ss * et + _dot_tn(kd, vn), s, e_tot, k_dec, v_new)
        if add_prev:
            o = [oo + prev_ref[bb, rows, ln] for oo, (bb, _), ln in zip(o, pairs, lanes)]
        for (bb, h), ss in zip(pairs, s_new):
            s_ref[bb, h] = ss
        for (bb, _), ln, oo in zip(pairs, lanes, o):
            o_ref[bb, rows, ln] = oo
        return carry

    lax.fori_loop(0, n_chunks, chunk, 0)

    @pl.when(step == pl.num_programs(1) - 1)
    def _():
        sfin_ref[...] = s_ref[...]


def dn_scan(proj, a_log, dt_bias, s0, prev, *, reverse):
    b, t, _ = proj.shape
    tc = min(t, DN_ROWS)
    nb = DN_BATCH
    assert tc % DN_CHUNK == 0 and t % tc == 0 and b % nb == 0
    nt = t // tc
    dirn = 1 if reverse else 0
    tix = (lambda s: nt - 1 - s) if reverse else (lambda s: s)
    wq = DN_WIDTH
    col = lambda cb: pl.BlockSpec((nb, tc, wq), lambda i, s: (i, tix(s), cb))
    vec = pl.BlockSpec((1, 128), lambda i, s: (0, 0))
    st = pl.BlockSpec((nb, DN_HEADS, DN_HEAD_DIM, DN_HEAD_DIM), lambda i, s: (i, 0, 0, 0))
    oblk = pl.BlockSpec((nb, tc, wq), lambda i, s: (i, tix(s), 0))
    in_specs = [col(COL_Q // wq), col(COL_K // wq), col(COL_V // wq),
                pl.BlockSpec((nb, tc, 128), lambda i, s: (i, tix(s), COL_BA // 128)), vec, vec, st]
    args = [proj, proj, proj, proj, a_log, dt_bias, s0]
    if prev is not None:
        in_specs.append(oblk)
        args.append(prev)
    return pl.pallas_call(
        functools.partial(_dn_kernel, reverse=reverse, add_prev=prev is not None, dirn=dirn),
        out_shape=(jax.ShapeDtypeStruct((b, t, wq), F32),
                   jax.ShapeDtypeStruct((b, DN_HEADS, DN_HEAD_DIM, DN_HEAD_DIM), F32)),
        grid=(b // nb, nt),
        in_specs=in_specs,
        out_specs=(oblk, st),
        scratch_shapes=[pltpu.VMEM((nb * tc, 128), F32), pltpu.VMEM((nb * tc, 128), F32),
                        pltpu.VMEM((nb, DN_HEADS, DN_HEAD_DIM, DN_HEAD_DIM), F32)],
        compiler_params=_cparams(("arbitrary", "arbitrary")),
        name="dn_scan_bwd" if reverse else "dn_scan_fwd",
    )(*args)


def _dn_lane_vec(p):
    return jnp.zeros((1, 128), F32).at[0, 2 * DN_HEADS:4 * DN_HEADS].set(p.reshape(-1))


def _merge_kernel(hl_ref, ay_ref, od_ref, z_ref, ys_ref, u_ref, gt_ref, x_ref, mod_ref, ng_ref, sd_ref,
                  wglu_ref, bglu_ref, wa_ref, wb_ref, wc_ref, wo_ref, gf_ref, wr1_ref, wr2_ref, br_ref,
                  xo_ref, h_ref, lg_ref):
    d = x_ref.shape[-1]
    mod = mod_ref[...]
    y_a = hl_ref[...] * _gelu_tanh(ay_ref[...])
    acc = _sigmoid(gt_ref[:, 0:d]) * _dot(y_a, wa_ref[...])

    ng = ng_ref[...]
    heads = []
    for h in range(DN_HEADS):
        lanes = slice(h * DN_HEAD_DIM, (h + 1) * DN_HEAD_DIM)
        o = od_ref[:, lanes]
        o = o * lax.rsqrt(jnp.mean(o * o, axis=-1, keepdims=True) + EPS) * ng
        heads.append((o * _silu(z_ref[:, lanes])).astype(BF16))
    y_b = jnp.concatenate(heads, axis=-1)
    acc = acc + _sigmoid(gt_ref[:, d:2 * d]) * jnp.dot(y_b, wb_ref[...], preferred_element_type=F32)

    y_c = _gelu_tanh(ys_ref[...] + sd_ref[...] * u_ref[...])
    y_c = y_c * _sigmoid(_dot(y_c, wglu_ref[...]) + bglu_ref[...])
    acc = acc + _sigmoid(gt_ref[:, 2 * d:3 * d]) * _dot(y_c, wc_ref[...])

    x_new = x_ref[...] + mod[2:3] * _dot(acc, wo_ref[...])
    xo_ref[...] = x_new
    h = _norm_mod(x_new, gf_ref[...], mod[3:4], mod[4:5])
    h1 = h.astype(BF16)
    h2 = (h - h1.astype(F32)).astype(BF16)
    h_ref[...] = h1
    lg_ref[...] = (jnp.dot(h1, wr1_ref[...], preferred_element_type=F32)
                   + jnp.dot(h1, wr2_ref[...], preferred_element_type=F32)
                   + jnp.dot(h2, wr1_ref[...], preferred_element_type=F32)) + br_ref[...]


def merge_stream(h_lru, proj, o_dn, y_s5, x, mod, p):
    b, t, d = x.shape
    tm = min(t, 256)
    row = lambda w, cb: pl.BlockSpec((None, tm, w), lambda i, r: (i, r, cb))
    const = lambda a: pl.BlockSpec(a.shape, lambda i, r: (0,) * a.ndim)
    consts = [p["dn_norm_g"], p["s5_d"], p["s5_w_glu"], p["s5_b_glu"], p["w_br_a"], p["w_br_b"], p["w_br_c"],
              p["w_out"], p["g_ffn"], p["w_r1"], p["w_r2"], p["b_router"]]
    return pl.pallas_call(
        _merge_kernel,
        out_shape=(jax.ShapeDtypeStruct((b, t, d), F32), jax.ShapeDtypeStruct((b, t, d), BF16),
                   jax.ShapeDtypeStruct((b, t, 128), F32)),
        grid=(b, t // tm),
        in_specs=[row(LRU_WIDTH, 0), row(LRU_WIDTH, COL_AY // LRU_WIDTH), row(DN_WIDTH, 0),
                  row(DN_WIDTH, COL_Z // DN_WIDTH), row(S5_WIDTH, 0), row(S5_WIDTH, COL_U // S5_WIDTH),
                  row(3 * d, COL_GATE // (3 * d)), row(d, 0),
                  pl.BlockSpec((None, 6, d), lambda i, r: (i, 0, 0))] + [const(a) for a in consts],
        out_specs=(row(d, 0), row(d, 0), row(128, 0)),
        compiler_params=_cparams(("arbitrary", "arbitrary")),
        name="merge_stream",
    )(h_lru, proj, o_dn, proj, y_s5, proj, proj, x, mod, *consts)


def _route_kernel(lg_ref, o_ref, ot_ref, cnt_ref):
    tm = lg_ref.shape[0]
    logits = lg_ref[...]
    lane = lax.broadcasted_iota(jnp.int32, logits.shape, 1)
    neg = jnp.float32(-jnp.inf)
    vals, idxs = [], []
    sel = jnp.zeros(logits.shape, F32)
    for _ in range(TOP_K):
        m = jnp.max(logits, axis=-1, keepdims=True)
        idx = jnp.min(jnp.where(logits == m, lane, 128), axis=-1, keepdims=True)
        hit = lane == idx
        logits = jnp.where(hit, neg, logits)
        sel = jnp.where(hit, 1.0, sel)
        vals.append(m)
        idxs.append(idx)
    exps = [jnp.exp(v - vals[0]) for v in vals]
    inv = 1.0 / (exps[0] + exps[1] + exps[2] + exps[3])

    ii = lax.broadcasted_iota(jnp.int32, (tm, tm), 0)
    jj = lax.broadcasted_iota(jnp.int32, (tm, tm), 1)
    before = jnp.dot((ii > jj).astype(BF16), sel.astype(BF16), preferred_element_type=F32)
    cnt = jnp.sum(sel, axis=0, keepdims=True)
    pieces = jnp.floor((cnt + (MOE_PIECE - 1)) * (1.0 / MOE_PIECE))
    ei = lax.broadcasted_iota(jnp.int32, (128, 128), 0)
    ej = lax.broadcasted_iota(jnp.int32, (128, 128), 1)
    start = MOE_PIECE * jnp.dot(jnp.broadcast_to(pieces, (SUBLANES, 128)).astype(BF16), (ei < ej).astype(BF16),
                                preferred_element_type=F32)[0:1]
    pos = before + start
    out = jnp.zeros(logits.shape, F32)
    for k in range(TOP_K):
        row = jnp.sum(jnp.where(lane == idxs[k], pos, 0.0), axis=-1, keepdims=True)
        out = jnp.where(lane == k, idxs[k].astype(F32), out)
        out = jnp.where(lane == TOP_K + k, row, out)
        out = jnp.where(lane == 2 * TOP_K + k, exps[k] * inv, out)
    o_ref[...] = out
    ot_ref[...] = out.T[:ot_ref.shape[0]]
    cnt_ref[...] = jnp.broadcast_to(cnt, cnt_ref.shape)


def route(logits):
    n = logits.shape[0]
    tm = MOE_TILE
    nt = n // tm
    return pl.pallas_call(
        _route_kernel,
        out_shape=(jax.ShapeDtypeStruct((n, 128), F32), jax.ShapeDtypeStruct((nt, 16, tm), F32),
                   jax.ShapeDtypeStruct((nt, SUBLANES, 128), F32)),
        grid=(nt,),
        in_specs=[pl.BlockSpec((tm, 128), lambda i: (i, 0))],
        out_specs=(pl.BlockSpec((tm, 128), lambda i: (i, 0)), pl.BlockSpec((None, 16, tm), lambda i: (i, 0, 0)),
                   pl.BlockSpec((None, SUBLANES, 128), lambda i: (i, 0, 0))),
        compiler_params=_cparams(("arbitrary",)),
        name="route",
    )(logits)


def _piece_copies(pc_ref, loc_ref, goff_ref, base, make_copy, wait):
    for e in range(N_EXPERTS):
        n = pc_ref[base + e]
        loc = loc_ref[base + e]
        goff = goff_ref[base + e]
        big = 16 * MOE_PIECE

        def body(j, carry, loc=loc, goff=goff):
            cp = make_copy(pl.multiple_of(loc + j * big, MOE_PIECE), pl.multiple_of(goff + j * big, MOE_PIECE), big)
            cp.wait() if wait else cp.start()
            return carry

        n_big = lax.shift_right_logical(n, 4)
        lax.fori_loop(0, n_big, body, 0)
        done = n_big * big
        for bit in (8, 4, 2, 1):
            rows = bit * MOE_PIECE
            off = done

            @pl.when((n & bit) != 0)
            def _(off=off, rows=rows, loc=loc, goff=goff):
                cp = make_copy(pl.multiple_of(loc + off, MOE_PIECE), pl.multiple_of(goff + off, MOE_PIECE), rows)
                cp.wait() if wait else cp.start()

            done = done + jnp.where((n & bit) != 0, rows, 0)


def _local_onehot_rows(rtt, c):
    rows = (lax.broadcasted_iota(jnp.int32, (MOE_CHUNK, rtt.shape[1]), 0) + c * MOE_CHUNK).astype(F32)
    return [rows == rtt[TOP_K + k:TOP_K + k + 1] for k in range(TOP_K)]


def _dispatch_kernel(pc_ref, loc_ref, goff_ref, h_ref, rtt_ref, xs_ref, xloc_ref, zero_ref, sem):
    tile = pl.program_id(0)
    n_tiles = pl.num_programs(0)
    rtt = rtt_ref[...]
    h = h_ref[...]
    for c in range(MOE_LOCAL // MOE_CHUNK):
        hit = _local_onehot_rows(rtt, c)
        p = jnp.where(hit[0], 1.0, jnp.where(hit[1], 1.0, jnp.where(hit[2], 1.0, jnp.where(hit[3], 1.0, 0.0))))
        xloc_ref[c * MOE_CHUNK:(c + 1) * MOE_CHUNK, :] = jnp.dot(
            p.astype(BF16), h, preferred_element_type=F32).astype(BF16)
    make = lambda src, dst, rows: pltpu.make_async_copy(
        xloc_ref.at[pl.ds(src, rows)], xs_ref.at[pl.ds(dst, rows)], sem)
    _piece_copies(pc_ref, loc_ref, goff_ref, tile * N_EXPERTS, make, wait=False)
    _piece_copies(pc_ref, loc_ref, goff_ref, tile * N_EXPERTS, make, wait=True)

    @pl.when(tile == n_tiles - 1)
    def _():
        zero_ref[...] = jnp.zeros_like(zero_ref)
        fill = lambda src, dst, rows: pltpu.make_async_copy(
            zero_ref.at[pl.ds(0, rows)], xs_ref.at[pl.ds(dst, rows)], sem)
        _piece_copies(pc_ref, loc_ref, goff_ref, n_tiles * N_EXPERTS, fill, wait=False)
        _piece_copies(pc_ref, loc_ref, goff_ref, n_tiles * N_EXPERTS, fill, wait=True)


def dispatch(h, rtt, pc, loc, goff, n_rows):
    n, d = h.shape
    nt = n // MOE_TILE
    return pl.pallas_call(
        _dispatch_kernel,
        out_shape=jax.ShapeDtypeStruct((n_rows, d), BF16),
        grid_spec=pltpu.PrefetchScalarGridSpec(
            num_scalar_prefetch=3,
            grid=(nt,),
            in_specs=[pl.BlockSpec((MOE_TILE, d), lambda i, *_: (i, 0)),
                      pl.BlockSpec((None, 16, MOE_TILE), lambda i, *_: (i, 0, 0))],
            out_specs=pl.BlockSpec(memory_space=pl.ANY),
            scratch_shapes=[pltpu.VMEM((MOE_LOCAL, d), BF16), pltpu.VMEM((MOE_BLOCK, d), BF16),
                            pltpu.SemaphoreType.DMA]),
        compiler_params=_cparams(("arbitrary",)),
        name="dispatch",
    )(pc, loc, goff, h, rtt)


def _expert_kernel(be_ref, nu_ref, x_ref, w1_ref, b1_ref, w2_ref, b2_ref, o_ref, w1b_ref, w2b_ref):
    i = pl.program_id(0)

    @pl.when(i < nu_ref[0])
    def _():
        @pl.when(jnp.logical_or(i == 0, be_ref[i] != be_ref[jnp.maximum(i - 1, 0)]))
        def _():
            w1b_ref[...] = w1_ref[...].astype(BF16)
            w2b_ref[...] = w2_ref[...].astype(BF16)

        de = w2_ref.shape[0]
        gu = jnp.dot(x_ref[...], w1b_ref[...], preferred_element_type=F32) + b1_ref[...]
        glu = jnp.minimum(gu[:, :de], SWIGLU_LIMIT)
        lin = jnp.clip(gu[:, de:], -SWIGLU_LIMIT, SWIGLU_LIMIT)
        act = glu * _sigmoid(SWIGLU_ALPHA * glu) * (lin + 1.0)
        o_ref[...] = jnp.dot(act.astype(BF16), w2b_ref[...], preferred_element_type=F32) + b2_ref[...]

    @pl.when(i >= nu_ref[0])
    def _():
        o_ref[...] = jnp.zeros_like(o_ref)


def expert_blocks(xs, block_e, n_used, layer, w1, b1, w2, b2):
    rows, d = xs.shape
    n_blocks = rows // MOE_BLOCK
    de = w2.shape[2]
    blk = lambda i, be, nu: (jnp.minimum(i, nu[0] - 1), 0)
    ex = lambda i, be, nu: (layer, be[jnp.minimum(i, nu[0] - 1)], 0, 0)
    return pl.pallas_call(
        _expert_kernel,
        out_shape=jax.ShapeDtypeStruct((rows, d), F32),
        grid_spec=pltpu.PrefetchScalarGridSpec(
            num_scalar_prefetch=2,
            grid=(n_blocks,),
            in_specs=[pl.BlockSpec((MOE_BLOCK, d), blk),
                      pl.BlockSpec((None, None, d, 2 * de), ex),
                      pl.BlockSpec((None, None, 1, 2 * de), ex),
                      pl.BlockSpec((None, None, de, d), ex),
                      pl.BlockSpec((None, None, 1, d), ex)],
            out_specs=pl.BlockSpec((MOE_BLOCK, d), lambda i, be, nu: (i, 0)),
            scratch_shapes=[pltpu.VMEM((d, 2 * de), BF16), pltpu.VMEM((de, d), BF16)]),
        compiler_params=_cparams(("arbitrary",)),
        name="expert_blocks",
    )(block_e, n_used, xs, w1, b1, w2, b2)


def _combine_kernel(pc_ref, loc_ref, goff_ref, yb_ref, rt_ref, rtt_ref, x_ref, gate_ref, gfin_ref, o_ref,
                    yloc_ref, sems, *, tile0, final_norm):
    step = pl.program_id(0)
    tile = step + tile0
    slot = step % 2

    def gather(tl, sl, wait):
        make = lambda loc, glob, rows: pltpu.make_async_copy(
            yb_ref.at[pl.ds(glob, rows)], yloc_ref.at[sl, pl.ds(loc, rows)], sems.at[sl])
        _piece_copies(pc_ref, loc_ref, goff_ref, tl * N_EXPERTS, make, wait=wait)

    @pl.when(step == 0)
    def _():
        yloc_ref[...] = jnp.zeros_like(yloc_ref)
        gather(tile, slot, False)

    @pl.when(step + 1 < pl.num_programs(0))
    def _():
        gather(tile + 1, 1 - slot, False)

    gather(tile, slot, True)

    rt = rt_ref[...]
    rtt = rtt_ref[...]
    tm = rt.shape[0]
    acc = jnp.zeros(x_ref.shape, F32)
    for c in range(MOE_LOCAL // MOE_CHUNK):
        hit = _local_onehot_rows(rtt, c)
        wgt = [rtt[2 * TOP_K + k:2 * TOP_K + k + 1] for k in range(TOP_K)]
        pw = jnp.where(hit[0], wgt[0], jnp.where(hit[1], wgt[1], jnp.where(hit[2], wgt[2],
                                                                           jnp.where(hit[3], wgt[3], 0.0))))
        ys = yloc_ref[slot, c * MOE_CHUNK:(c + 1) * MOE_CHUNK, :] * jnp.sum(pw, axis=-1, keepdims=True)
        cols = (lax.broadcasted_iota(jnp.int32, (tm, MOE_CHUNK), 1) + c * MOE_CHUNK).astype(F32)
        own = [cols == rt[:, TOP_K + k:TOP_K + k + 1] for k in range(TOP_K)]
        pt = jnp.where(own[0], 1.0, jnp.where(own[1], 1.0, jnp.where(own[2], 1.0,
                                                                     jnp.where(own[3], 1.0, 0.0)))).astype(BF16)
        acc = acc + jnp.dot(pt, ys.astype(BF16), preferred_element_type=F32)

    nseg = gate_ref.shape[0]
    seg = tm // nseg
    for r in range(nseg):
        rows = slice(r * seg, (r + 1) * seg)
        x_new = x_ref[rows, :] + gate_ref[r] * acc[rows]
        if final_norm:
            x_new = x_new * lax.rsqrt(jnp.mean(x_new * x_new, axis=-1, keepdims=True) + EPS) * gfin_ref[...]
        o_ref[rows, :] = x_new


def combine(yb, rt, rtt, pc, loc, goff, row0, x, mod, g_final, *, final_norm):
    b, t, d = x.shape
    tm = MOE_TILE
    assert row0 % tm == 0 and (b * t) % tm == 0 and (t % tm == 0 or tm % t == 0)
    tile0 = row0 // tm
    nseg = max(tm // t, 1)
    tiles_per_b = max(t // tm, 1)
    gate = mod[:, 5:6, :]
    out = pl.pallas_call(
        functools.partial(_combine_kernel, tile0=tile0, final_norm=final_norm),
        out_shape=jax.ShapeDtypeStruct((b * t, d), F32),
        grid_spec=pltpu.PrefetchScalarGridSpec(
            num_scalar_prefetch=3,
            grid=(b * t // tm,),
            in_specs=[pl.BlockSpec(memory_space=pl.ANY),
                      pl.BlockSpec((tm, 128), lambda i, *_: (i + tile0, 0)),
                      pl.BlockSpec((None, 16, tm), lambda i, *_: (i + tile0, 0, 0)),
                      pl.BlockSpec((tm, d), lambda i, *_: (i, 0)),
                      pl.BlockSpec((nseg, 1, d), lambda i, *_: (i // tiles_per_b, 0, 0)),
                      pl.BlockSpec((1, d), lambda i, *_: (0, 0))],
            out_specs=pl.BlockSpec((tm, d), lambda i, *_: (i, 0)),
            scratch_shapes=[pltpu.VMEM((2, MOE_LOCAL, d), F32), pltpu.SemaphoreType.DMA((2,))]),
        compiler_params=_cparams(("arbitrary",)),
        name="combine",
    )(pc, loc, goff, yb, rt, rtt, x.reshape(b * t, d), gate, g_final)
    return out.reshape(b, t, d)


def _pad_w_in(w_in):
    d = w_in.shape[0]
    n_ba = 4 * DN_HEADS
    pad = jnp.zeros((d, COL_U - COL_BA - n_ba), w_in.dtype)
    return jnp.concatenate([w_in[:, :COL_BA + n_ba], pad, w_in[:, COL_BA + n_ba:]], axis=1).astype(BF16)


def kernel(x, c, ctx, c_ctx, w_ada, b_ada, g_mix, g_ffn, w_in, lru_conv_w, lru_conv_b, lru_w_gate, lru_b_gate, lru_lam, dn_conv_w, dn_a_log, dn_dt_bias, dn_norm_g, s5_lam_re, s5_lam_im, s5_log_dt, s5_b_re, s5_b_im, s5_c_re, s5_c_im, s5_d, s5_w_glu, s5_b_glu, w_br_a, w_br_b, w_br_c, w_out, w_router, b_router, w_e1, b_e1, w_e2, b_e2, g_final):
    bsz, t, d = x.shape
    n_layers = w_in.shape[0]
    s = jnp.concatenate([c, jnp.broadcast_to(c_ctx[None], (SUBLANES, d))], axis=0)
    mods = ada_modulation(s, w_ada.astype(BF16), b_ada)
    x_lat, x_ctx = x, ctx
    for l in range(n_layers):
        last = l == n_layers - 1
        p = _layer_params(l, g_mix, g_ffn, w_in, lru_conv_w, lru_conv_b, lru_w_gate, lru_b_gate, lru_lam, dn_conv_w,
                          dn_a_log, dn_dt_bias, dn_norm_g, s5_lam_re, s5_lam_im, s5_log_dt, s5_b_re, s5_b_im,
                          s5_c_re, s5_c_im, s5_d, s5_w_glu, s5_b_glu, w_br_a, w_br_b, w_br_c, w_out, w_router,
                          b_router)
        m_lat = mods[l, :bsz].reshape(bsz, 6, d)
        m_ctx = jnp.broadcast_to(mods[l, bsz].reshape(1, 6, d), (bsz, 6, d))
        ctx_out, ctx_states = mix_stream(x_ctx, m_ctx, p, None, x_ctx.shape[1], not last)
        lat_out, _ = mix_stream(x_lat, m_lat, p, ctx_states, GRID_W, True)
        experts = (l, w_e1, b_e1[:, :, None, :], w_e2, b_e2[:, :, None, :])
        if last:
            x_mid, h, lg = lat_out
            routed = moe(h.reshape(-1, d), lg.reshape(-1, 128), *experts)
            x_lat = combine(*routed, 0, x_mid, m_lat, g_final[None], final_norm=True)
        else:
            xc_mid, hc, lgc = ctx_out
            xl_mid, hl, lgl = lat_out
            n_ctx = bsz * x_ctx.shape[1]
            routed = moe(jnp.concatenate([hc.reshape(-1, d), hl.reshape(-1, d)], axis=0),
                         jnp.concatenate([lgc.reshape(-1, 128), lgl.reshape(-1, 128)], axis=0), *experts)
            x_ctx = combine(*routed, 0, xc_mid, m_ctx, g_final[None], final_norm=False)
            x_lat = combine(*routed, n_ctx, xl_mid, m_lat, g_final[None], final_norm=False)
    return x_lat


def _layer_params(l, g_mix, g_ffn, w_in, lru_conv_w, lru_conv_b, lru_w_gate, lru_b_gate, lru_lam, dn_conv_w,
                  dn_a_log, dn_dt_bias, dn_norm_g, s5_lam_re, s5_lam_im, s5_log_dt, s5_b_re, s5_b_im, s5_c_re,
                  s5_c_im, s5_d, s5_w_glu, s5_b_glu, w_br_a, w_br_b, w_br_c, w_out, w_router, b_router):
    s5 = [s5_lam_re[l], s5_lam_im[l], s5_log_dt[l], s5_b_re[l], s5_b_im[l], s5_c_re[l], s5_c_im[l]]
    wr = jnp.zeros((w_router.shape[1], 128), F32).at[:, :N_EXPERTS].set(w_router[l])
    wr1 = wr.astype(BF16)
    return {
        "g_mix": g_mix[l][None], "w_in": _pad_w_in(w_in[l]),
        "lru_conv_w": lru_conv_w[l], "lru_conv_b": lru_conv_b[l][None],
        "lru_wg": [_lru_gate_dense(lru_w_gate[l, dr]) for dr in range(2)],
        "lru_bg": lru_b_gate[l].reshape(2, 1, 2 * LRU_WIDTH), "lru_lam": lru_lam[l][:, None, :],
        "dn_conv_w": dn_conv_w[l], "dn_a_log": _dn_lane_vec(dn_a_log[l]), "dn_dt_bias": _dn_lane_vec(dn_dt_bias[l]),
        "s5": [_s5_params(*(a[dr] for a in s5)) for dr in range(2)],
        "dn_norm_g": dn_norm_g[l][None], "s5_d": s5_d[l][None], "s5_w_glu": s5_w_glu[l].astype(BF16),
        "s5_b_glu": s5_b_glu[l][None], "w_br_a": w_br_a[l].astype(BF16), "w_br_b": w_br_b[l].astype(BF16),
        "w_br_c": w_br_c[l].astype(BF16), "w_out": w_out[l].astype(BF16), "g_ffn": g_ffn[l][None],
        "w_r1": wr1, "w_r2": (wr - wr1.astype(F32)).astype(BF16),
        "b_router": jnp.full((1, 128), -1e30, F32).at[0, :N_EXPERTS].set(b_router[l]),
    }


def mix_stream(x, mod, p, init, seg, emit):
    bsz = x.shape[0]
    if init is None:
        zl = jnp.zeros((bsz, LRU_WIDTH), F32)
        zd = jnp.zeros((bsz, DN_HEADS, DN_HEAD_DIM, DN_HEAD_DIM), F32)
        zs = jnp.zeros((bsz, S5_LANES), F32)
        init = ((zl, zl), (zd, zd), ((zs, zs), (zs, zs)))
    proj = inproj(x, mod, p["g_mix"], p["w_in"], p["dn_conv_w"], seg)

    lru = lambda dr, prev: lru_scan(proj, p["lru_conv_w"], p["lru_conv_b"], p["lru_wg"][dr], p["lru_bg"][dr],
                                    p["lru_lam"][dr], init[0][dr], prev, seg=seg, reverse=dr == 1)
    h_f, lru_f = lru(0, None)
    h_sum, lru_b = lru(1, h_f)

    dn = lambda dr, prev: dn_scan(proj, p["dn_a_log"], p["dn_dt_bias"], init[1][dr], prev, reverse=dr == 1)
    o_f, dn_f = dn(0, None)
    o_sum, dn_b = dn(1, o_f)

    s5 = lambda dr, prev: s5_scan(proj, p["s5"][dr], init[2][dr][0], init[2][dr][1], prev, reverse=dr == 1)
    y_f, s5_fr, s5_fi = s5(0, None)
    y_sum, s5_br, s5_bi = s5(1, y_f)

    states = ((lru_f, lru_b), (dn_f, dn_b), ((s5_fr, s5_fi), (s5_br, s5_bi)))
    if not emit:
        return None, states
    return merge_stream(h_sum, proj, o_sum, y_sum, x, mod, p), states


def moe(h, logits, layer, w1, b1, w2, b2):
    n, d = h.shape
    nt = n // MOE_TILE
    rt, rtt, counts = route(logits)
    cnt = counts[:, 0, :N_EXPERTS].astype(jnp.int32)
    pc = (cnt + MOE_PIECE - 1) // MOE_PIECE
    loc = MOE_PIECE * (jnp.cumsum(pc, axis=1) - pc)
    rows_e = MOE_PIECE * jnp.sum(pc, axis=0)
    padded = ((rows_e + MOE_BLOCK - 1) // MOE_BLOCK) * MOE_BLOCK
    ends_pad = jnp.cumsum(padded)
    goff = (ends_pad - padded)[None, :] + MOE_PIECE * (jnp.cumsum(pc, axis=0) - pc)
    n_blocks = -(-(n * TOP_K + (MOE_PIECE - 1) * N_EXPERTS * nt) // MOE_BLOCK) + N_EXPERTS
    block_start = jnp.arange(n_blocks, dtype=jnp.int32) * MOE_BLOCK
    block_e = jnp.minimum(jnp.sum(ends_pad[None, :] <= block_start[:, None], axis=1), N_EXPERTS - 1).astype(jnp.int32)
    n_used = (ends_pad[-1:] // MOE_BLOCK).astype(jnp.int32)
    region_end = ends_pad.at[N_EXPERTS - 1].set(n_blocks * MOE_BLOCK)
    pc = jnp.concatenate([pc, ((region_end - (ends_pad - padded + rows_e)) // MOE_PIECE)[None, :]], axis=0)
    loc = jnp.concatenate([loc, jnp.zeros((1, N_EXPERTS), loc.dtype)], axis=0)
    goff = jnp.concatenate([goff, (ends_pad - padded + rows_e)[None, :]], axis=0)
    pc, loc, goff = (a.reshape(-1).astype(jnp.int32) for a in (pc, loc, goff))
    xs = dispatch(h, rtt, pc, loc, goff, n_blocks * MOE_BLOCK)
    yb = expert_blocks(xs, block_e, n_used, layer, w1, b1, w2, b2)
    return yb, rt, rtt, pc, loc, goff
```
